```python
import jax, jax.numpy as jnp
from jax import lax
import numpy as np

D_MODEL = 2048
BATCH = 1
SEQ = 8192
DEPTH = 2
DEC_BATCH = 32
DEC_SEQ = 4
PAST_LEN = 8192
PAGE_SIZE = 128

HEAD_DIM = 64
N_MIX_HEADS = 24
MIX_W = N_MIX_HEADS * HEAD_DIM
N_MEM_HEADS = 4
MEM_HEAD_DIM = 128
MEM_W = N_MEM_HEADS * MEM_HEAD_DIM
N_MEM = 256
D_FF = 4 * D_MODEL
DECAY_LORA = 64
ICLR_LORA = 64
GATE_LORA = 224
RWKV_PROJ = 3 * MIX_W + DECAY_LORA + ICLR_LORA + GATE_LORA
RWKV_SPLITS = (MIX_W, 2 * MIX_W, 3 * MIX_W, 3 * MIX_W + DECAY_LORA, 3 * MIX_W + DECAY_LORA + ICLR_LORA)
SB_PROJ = 3 * MIX_W
OUT_IN = MIX_W + MEM_W
N_RWKV_LAYERS = (DEPTH + 1) // 2
N_SB_LAYERS = DEPTH // 2
Q_BLOCK = 128
RMS_EPS = 1e-6
GN_EPS = 64e-5
SB_QK_BIAS_NORM = 8.0

kernel_name = 'rwkv7_stickbreak_memory_hybrid_step'


def rmsnorm(x, g):
    xf = x.astype(jnp.float32)
    y = xf * lax.rsqrt(jnp.mean(xf * xf, axis=-1, keepdims=True) + RMS_EPS)
    return (y * g.astype(jnp.float32)).astype(x.dtype)


def mem_kv(mem, g, w):
    kv = rmsnorm(mem, g) @ w
    B = mem.shape[0]
    k = kv[..., :MEM_W].reshape(B, N_MEM, N_MEM_HEADS, MEM_HEAD_DIM)
    v = kv[..., MEM_W:].reshape(B, N_MEM, N_MEM_HEADS, MEM_HEAD_DIM)
    return k, v


def mem_attend(q, mk, mv):
    B, T, _ = q.shape
    qh = q.reshape(B, T, N_MEM_HEADS, MEM_HEAD_DIM)
    s = jnp.einsum('bthd,bmhd->bhtm', qh, mk, preferred_element_type=jnp.float32) * (MEM_HEAD_DIM ** -0.5)
    p = jax.nn.softmax(s, axis=-1)
    o = jnp.einsum('bhtm,bmhd->bthd', p.astype(mv.dtype), mv)
    return o.reshape(B, T, MEM_W)


def rwkv_mix(p, shift_prev, s0, mu, w0, w2, a0, a2, g2, k_k, k_a, r_k, lnx_w, lnx_b):
    B, T, _ = p.shape
    f32 = jnp.float32
    prev = jnp.concatenate([shift_prev[:, None, :].astype(p.dtype), p[:, :-1]], axis=1)
    pm = p + (prev - p) * mu
    r, k, v, xw, xa, xg = jnp.split(pm, RWKV_SPLITS, axis=-1)
    w_log = -jax.nn.softplus(-(w0 + jnp.tanh(xw) @ w2).astype(f32)) - 0.5
    decay = jnp.exp(-jnp.exp(w_log))
    a = jax.nn.sigmoid((a0 + xa @ a2).astype(f32))
    g = jax.nn.sigmoid(xg) @ g2
    hd = lambda t: t.reshape(B, T, N_MIX_HEADS, HEAD_DIM)
    kk = hd((k * k_k).astype(f32))
    kk = kk / jnp.maximum(jnp.sqrt(jnp.sum(kk * kk, axis=-1, keepdims=True)), 1e-12)
    k = k.astype(f32) * (1.0 + (a - 1.0) * k_a.astype(f32))
    rh, kh, vh, wh, ah = hd(r.astype(f32)), hd(k), hd(v.astype(f32)), hd(decay), hd(a)
    seq = tuple(jnp.swapaxes(t, 0, 1) for t in (rh, wh, kh, vh, -kk, kk * ah))

    def step(S, inp):
        r_t, w_t, k_t, v_t, a_t, b_t = inp
        sa = jnp.einsum('bhvk,bhk->bhv', S, a_t)
        S = S * w_t[:, :, None, :] + sa[..., None] * b_t[:, :, None, :] + v_t[..., None] * k_t[:, :, None, :]
        return S, jnp.einsum('bhvk,bhk->bhv', S, r_t)

    s_fin, y = lax.scan(step, s0.astype(f32), seq)
    y = jnp.swapaxes(y, 0, 1)
    m = jnp.mean(y, axis=-1, keepdims=True)
    var = jnp.mean(jnp.square(y - m), axis=-1, keepdims=True)
    yn = ((y - m) * lax.rsqrt(var + GN_EPS)).reshape(B, T, MIX_W) * lnx_w + lnx_b
    bonus = (jnp.sum(rh * kh * r_k.astype(f32), axis=-1, keepdims=True) * vh).reshape(B, T, MIX_W)
    out = ((yn + bonus) * g.astype(f32)).astype(p.dtype)
    return out, p[:, -1], s_fin.astype(s0.dtype)


def sb_weights(z, allowed):
    log_1mb = jnp.where(allowed, jax.nn.log_sigmoid(-z), 0.0)
    suffix = lax.cumsum(log_1mb, axis=z.ndim - 1, reverse=True) - log_1mb
    return jnp.where(allowed, jnp.exp(jax.nn.log_sigmoid(z) + suffix), 0.0)


def sb_prompt(q, k, v):
    B, S = q.shape[:2]
    k_pos = jnp.arange(S)

    def block(i):
        q_blk = lax.dynamic_slice_in_dim(q, i * Q_BLOCK, Q_BLOCK, axis=1)
        q_pos = i * Q_BLOCK + jnp.arange(Q_BLOCK)
        z = jnp.einsum('bqhd,bkhd->bhqk', q_blk, k, preferred_element_type=jnp.float32) * (HEAD_DIM ** -0.5)
        A = sb_weights(z, k_pos[None, :] < q_pos[:, None])
        return jnp.einsum('bhqk,bkhd->bqhd', A.astype(v.dtype), v)

    o = lax.map(block, jnp.arange(S // Q_BLOCK))
    return jnp.moveaxis(o, 0, 1).reshape(B, S, MIX_W)


def sb_sample(q, k, v, k_past, v_past):
    B, T = q.shape[:2]
    P = k_past.shape[1]
    scale = HEAD_DIM ** -0.5
    z = jnp.concatenate([
        jnp.einsum('bqhd,bkhd->bhqk', q, k_past, preferred_element_type=jnp.float32),
        jnp.einsum('bqhd,bkhd->bhqk', q, k, preferred_element_type=jnp.float32)], axis=-1) * scale
    k_pos = jnp.arange(P + T)
    q_pos = P + jnp.arange(T)
    A = sb_weights(z, k_pos[None, :] < q_pos[:, None]).astype(v.dtype)
    o = jnp.einsum('bhqk,bkhd->bqhd', A[..., :P], v_past) + jnp.einsum('bhqk,bkhd->bqhd', A[..., P:], v)
    return o.reshape(B, T, MIX_W)


def ffn(x, g, w_up, w_down):
    h = jax.nn.relu(rmsnorm(x, g) @ w_up)
    return (h * h) @ w_down


def setup_inputs(seed: int = 0) -> dict:
    key = jax.random.key(seed)
    ks = iter(jax.random.split(key, 40))
    nrm = lambda shape, scale: scale * jax.random.normal(next(ks), shape, jnp.float32)
    n_pages = PAST_LEN // PAGE_SIZE
    n_pool = (DEC_BATCH * n_pages * 5) // 4
    page_table = jax.random.permutation(next(ks), n_pool)[: DEC_BATCH * n_pages].reshape(DEC_BATCH, n_pages).astype(jnp.int32)
    NR, NS = N_RWKV_LAYERS, N_SB_LAYERS
    u = nrm((NS, N_MIX_HEADS, HEAD_DIM), 1.0)
    u = u / jnp.sqrt(jnp.sum(u * u, axis=-1, keepdims=True))
    b_q = SB_QK_BIAS_NORM * u
    b_k = -SB_QK_BIAS_NORM * u
    sb_b_qk = jnp.concatenate([b_q.reshape(NS, MIX_W), b_k.reshape(NS, MIX_W)], axis=-1)
    cache_sb_k = nrm((NS, n_pool, PAGE_SIZE, N_MIX_HEADS, HEAD_DIM), 1.0) + b_k[:, None, None]
    return {
        'x_prompt': nrm((BATCH, SEQ, D_MODEL), 1.0),
        'x_sample': nrm((DEC_BATCH, DEC_SEQ, D_MODEL), 1.0),
        'mem_prompt': nrm((BATCH, N_MEM, D_MODEL), 1.0),
        'state_rwkv_wkv': nrm((NR, DEC_BATCH, N_MIX_HEADS, HEAD_DIM, HEAD_DIM), 0.5),
        'state_rwkv_shift': nrm((NR, DEC_BATCH, RWKV_PROJ), 1.0),
        'cache_sb_k': cache_sb_k,
        'cache_sb_v': nrm((NS, n_pool, PAGE_SIZE, N_MIX_HEADS, HEAD_DIM), 1.0),
        'page_table': page_table,
        'cache_mem_k': nrm((DEPTH, DEC_BATCH, N_MEM, N_MEM_HEADS, MEM_HEAD_DIM), 1.0),
        'cache_mem_v': nrm((DEPTH, DEC_BATCH, N_MEM, N_MEM_HEADS, MEM_HEAD_DIM), 1.0),
        'norm_mix': 1.0 + nrm((DEPTH, D_MODEL), 0.02),
        'norm_ffn': 1.0 + nrm((DEPTH, D_MODEL), 0.02),
        'norm_final': 1.0 + nrm((D_MODEL,), 0.02),
        'mem_norm': 1.0 + nrm((DEPTH, D_MODEL), 0.02),
        'w_mem_kv': nrm((DEPTH, D_MODEL, 2 * MEM_W), D_MODEL ** -0.5),
        'w_out': nrm((DEPTH, OUT_IN, D_MODEL), OUT_IN ** -0.5),
        'w_ffn_up': nrm((DEPTH, D_MODEL, D_FF), D_MODEL ** -0.5),
        'w_ffn_down': nrm((DEPTH, D_FF, D_MODEL), D_FF ** -0.5),
        'rwkv_w_in': nrm((NR, D_MODEL, RWKV_PROJ + MEM_W), D_MODEL ** -0.5),
        'rwkv_mu': jax.random.uniform(next(ks), (NR, RWKV_PROJ), jnp.float32),
        'rwkv_w0': -2.0 + nrm((NR, MIX_W), 0.5),
        'rwkv_w2': nrm((NR, DECAY_LORA, MIX_W), 0.5 * DECAY_LORA ** -0.5),
        'rwkv_a0': nrm((NR, MIX_W), 0.1),
        'rwkv_a2': nrm((NR, ICLR_LORA, MIX_W), 0.5 * ICLR_LORA ** -0.5),
        'rwkv_g2': nrm((NR, GATE_LORA, MIX_W), GATE_LORA ** -0.5),
        'rwkv_k_k': 0.85 + nrm((NR, MIX_W), 0.05),
        'rwkv_k_a': 1.0 + nrm((NR, MIX_W), 0.05),
        'rwkv_r_k': nrm((NR, N_MIX_HEADS, HEAD_DIM), 0.5),
        'rwkv_lnx_w': 1.0 + nrm((NR, MIX_W), 0.05),
        'rwkv_lnx_b': nrm((NR, MIX_W), 0.02),
        'sb_w_in': nrm((NS, D_MODEL, SB_PROJ + MEM_W), D_MODEL ** -0.5),
        'sb_b_qk': sb_b_qk,
    }


def reference(x_prompt, x_sample, mem_prompt, state_rwkv_wkv, state_rwkv_shift, cache_sb_k, cache_sb_v,
              page_table, cache_mem_k, cache_mem_v, norm_mix, norm_ffn, norm_final, mem_norm, w_mem_kv,
              w_out, w_ffn_up, w_ffn_down, rwkv_w_in, rwkv_mu, rwkv_w0, rwkv_w2, rwkv_a0, rwkv_a2, rwkv_g2,
              rwkv_k_k, rwkv_k_a, rwkv_r_k, rwkv_lnx_w, rwkv_lnx_b, sb_w_in, sb_b_qk):
    B, S, _ = x_prompt.shape
    DB = x_sample.shape[0]
    P = page_table.shape[1] * PAGE_SIZE
    heads = lambda t: t.reshape(t.shape[0], t.shape[1], N_MIX_HEADS, HEAD_DIM)
    xp, xs = x_prompt, x_sample
    mem_k_p, mem_v_p = [], []
    wkv_p, shift_p, wkv_s, shift_s = [], [], [], []
    sbk_p, sbv_p, sbk_s, sbv_s = [], [], [], []
    for i in range(DEPTH):
        j = i // 2
        hp = rmsnorm(xp, norm_mix[i])
        hs = rmsnorm(xs, norm_mix[i])
        mk, mv = mem_kv(mem_prompt, mem_norm[i], w_mem_kv[i])
        mem_k_p.append(mk)
        mem_v_p.append(mv)
        if i % 2 == 0:
            rw = (rwkv_mu[j], rwkv_w0[j], rwkv_w2[j], rwkv_a0[j], rwkv_a2[j], rwkv_g2[j],
                  rwkv_k_k[j], rwkv_k_a[j], rwkv_r_k[j], rwkv_lnx_w[j], rwkv_lnx_b[j])
            zp = hp @ rwkv_w_in[j]
            zs = hs @ rwkv_w_in[j]
            mix_p, sh_p, st_p = rwkv_mix(zp[..., :RWKV_PROJ], jnp.zeros((B, RWKV_PROJ), xp.dtype),
                                         jnp.zeros((B, N_MIX_HEADS, HEAD_DIM, HEAD_DIM), xp.dtype), *rw)
            mix_s, sh_s, st_s = rwkv_mix(zs[..., :RWKV_PROJ], state_rwkv_shift[j], state_rwkv_wkv[j], *rw)
            wkv_p.append(st_p)
            shift_p.append(sh_p)
            wkv_s.append(st_s)
            shift_s.append(sh_s)
            qp, qs = zp[..., RWKV_PROJ:], zs[..., RWKV_PROJ:]
        else:
            zp = hp @ sb_w_in[j]
            zs = hs @ sb_w_in[j]
            qk_p = zp[..., :2 * MIX_W] + sb_b_qk[j]
            qk_s = zs[..., :2 * MIX_W] + sb_b_qk[j]
            q_p, k_p, v_p = heads(qk_p[..., :MIX_W]), heads(qk_p[..., MIX_W:]), heads(zp[..., 2 * MIX_W:SB_PROJ])
            q_s, k_s, v_s = heads(qk_s[..., :MIX_W]), heads(qk_s[..., MIX_W:]), heads(zs[..., 2 * MIX_W:SB_PROJ])
            mix_p = sb_prompt(q_p, k_p, v_p)
            k_past = cache_sb_k[j][page_table].reshape(DB, P, N_MIX_HEADS, HEAD_DIM)
            v_past = cache_sb_v[j][page_table].reshape(DB, P, N_MIX_HEADS, HEAD_DIM)
            mix_s = sb_sample(q_s, k_s, v_s, k_past, v_past)
            sbk_p.append(k_p.reshape(B, S // PAGE_SIZE, PAGE_SIZE, N_MIX_HEADS, HEAD_DIM))
            sbv_p.append(v_p.reshape(B, S // PAGE_SIZE, PAGE_SIZE, N_MIX_HEADS, HEAD_DIM))
            sbk_s.append(k_s)
            sbv_s.append(v_s)
            qp, qs = zp[..., SB_PROJ:], zs[..., SB_PROJ:]
        cp = mem_attend(qp, mk, mv)
        cs = mem_attend(qs, cache_mem_k[i], cache_mem_v[i])
        xp = xp + jnp.concatenate([mix_p, cp], axis=-1) @ w_out[i]
        xs = xs + jnp.concatenate([mix_s, cs], axis=-1) @ w_out[i]
        xp = xp + ffn(xp, norm_ffn[i], w_ffn_up[i], w_ffn_down[i])
        xs = xs + ffn(xs, norm_ffn[i], w_ffn_up[i], w_ffn_down[i])
    y_prompt = rmsnorm(xp, norm_final)
    y_sample = rmsnorm(xs, norm_final)
    return (y_prompt, y_sample, jnp.stack(mem_k_p), jnp.stack(mem_v_p),
            jnp.stack(wkv_p), jnp.stack(shift_p), jnp.stack(wkv_s), jnp.stack(shift_s),
            jnp.stack(sbk_p), jnp.stack(sbv_p), jnp.stack(sbk_s), jnp.stack(sbv_s))
```

```python
import functools

import jax
import jax.numpy as jnp
from jax import lax
from jax.experimental import pallas as pl
from jax.experimental.pallas import tpu as pltpu

F32 = jnp.float32
BF16 = jnp.bfloat16

HEAD = 64
N_HEADS = 24
MIX = N_HEADS * HEAD
N_PAIRS = N_HEADS // 2
LANES = 128
MEM_HEADS = 4
MEM_HD = 128
MEM_W = MEM_HEADS * MEM_HD
DECAY_LORA = 64
ICLR_LORA = 64
GATE_LORA = 224
LORA_PAD = 512
RWKV_PROJ = 3 * MIX + DECAY_LORA + ICLR_LORA + GATE_LORA
RMS_EPS = 1e-6
GN_EPS = 64e-5
PAGE = 128

_NT = (((1,), (1,)), ((), ()))


def _cparams(sem, vmem_mb):
    return pltpu.CompilerParams(dimension_semantics=sem, vmem_limit_bytes=vmem_mb << 20)


def _split3(x):
    hi = x.astype(BF16)
    r1 = x - hi.astype(F32)
    mid = r1.astype(BF16)
    lo = (r1 - mid.astype(F32)).astype(BF16)
    return hi, mid, lo


def _dot_exact_rhs(x, ones_bf16):
    hi, mid, lo = _split3(x)
    d = lambda a: jnp.dot(a, ones_bf16, preferred_element_type=F32)
    return d(hi) + d(mid) + d(lo)


def _rms_rows(x, g):
    ms = jnp.mean(x * x, axis=-1, keepdims=True)
    return x * lax.rsqrt(ms + RMS_EPS) * g


def _rms_matmul_kernel(x_ref, g_ref, w_ref, b_ref, o_ref, xn_ref):
    @pl.when(pl.program_id(1) == 0)
    def _():
        xn_ref[...] = _rms_rows(x_ref[...], g_ref[...]).astype(BF16)

    o_ref[...] = jnp.dot(xn_ref[...], w_ref[...], preferred_element_type=F32) + b_ref[...]


def _rms_matmul(x, g, w, b, tm, tn):
    m, d = x.shape
    n = w.shape[1]
    return pl.pallas_call(
        _rms_matmul_kernel,
        out_shape=jax.ShapeDtypeStruct((m, n), F32),
        grid=(m // tm, n // tn),
        in_specs=[
            pl.BlockSpec((tm, d), lambda i, j: (i, 0)),
            pl.BlockSpec((1, d), lambda i, j: (0, 0)),
            pl.BlockSpec((d, tn), lambda i, j: (0, j)),
            pl.BlockSpec((1, tn), lambda i, j: (0, j)),
        ],
        out_specs=pl.BlockSpec((tm, tn), lambda i, j: (i, j)),
        scratch_shapes=[pltpu.VMEM((tm, d), BF16)],
        compiler_params=_cparams(("parallel", "arbitrary"), 48),
        name="rms_matmul",
    )(x, g.reshape(1, d), w, b.reshape(1, n))


def _ffn_kernel(x_ref, g_ref, wu_ref, wd_ref, o_ref, xn_ref):
    @pl.when(pl.program_id(1) == 0)
    def _():
        x = x_ref[...]
        xn_ref[...] = _rms_rows(x, g_ref[...]).astype(BF16)
        o_ref[...] = x

    h = jnp.maximum(jnp.dot(xn_ref[...], wu_ref[...], preferred_element_type=F32), 0.0)
    o_ref[...] += jnp.dot((h * h).astype(BF16), wd_ref[...], preferred_element_type=F32)


def _ffn(x, g, wu, wd, tm, tf):
    m, d = x.shape
    f = wu.shape[1]
    return pl.pallas_call(
        _ffn_kernel,
        out_shape=jax.ShapeDtypeStruct((m, d), F32),
        grid=(m // tm, f // tf),
        in_specs=[
            pl.BlockSpec((tm, d), lambda i, j: (i, 0)),
            pl.BlockSpec((1, d), lambda i, j: (0, 0)),
            pl.BlockSpec((d, tf), lambda i, j: (0, j)),
            pl.BlockSpec((tf, d), lambda i, j: (j, 0)),
        ],
        out_specs=pl.BlockSpec((tm, d), lambda i, j: (i, 0)),
        scratch_shapes=[pltpu.VMEM((tm, d), BF16)],
        compiler_params=_cparams(("parallel", "arbitrary"), 48),
        name="ffn",
    )(x, g.reshape(1, d), wu, wd)


def _outproj_kernel(x_ref, mix_ref, c_ref, w1_ref, w2_ref, o_ref):
    acc = jnp.dot(mix_ref[...].astype(BF16), w1_ref[...], preferred_element_type=F32)
    acc += jnp.dot(c_ref[...].astype(BF16), w2_ref[...], preferred_element_type=F32)
    o_ref[...] = x_ref[...] + acc


def _outproj(x, mix, c, w1, w2, tm):
    m, d = x.shape
    return pl.pallas_call(
        _outproj_kernel,
        out_shape=jax.ShapeDtypeStruct((m, d), F32),
        grid=(m // tm,),
        in_specs=[
            pl.BlockSpec((tm, d), lambda i: (i, 0)),
            pl.BlockSpec((tm, MIX), lambda i: (i, 0)),
            pl.BlockSpec((tm, MEM_W), lambda i: (i, 0)),
            pl.BlockSpec((MIX, d), lambda i: (0, 0)),
            pl.BlockSpec((MEM_W, d), lambda i: (0, 0)),
        ],
        out_specs=pl.BlockSpec((tm, d), lambda i: (i, 0)),
        compiler_params=_cparams(("parallel",), 48),
        name="outproj",
    )(x, mix, c, w1, w2)


def _mem_attend_kernel(q_ref, k_ref, v_ref, o_ref):
    q = q_ref[0]
    for h in range(MEM_HEADS):
        sl = slice(h * MEM_HD, (h + 1) * MEM_HD)
        s = lax.dot_general(q[:, sl].astype(BF16), k_ref[0, :, sl].astype(BF16), _NT,
                            preferred_element_type=F32) * (MEM_HD ** -0.5)
        e = jnp.exp(s - jnp.max(s, axis=-1, keepdims=True))
        p = e / jnp.sum(e, axis=-1, keepdims=True)
        o_ref[0, :, sl] = jnp.dot(p.astype(BF16), v_ref[0, :, sl].astype(BF16),
                                  preferred_element_type=F32)


def _mem_attend(q, mk, mv, tt):
    b, t, _ = q.shape
    nm = mk.shape[1]
    return pl.pallas_call(
        _mem_attend_kernel,
        out_shape=jax.ShapeDtypeStruct((b, t, MEM_W), F32),
        grid=(b, t // tt),
        in_specs=[
            pl.BlockSpec((1, tt, MEM_W), lambda i, j: (i, j, 0)),
            pl.BlockSpec((1, nm, MEM_W), lambda i, j: (i, 0, 0)),
            pl.BlockSpec((1, nm, MEM_W), lambda i, j: (i, 0, 0)),
        ],
        out_specs=pl.BlockSpec((1, tt, MEM_W), lambda i, j: (i, j, 0)),
        compiler_params=_cparams(("parallel", "parallel"), 32),
        name="mem_attend",
    )(q, mk, mv)


def _rwkv_prep_kernel(seq_len, tm,
                      pr_ref, pk_ref, pv_ref, pl_ref,
                      br_ref, bk_ref, bv_ref, bl_ref,
                      sr_ref, sk_ref, sv_ref, sl_ref,
                      mur_ref, muk_ref, muv_ref, mul_ref,
                      w0_ref, a0_ref, kk_ref, ka_ref, rk_ref,
                      w2_ref, a2_ref, g2_ref, ones_ref,
                      r_out, w_out, k_out, v_out, a_out, b_out, g_out, bonus_out):
    i = pl.program_id(0)
    row = lax.broadcasted_iota(jnp.int32, (tm, 1), 0)

    def shifted(cur_ref, before_ref, state_ref, mu_ref):
        cur = cur_ref[...]
        rolled = pltpu.roll(cur, 1, 0)
        if seq_len >= tm:
            first = jnp.where(i == 0, state_ref[0:1, :], before_ref[7:8, :])
            prev = jnp.where(row == 0, first, rolled)
        else:
            prev = jnp.where(row % seq_len == 0, state_ref[...], rolled)
        return cur + (prev - cur) * mu_ref[...]

    r = shifted(pr_ref, br_ref, sr_ref, mur_ref)
    k = shifted(pk_ref, bk_ref, sk_ref, muk_ref)
    v = shifted(pv_ref, bv_ref, sv_ref, muv_ref)
    lo = shifted(pl_ref, bl_ref, sl_ref, mul_ref)

    xw = jnp.tanh(lo[:, 0:128]).astype(BF16)
    xa = lo[:, 128:256].astype(BF16)
    xg = jax.nn.sigmoid(lo[:, 256:512]).astype(BF16)
    wl = w0_ref[...] + jnp.dot(xw, w2_ref[...], preferred_element_type=F32)
    w_log = -jax.nn.softplus(-wl) - 0.5
    decay = jnp.exp(-jnp.exp(w_log))
    a = jax.nn.sigmoid(a0_ref[...] + jnp.dot(xa, a2_ref[...], preferred_element_type=F32))
    g = jnp.dot(xg, g2_ref[...], preferred_element_type=F32)

    ones = ones_ref[...]
    kk = k * kk_ref[...]
    nrm = jnp.sqrt(_dot_exact_rhs(kk * kk, ones))
    kk = kk / jnp.maximum(nrm, 1e-12)
    k2 = k * (1.0 + (a - 1.0) * ka_ref[...])
    bonus = _dot_exact_rhs(r * k2 * rk_ref[...], ones) * v

    r_out[...] = r
    w_out[...] = decay
    k_out[...] = k2
    v_out[...] = v
    a_out[...] = -kk
    b_out[...] = kk * a
    g_out[...] = g
    bonus_out[...] = bonus


def _rwkv_prep(z, shift_rows, seq_len, tm, mu, w0, a0, k_k, k_a, r_k, w2, a2, g2, ones):
    m = z.shape[0]
    nc = MIX // LANES
    lb = 3 * MIX // LORA_PAD
    tb = tm // 8
    srows = shift_rows.shape[0] if seq_len >= tm else tm

    def cur(off):
        return pl.BlockSpec((tm, LANES), lambda i, c: (i, off + c))

    def before(off):
        return pl.BlockSpec((8, LANES), lambda i, c: (jnp.maximum(i * tb - 1, 0), off + c))

    def state(off):
        if seq_len >= tm:
            return pl.BlockSpec((srows, LANES), lambda i, c: (0, off + c))
        return pl.BlockSpec((tm, LANES), lambda i, c: (i, off + c))

    def vec(off):
        return pl.BlockSpec((1, LANES), lambda i, c: (0, off + c))

    cur_l = pl.BlockSpec((tm, LORA_PAD), lambda i, c: (i, lb))
    before_l = pl.BlockSpec((8, LORA_PAD), lambda i, c: (jnp.maximum(i * tb - 1, 0), lb))
    if seq_len >= tm:
        state_l = pl.BlockSpec((srows, LORA_PAD), lambda i, c: (0, lb))
    else:
        state_l = pl.BlockSpec((tm, LORA_PAD), lambda i, c: (i, lb))
    vec_l = pl.BlockSpec((1, LORA_PAD), lambda i, c: (0, lb))

    out_spec = pl.BlockSpec((tm, LANES), lambda i, c: (i, c))
    out_sds = jax.ShapeDtypeStruct((m, MIX), F32)
    return pl.pallas_call(
        functools.partial(_rwkv_prep_kernel, seq_len, tm),
        out_shape=(out_sds,) * 8,
        grid=(m // tm, nc),
        in_specs=[
            cur(0), cur(nc), cur(2 * nc), cur_l,
            before(0), before(nc), before(2 * nc), before_l,
            state(0), state(nc), state(2 * nc), state_l,
            vec(0), vec(nc), vec(2 * nc), vec_l,
            vec(0), vec(0), vec(0), vec(0), vec(0),
            pl.BlockSpec((LANES, LANES), lambda i, c: (0, c)),
            pl.BlockSpec((LANES, LANES), lambda i, c: (0, c)),
            pl.BlockSpec((2 * LANES, LANES), lambda i, c: (0, c)),
            pl.BlockSpec((LANES, LANES), lambda i, c: (0, 0)),
        ],
        out_specs=(out_spec,) * 8,
        compiler_params=_cparams(("parallel", "parallel"), 48),
        name="rwkv_prep",
    )(z, z, z, z, z, z, z, z, shift_rows, shift_rows, shift_rows, shift_rows,
      mu, mu, mu, mu, w0, a0, k_k, k_a, r_k, w2, a2, g2, ones)


def _rwkv_scan_kernel(tb, r_ref, w_ref, k_ref, v_ref, a_ref, b_ref, s0_ref, y_ref, sf_ref, s_ref):
    tblk = pl.program_id(1)

    @pl.when(tblk == 0)
    def _():
        s_ref[...] = s0_ref[0]

    lane = lax.broadcasted_iota(jnp.int32, (HEAD, LANES), 1)
    sub = lax.broadcasted_iota(jnp.int32, (HEAD, LANES), 0)
    lo = lane < HEAD
    diag = (lane % HEAD) == sub

    def head_sums(x):
        sa = jnp.sum(jnp.where(lo, x, 0.0), axis=1, keepdims=True)
        sb = jnp.sum(jnp.where(lo, 0.0, x), axis=1, keepdims=True)
        return jnp.where(lo, sa, sb)

    def step(t, carry):
        row = lambda ref: ref[0, pl.ds(t, 1), :]
        r_t, w_t, k_t, v_t, a_t, b_t = (row(x) for x in (r_ref, w_ref, k_ref, v_ref, a_ref, b_ref))
        ys = []
        for p in range(N_PAIRS):
            sl = slice(p * LANES, (p + 1) * LANES)
            s = s_ref[p]
            sa = head_sums(s * a_t[:, sl])
            vc = head_sums(jnp.where(diag, v_t[:, sl], 0.0))
            s = s * w_t[:, sl] + sa * b_t[:, sl] + vc * k_t[:, sl]
            s_ref[p] = s
            yc = head_sums(s * r_t[:, sl])
            ys.append(jnp.sum(jnp.where(diag, yc, 0.0), axis=0, keepdims=True))
        y_ref[0, pl.ds(t, 1), :] = jnp.concatenate(ys, axis=1)
        return carry

    lax.fori_loop(0, tb, step, 0)

    @pl.when(tblk == pl.num_programs(1) - 1)
    def _():
        sf_ref[0] = s_ref[...]


def _rwkv_scan(r, w, k, v, a, b, s0, tb):
    bsz, t, _ = r.shape
    s0p = s0.reshape(bsz, N_PAIRS, 2, HEAD, HEAD).transpose(0, 1, 3, 2, 4).reshape(bsz, N_PAIRS, HEAD, LANES)
    seq = pl.BlockSpec((1, tb, MIX), lambda i, j: (i, j, 0))
    st = pl.BlockSpec((1, N_PAIRS, HEAD, LANES), lambda i, j: (i, 0, 0, 0))
    y, sf = pl.pallas_call(
        functools.partial(_rwkv_scan_kernel, tb),
        out_shape=(jax.ShapeDtypeStruct((bsz, t, MIX), F32),
                   jax.ShapeDtypeStruct((bsz, N_PAIRS, HEAD, LANES), F32)),
        grid=(bsz, t // tb),
        in_specs=[seq] * 6 + [st],
        out_specs=(seq, st),
        scratch_shapes=[pltpu.VMEM((N_PAIRS, HEAD, LANES), F32)],
        compiler_params=_cparams(("parallel", "arbitrary"), 48),
        name="rwkv_scan",
    )(r, w, k, v, a, b, s0p)
    sf = sf.reshape(bsz, N_PAIRS, HEAD, 2, HEAD).transpose(0, 1, 3, 2, 4).reshape(bsz, N_HEADS, HEAD, HEAD)
    return y, sf


def _rwkv_post_kernel(y_ref, bonus_ref, g_ref, lw_ref, lb_ref, ones_ref, o_ref):
    y = y_ref[...]
    ones = ones_ref[...]
    m = _dot_exact_rhs(y, ones) * (1.0 / HEAD)
    d = y - m
    var = _dot_exact_rhs(d * d, ones) * (1.0 / HEAD)
    yn = d * lax.rsqrt(var + GN_EPS) * lw_ref[...] + lb_ref[...]
    o_ref[...] = (yn + bonus_ref[...]) * g_ref[...]


def _rwkv_post(y, bonus, g, lnx_w, lnx_b, ones, tm):
    m = y.shape[0]
    blk = pl.BlockSpec((tm, LANES), lambda i, c: (i, c))
    vec = pl.BlockSpec((1, LANES), lambda i, c: (0, c))
    return pl.pallas_call(
        _rwkv_post_kernel,
        out_shape=jax.ShapeDtypeStruct((m, MIX), F32),
        grid=(m // tm, MIX // LANES),
        in_specs=[blk, blk, blk, vec, vec, pl.BlockSpec((LANES, LANES), lambda i, c: (0, 0))],
        out_specs=blk,
        compiler_params=_cparams(("parallel", "parallel"), 32),
        name="rwkv_post",
    )(y, bonus, g, lnx_w, lnx_b, ones)


def _sb_tile(z, carry, upper, allowed):
    sp = jnp.log1p(jnp.exp(-jnp.abs(z)))
    ls = jnp.minimum(z, 0.0) - sp
    l1m = -jnp.maximum(z, 0.0) - sp
    if allowed is not None:
        l1m = jnp.where(allowed, l1m, 0.0)
    hi = l1m.astype(BF16)
    lo = (l1m - hi.astype(F32)).astype(BF16)
    suffix = (jnp.dot(hi, upper, preferred_element_type=F32)
              + jnp.dot(lo, upper, preferred_element_type=F32))
    a = jnp.exp(ls + suffix + carry)
    if allowed is not None:
        a = jnp.where(allowed, a, 0.0)
    return a.astype(BF16), carry + jnp.sum(l1m, axis=1, keepdims=True)


def _sb_prompt_kernel(tq, q_ref, k_ref, v_ref, up_ref, o_ref, kb_ref, va_ref, vb_ref, acc_ref):
    qi = pl.program_id(1)
    lane = lax.broadcasted_iota(jnp.int32, (1, LANES), 1)
    lo = lane < HEAD

    @pl.when(qi == 0)
    def _():
        kb_ref[...] = k_ref[...].astype(BF16)
        v = v_ref[...]
        va_ref[...] = jnp.where(lo, v, 0.0).astype(BF16)
        vb_ref[...] = jnp.where(lo, 0.0, v).astype(BF16)

    q = q_ref[...] * (HEAD ** -0.5)
    qa = jnp.where(lo, q, 0.0).astype(BF16)
    qb = jnp.where(lo, 0.0, q).astype(BF16)
    upper = up_ref[...]
    acc_ref[...] = jnp.zeros_like(acc_ref)

    def tile(j, carry, masked):
        ca, cb = carry
        ks = pl.multiple_of(j * tq, tq)
        kblk = kb_ref[pl.ds(ks, tq), :]
        allowed = None
        if masked:
            allowed = (lax.broadcasted_iota(jnp.int32, (tq, tq), 1)
                       < lax.broadcasted_iota(jnp.int32, (tq, tq), 0))
        za = lax.dot_general(qa, kblk, _NT, preferred_element_type=F32)
        zb = lax.dot_general(qb, kblk, _NT, preferred_element_type=F32)
        pa, ca = _sb_tile(za, ca, upper, allowed)
        pb, cb = _sb_tile(zb, cb, upper, allowed)
        acc_ref[...] += (jnp.dot(pa, va_ref[pl.ds(ks, tq), :], preferred_element_type=F32)
                         + jnp.dot(pb, vb_ref[pl.ds(ks, tq), :], preferred_element_type=F32))
        return ca, cb

    zero = jnp.zeros((tq, 1), F32)
    carry = tile(qi, (zero, zero), True)
    lax.fori_loop(0, qi, lambda n, c: tile(qi - 1 - n, c, False), carry)
    o_ref[...] = acc_ref[...]


def _sb_prompt(z, upper, tq):
    s = z.shape[0]
    nc = MIX // LANES
    return pl.pallas_call(
        functools.partial(_sb_prompt_kernel, tq),
        out_shape=jax.ShapeDtypeStruct((s, MIX), F32),
        grid=(nc, s // tq),
        in_specs=[
            pl.BlockSpec((tq, LANES), lambda c, i: (i, c)),
            pl.BlockSpec((s, LANES), lambda c, i: (0, nc + c)),
            pl.BlockSpec((s, LANES), lambda c, i: (0, 2 * nc + c)),
            pl.BlockSpec((tq, tq), lambda c, i: (0, 0)),
        ],
        out_specs=pl.BlockSpec((tq, LANES), lambda c, i: (i, c)),
        scratch_shapes=[pltpu.VMEM((s, LANES), BF16)] * 3 + [pltpu.VMEM((tq, LANES), F32)],
        compiler_params=_cparams(("parallel", "arbitrary"), 48),
        name="sb_prompt",
    )(z, z, z, upper)


def _sb_sample_kernel(n_pages, t_new, pt_ref, q_ref, kn_ref, vn_ref, kc_ref, vc_ref, up_ref, o_ref,
                      acc_ref, carry_ref):
    p = pl.program_id(1)
    rows = q_ref.shape[1]
    q = q_ref[0]
    upper = up_ref[...]

    def tile(k, v, allowed):
        z = lax.dot_general(q, k.astype(BF16), _NT, preferred_element_type=F32)
        a, carry = _sb_tile(z, carry_ref[...], upper, allowed)
        carry_ref[...] = carry
        acc_ref[...] += jnp.dot(a, v.astype(BF16), preferred_element_type=F32)

    @pl.when(p == 0)
    def _():
        acc_ref[...] = jnp.zeros_like(acc_ref)
        carry_ref[...] = jnp.zeros_like(carry_ref)
        key = lax.broadcasted_iota(jnp.int32, (rows, PAGE), 1)
        qry = lax.broadcasted_iota(jnp.int32, (rows, PAGE), 0) // N_HEADS
        tile(kn_ref[0], vn_ref[0], key < qry)

    @pl.when(p > 0)
    def _():
        tile(kc_ref[0], vc_ref[0], None)

    @pl.when(p == n_pages)
    def _():
        head_of_lane = lax.broadcasted_iota(jnp.int32, (N_HEADS, MIX), 1) // HEAD
        own = head_of_lane == lax.broadcasted_iota(jnp.int32, (N_HEADS, MIX), 0)
        for t in range(t_new):
            blk = acc_ref[t * N_HEADS:(t + 1) * N_HEADS, :]
            o_ref[0, t:t + 1, :] = jnp.sum(jnp.where(own, blk, 0.0), axis=0, keepdims=True)


def _sb_sample(q_rows, k_new, v_new, cache_k, cache_v, page_table, upper):
    bsz, rows, _ = q_rows.shape
    t_new = rows // N_HEADS
    n_pages = page_table.shape[1]

    def page(b, p, pt):
        return (pt[b, jnp.clip(n_pages - p, 0, n_pages - 1)], 0, 0)

    grid_spec = pltpu.PrefetchScalarGridSpec(
        num_scalar_prefetch=1,
        grid=(bsz, n_pages + 1),
        in_specs=[
            pl.BlockSpec((1, rows, MIX), lambda b, p, pt: (b, 0, 0)),
            pl.BlockSpec((1, PAGE, MIX), lambda b, p, pt: (b, 0, 0)),
            pl.BlockSpec((1, PAGE, MIX), lambda b, p, pt: (b, 0, 0)),
            pl.BlockSpec((1, PAGE, MIX), page),
            pl.BlockSpec((1, PAGE, MIX), page),
            pl.BlockSpec((PAGE, PAGE), lambda b, p, pt: (0, 0)),
        ],
        out_specs=pl.BlockSpec((1, t_new, MIX), lambda b, p, pt: (b, 0, 0)),
        scratch_shapes=[pltpu.VMEM((rows, MIX), F32), pltpu.VMEM((rows, 1), F32)],
    )
    return pl.pallas_call(
        functools.partial(_sb_sample_kernel, n_pages, t_new),
        out_shape=jax.ShapeDtypeStruct((bsz, t_new, MIX), F32),
        grid_spec=grid_spec,
        compiler_params=_cparams(("parallel", "arbitrary"), 32),
        name="sb_sample",
    )(page_table, q_rows, k_new, v_new, cache_k, cache_v, upper)


def _rmsnorm_kernel(x_ref, g_ref, o_ref):
    o_ref[...] = _rms_rows(x_ref[...], g_ref[...])


def _rmsnorm(x, g, tm):
    m, d = x.shape
    return pl.pallas_call(
        _rmsnorm_kernel,
        out_shape=jax.ShapeDtypeStruct((m, d), F32),
        grid=(m // tm,),
        in_specs=[pl.BlockSpec((tm, d), lambda i: (i, 0)), pl.BlockSpec((1, d), lambda i: (0, 0))],
        out_specs=pl.BlockSpec((tm, d), lambda i: (i, 0)),
        compiler_params=_cparams(("parallel",), 32),
        name="rmsnorm",
    )(x, g.reshape(1, d))


def _pad_cols(x, width):
    return jnp.pad(x, ((0, 0), (0, width - x.shape[1])))


def _rwkv_cols(x):
    o = 3 * MIX
    parts = [x[:, :o],
             _pad_cols(x[:, o:o + DECAY_LORA], LANES),
             _pad_cols(x[:, o + DECAY_LORA:o + DECAY_LORA + ICLR_LORA], LANES),
             _pad_cols(x[:, o + DECAY_LORA + ICLR_LORA:RWKV_PROJ], 2 * LANES)]
    if x.shape[1] > RWKV_PROJ:
        parts.append(x[:, RWKV_PROJ:])
    return jnp.concatenate(parts, axis=1)


def _rwkv_uncols(x):
    o = 3 * MIX
    return jnp.concatenate([x[:, :o], x[:, o:o + DECAY_LORA], x[:, o + LANES:o + LANES + ICLR_LORA],
                            x[:, o + 2 * LANES:o + 2 * LANES + GATE_LORA]], axis=1)


def _pad_rows(x, rows):
    return jnp.pad(x, ((0, rows - x.shape[0]), (0, 0)))


def _tile(m, pref):
    return pref if m % pref == 0 else m


def kernel(x_prompt, x_sample, mem_prompt, state_rwkv_wkv, state_rwkv_shift, cache_sb_k, cache_sb_v, page_table, cache_mem_k, cache_mem_v, norm_mix, norm_ffn, norm_final, mem_norm, w_mem_kv, w_out, w_ffn_up, w_ffn_down, rwkv_w_in, rwkv_mu, rwkv_w0, rwkv_w2, rwkv_a0, rwkv_a2, rwkv_g2, rwkv_k_k, rwkv_k_a, rwkv_r_k, rwkv_lnx_w, rwkv_lnx_b, sb_w_in, sb_b_qk):
    bp, s, d = x_prompt.shape
    db, t_new, _ = x_sample.shape
    assert bp == 1, "the prompt path handles a single sequence"
    depth = norm_mix.shape[0]
    ms = db * t_new
    n_mem = mem_prompt.shape[1]
    xp = x_prompt.reshape(s, d)
    xs = x_sample.reshape(ms, d)
    tmp = _tile(s, 512)
    tms = ms

    idx = lax.broadcasted_iota(jnp.int32, (LANES, LANES), 0)
    jdx = lax.broadcasted_iota(jnp.int32, (LANES, LANES), 1)
    head_ones = ((idx // HEAD) == (jdx // HEAD)).astype(BF16)
    tq = _tile(s, 256)
    iq = lax.broadcasted_iota(jnp.int32, (tq, tq), 0)
    jq = lax.broadcasted_iota(jnp.int32, (tq, tq), 1)
    upper_q = (iq > jq).astype(BF16)
    upper_p = (idx > jdx).astype(BF16)

    mem_k, mem_v = [], []
    wkv_p, shift_p, wkv_s, shift_s = [], [], [], []
    sbk_p, sbv_p, sbk_s, sbv_s = [], [], [], []
    for i in range(depth):
        j = i // 2
        kv = _rms_matmul(mem_prompt.reshape(n_mem, d), mem_norm[i], w_mem_kv[i].astype(BF16),
                         jnp.zeros((2 * MEM_W,), F32), n_mem, 2 * MEM_W)
        mk, mv = kv[:, :MEM_W], kv[:, MEM_W:]
        mem_k.append(mk.reshape(1, n_mem, MEM_HEADS, MEM_HD))
        mem_v.append(mv.reshape(1, n_mem, MEM_HEADS, MEM_HD))
        cmk = cache_mem_k[i].reshape(db, n_mem, MEM_W)
        cmv = cache_mem_v[i].reshape(db, n_mem, MEM_W)
        if i % 2 == 0:
            w_in = _rwkv_cols(rwkv_w_in[j]).astype(BF16)
            n_in = w_in.shape[1]
            zero_b = jnp.zeros((n_in,), F32)
            zp = _rms_matmul(xp, norm_mix[i], w_in, zero_b, tmp, 512)
            zs = _rms_matmul(xs, norm_mix[i], w_in, zero_b, tms, 512)
            shift_p.append(_rwkv_uncols(zp[s - 1:s]))
            shift_s.append(_rwkv_uncols(zs.reshape(db, t_new, n_in)[:, t_new - 1]))
            wide = 3 * MIX + LORA_PAD
            mu = _rwkv_cols(rwkv_mu[j][None])
            vecs = [x.reshape(1, MIX) for x in (rwkv_w0[j], rwkv_a0[j], rwkv_k_k[j], rwkv_k_a[j], rwkv_r_k[j])]
            lora = (_pad_rows(rwkv_w2[j], LANES).astype(BF16), _pad_rows(rwkv_a2[j], LANES).astype(BF16),
                    _pad_rows(rwkv_g2[j], 2 * LANES).astype(BF16))
            lnx = (rwkv_lnx_w[j].reshape(1, MIX), rwkv_lnx_b[j].reshape(1, MIX))

            def rwkv(z, shift_rows, seq_len, tm, s0, nb, tb, tpost):
                r, w, k, v, a, b, g, bonus = _rwkv_prep(z, shift_rows, seq_len, tm, mu, *vecs, *lora, head_ones)
                sh = lambda x: x.reshape(nb, seq_len, MIX)
                y, s_fin = _rwkv_scan(sh(r), sh(w), sh(k), sh(v), sh(a), sh(b), s0, tb)
                return _rwkv_post(y.reshape(nb * seq_len, MIX), bonus, g, *lnx, head_ones, tpost), s_fin

            zero_shift = jnp.zeros((8, wide), F32)
            zero_state = jnp.zeros((1, N_HEADS, HEAD, HEAD), F32)
            mix_p, st_p = rwkv(zp, zero_shift, s, tmp, zero_state, 1, _tile(s, 256), tmp)
            shift_rows = jnp.repeat(_rwkv_cols(state_rwkv_shift[j]), t_new, axis=0)
            mix_s, st_s = rwkv(zs, shift_rows, t_new, tms, state_rwkv_wkv[j], db, t_new, tms)
            wkv_p.append(st_p)
            wkv_s.append(st_s)
            q_off = wide
        else:
            w_in = sb_w_in[j].astype(BF16)
            bias = jnp.concatenate([sb_b_qk[j], jnp.zeros((w_in.shape[1] - 2 * MIX,), F32)])
            zp = _rms_matmul(xp, norm_mix[i], w_in, bias, tmp, 512)
            zs = _rms_matmul(xs, norm_mix[i], w_in, bias, tms, 512)
            mix_p = _sb_prompt(zp, upper_q, tq)
            n_pool = cache_sb_k.shape[1]
            q_s = zs[:, :MIX].reshape(db, t_new, N_HEADS, 1, HEAD) * (HEAD ** -0.5)
            eye = jnp.eye(N_HEADS, dtype=F32).reshape(1, 1, N_HEADS, N_HEADS, 1)
            q_rows = (q_s * eye).reshape(db, t_new * N_HEADS, MIX).astype(BF16)
            k_s = zs[:, MIX:2 * MIX].reshape(db, t_new, MIX)
            v_s = zs[:, 2 * MIX:3 * MIX].reshape(db, t_new, MIX)
            pad_new = lambda x: jnp.pad(x, ((0, 0), (0, PAGE - t_new), (0, 0)))
            mix_s = _sb_sample(q_rows, pad_new(k_s), pad_new(v_s),
                               cache_sb_k[j].reshape(n_pool, PAGE, MIX), cache_sb_v[j].reshape(n_pool, PAGE, MIX),
                               page_table, upper_p).reshape(ms, MIX)
            sbk_p.append(zp[:, MIX:2 * MIX].reshape(1, s // PAGE, PAGE, N_HEADS, HEAD))
            sbv_p.append(zp[:, 2 * MIX:3 * MIX].reshape(1, s // PAGE, PAGE, N_HEADS, HEAD))
            sbk_s.append(k_s.reshape(db, t_new, N_HEADS, HEAD))
            sbv_s.append(v_s.reshape(db, t_new, N_HEADS, HEAD))
            q_off = 3 * MIX
        cp = _mem_attend(zp[:, q_off:q_off + MEM_W][None], mk[None], mv[None], tmp)[0]
        cs = _mem_attend(zs[:, q_off:q_off + MEM_W].reshape(db, t_new, MEM_W), cmk, cmv, t_new).reshape(ms, MEM_W)
        wo = w_out[i].astype(BF16)
        xp = _outproj(xp, mix_p, cp, wo[:MIX], wo[MIX:], tmp)
        xs = _outproj(xs, mix_s, cs, wo[:MIX], wo[MIX:], tms)
        wu, wd = w_ffn_up[i].astype(BF16), w_ffn_down[i].astype(BF16)
        xp = _ffn(xp, norm_ffn[i], wu, wd, tmp, 512)
        xs = _ffn(xs, norm_ffn[i], wu, wd, tms, 512)
    y_prompt = _rmsnorm(xp, norm_final, tmp).reshape(1, s, d)
    y_sample = _rmsnorm(xs, norm_final, tms).reshape(db, t_new, d)
    return (y_prompt, y_sample, jnp.stack(mem_k), jnp.stack(mem_v),
            jnp.stack(wkv_p), jnp.stack(shift_p), jnp.stack(wkv_s), jnp.stack(shift_s),
            jnp.stack(sbk_p), jnp.stack(sbv_p), jnp.stack(sbk_s), jnp.stack(sbv_s))
```

```python
import functools

import jax
import jax.numpy as jnp
from jax import lax
from jax.experimental import pallas as pl
from jax.experimental.pallas import tpu as pltpu

F32 = jnp.float32
BF16 = jnp.bfloat16

HEAD = 64
N_HEADS = 24
MIX = N_HEADS * HEAD
N_PAIRS = N_HEADS // 2
LANES = 128
MEM_HEADS = 4
MEM_HD = 128
MEM_W = MEM_HEADS * MEM_HD
DECAY_LORA = 64
ICLR_LORA = 64
GATE_LORA = 224
LORA_PAD = 512
RWKV_PROJ = 3 * MIX + DECAY_LORA + ICLR_LORA + GATE_LORA
RMS_EPS = 1e-6
GN_EPS = 64e-5
PAGE = 128

_NT = (((1,), (1,)), ((), ()))


def _cparams(sem, vmem_mb):
    return pltpu.CompilerParams(dimension_semantics=sem, vmem_limit_bytes=vmem_mb << 20)


def _split3(x):
    hi = x.astype(BF16)
    r1 = x - hi.astype(F32)
    mid = r1.astype(BF16)
    lo = (r1 - mid.astype(F32)).astype(BF16)
    return hi, mid, lo


def _dot_exact_rhs(x, ones_bf16):
    hi, mid, lo = _split3(x)
    d = lambda a: jnp.dot(a, ones_bf16, preferred_element_type=F32)
    return d(hi) + d(mid) + d(lo)


def _rms_rows(x, g):
    ms = jnp.mean(x * x, axis=-1, keepdims=True)
    return x * lax.rsqrt(ms + RMS_EPS) * g


def _rms_matmul_kernel(x_ref, g_ref, w_ref, b_ref, o_ref, xn_ref):
    @pl.when(pl.program_id(1) == 0)
    def _():
        xn_ref[...] = _rms_rows(x_ref[...], g_ref[...]).astype(BF16)

    o_ref[...] = jnp.dot(xn_ref[...], w_ref[...], preferred_element_type=F32) + b_ref[...]


def _rms_matmul(x, g, w, b, tm, tn):
    m, d = x.shape
    n = w.shape[1]
    return pl.pallas_call(
        _rms_matmul_kernel,
        out_shape=jax.ShapeDtypeStruct((m, n), F32),
        grid=(m // tm, n // tn),
        in_specs=[
            pl.BlockSpec((tm, d), lambda i, j: (i, 0)),
            pl.BlockSpec((1, d), lambda i, j: (0, 0)),
            pl.BlockSpec((d, tn), lambda i, j: (0, j)),
            pl.BlockSpec((1, tn), lambda i, j: (0, j)),
        ],
        out_specs=pl.BlockSpec((tm, tn), lambda i, j: (i, j)),
        scratch_shapes=[pltpu.VMEM((tm, d), BF16)],
        compiler_params=_cparams(("parallel", "arbitrary"), 48),
        name="rms_matmul",
    )(x, g.reshape(1, d), w, b.reshape(1, n))


def _ffn_kernel(x_ref, g_ref, wu_ref, wd_ref, o_ref, xn_ref):
    @pl.when(pl.program_id(1) == 0)
    def _():
        x = x_ref[...]
        xn_ref[...] = _rms_rows(x, g_ref[...]).astype(BF16)
        o_ref[...] = x

    h = jnp.maximum(jnp.dot(xn_ref[...], wu_ref[...], preferred_element_type=F32), 0.0)
    o_ref[...] += jnp.dot((h * h).astype(BF16), wd_ref[...], preferred_element_type=F32)


def _ffn(x, g, wu, wd, tm, tf):
    m, d = x.shape
    f = wu.shape[1]
    return pl.pallas_call(
        _ffn_kernel,
        out_shape=jax.ShapeDtypeStruct((m, d), F32),
        grid=(m // tm, f // tf),
        in_specs=[
            pl.BlockSpec((tm, d), lambda i, j: (i, 0)),
            pl.BlockSpec((1, d), lambda i, j: (0, 0)),
            pl.BlockSpec((d, tf), lambda i, j: (0, j)),
            pl.BlockSpec((tf, d), lambda i, j: (j, 0)),
        ],
        out_specs=pl.BlockSpec((tm, d), lambda i, j: (i, 0)),
        scratch_shapes=[pltpu.VMEM((tm, d), BF16)],
        compiler_params=_cparams(("parallel", "arbitrary"), 48),
        name="ffn",
    )(x, g.reshape(1, d), wu, wd)


def _outproj_kernel(x_ref, mix_ref, c_ref, w1_ref, w2_ref, o_ref):
    acc = jnp.dot(mix_ref[...].astype(BF16), w1_ref[...], preferred_element_type=F32)
    acc += jnp.dot(c_ref[...].astype(BF16), w2_ref[...], preferred_element_type=F32)
    o_ref[...] = x_ref[...] + acc


def _outproj(x, mix, c, w1, w2, tm):
    m, d = x.shape
    return pl.pallas_call(
        _outproj_kernel,
        out_shape=jax.ShapeDtypeStruct((m, d), F32),
        grid=(m // tm,),
        in_specs=[
            pl.BlockSpec((tm, d), lambda i: (i, 0)),
            pl.BlockSpec((tm, MIX), lambda i: (i, 0)),
            pl.BlockSpec((tm, MEM_W), lambda i: (i, 0)),
            pl.BlockSpec((MIX, d), lambda i: (0, 0)),
            pl.BlockSpec((MEM_W, d), lambda i: (0, 0)),
        ],
        out_specs=pl.BlockSpec((tm, d), lambda i: (i, 0)),
        compiler_params=_cparams(("parallel",), 48),
        name="outproj",
    )(x, mix, c, w1, w2)


def _mem_attend_kernel(q_ref, k_ref, v_ref, o_ref):
    q = q_ref[0]
    for h in range(MEM_HEADS):
        sl = slice(h * MEM_HD, (h + 1) * MEM_HD)
        s = lax.dot_general(q[:, sl].astype(BF16), k_ref[0, :, sl].astype(BF16), _NT,
                            preferred_element_type=F32) * (MEM_HD ** -0.5)
        e = jnp.exp(s - jnp.max(s, axis=-1, keepdims=True))
        p = e / jnp.sum(e, axis=-1, keepdims=True)
        o_ref[0, :, sl] = jnp.dot(p.astype(BF16), v_ref[0, :, sl].astype(BF16),
                                  preferred_element_type=F32)


def _mem_attend(q, mk, mv, tt):
    b, t, _ = q.shape
    nm = mk.shape[1]
    return pl.pallas_call(
        _mem_attend_kernel,
        out_shape=jax.ShapeDtypeStruct((b, t, MEM_W), F32),
        grid=(b, t // tt),
        in_specs=[
            pl.BlockSpec((1, tt, MEM_W), lambda i, j: (i, j, 0)),
            pl.BlockSpec((1, nm, MEM_W), lambda i, j: (i, 0, 0)),
            pl.BlockSpec((1, nm, MEM_W), lambda i, j: (i, 0, 0)),
        ],
        out_specs=pl.BlockSpec((1, tt, MEM_W), lambda i, j: (i, j, 0)),
        compiler_params=_cparams(("parallel", "parallel"), 32),
        name="mem_attend",
    )(q, mk, mv)


def _rwkv_prep_kernel(seq_len, tm,
                      pr_ref, pk_ref, pv_ref, pl_ref,
                      br_ref, bk_ref, bv_ref, bl_ref,
                      sr_ref, sk_ref, sv_ref, sl_ref,
                      mur_ref, muk_ref, muv_ref, mul_ref,
                      w0_ref, a0_ref, kk_ref, ka_ref, rk_ref,
                      w2_ref, a2_ref, g2_ref, ones_ref,
                      r_out, w_out, k_out, v_out, a_out, b_out, g_out, bonus_out):
    i = pl.program_id(0)
    row = lax.broadcasted_iota(jnp.int32, (tm, 1), 0)

    def shifted(cur_ref, before_ref, state_ref, mu_ref):
        cur = cur_ref[...]
        rolled = pltpu.roll(cur, 1, 0)
        if seq_len >= tm:
            first = jnp.where(i == 0, state_ref[0:1, :], before_ref[7:8, :])
            prev = jnp.where(row == 0, first, rolled)
        else:
            prev = jnp.where(row % seq_len == 0, state_ref[...], rolled)
        return cur + (prev - cur) * mu_ref[...]

    r = shifted(pr_ref, br_ref, sr_ref, mur_ref)
    k = shifted(pk_ref, bk_ref, sk_ref, muk_ref)
    v = shifted(pv_ref, bv_ref, sv_ref, muv_ref)
    lo = shifted(pl_ref, bl_ref, sl_ref, mul_ref)

    xw = jnp.tanh(lo[:, 0:128]).astype(BF16)
    xa = lo[:, 128:256].astype(BF16)
    xg = jax.nn.sigmoid(lo[:, 256:512]).astype(BF16)
    wl = w0_ref[...] + jnp.dot(xw, w2_ref[...], preferred_element_type=F32)
    w_log = -jax.nn.softplus(-wl) - 0.5
    log_decay = -jnp.exp(w_log)
    a = jax.nn.sigmoid(a0_ref[...] + jnp.dot(xa, a2_ref[...], preferred_element_type=F32))
    g = jnp.dot(xg, g2_ref[...], preferred_element_type=F32)

    ones = ones_ref[...]
    kk = k * kk_ref[...]
    nrm = jnp.sqrt(_dot_exact_rhs(kk * kk, ones))
    kk = kk / jnp.maximum(nrm, 1e-12)
    k2 = k * (1.0 + (a - 1.0) * ka_ref[...])
    bonus = _dot_exact_rhs(r * k2 * rk_ref[...], ones) * v

    r_out[...] = r
    w_out[...] = log_decay
    k_out[...] = k2
    v_out[...] = v
    a_out[...] = -kk
    b_out[...] = kk * a
    g_out[...] = g
    bonus_out[...] = bonus


def _rwkv_prep(z, shift_rows, seq_len, tm, mu, w0, a0, k_k, k_a, r_k, w2, a2, g2, ones):
    m = z.shape[0]
    nc = MIX // LANES
    lb = 3 * MIX // LORA_PAD
    tb = tm // 8
    srows = shift_rows.shape[0] if seq_len >= tm else tm

    def cur(off):
        return pl.BlockSpec((tm, LANES), lambda i, c: (i, off + c))

    def before(off):
        return pl.BlockSpec((8, LANES), lambda i, c: (jnp.maximum(i * tb - 1, 0), off + c))

    def state(off):
        if seq_len >= tm:
            return pl.BlockSpec((srows, LANES), lambda i, c: (0, off + c))
        return pl.BlockSpec((tm, LANES), lambda i, c: (i, off + c))

    def vec(off):
        return pl.BlockSpec((1, LANES), lambda i, c: (0, off + c))

    cur_l = pl.BlockSpec((tm, LORA_PAD), lambda i, c: (i, lb))
    before_l = pl.BlockSpec((8, LORA_PAD), lambda i, c: (jnp.maximum(i * tb - 1, 0), lb))
    if seq_len >= tm:
        state_l = pl.BlockSpec((srows, LORA_PAD), lambda i, c: (0, lb))
    else:
        state_l = pl.BlockSpec((tm, LORA_PAD), lambda i, c: (i, lb))
    vec_l = pl.BlockSpec((1, LORA_PAD), lambda i, c: (0, lb))

    out_spec = pl.BlockSpec((tm, LANES), lambda i, c: (i, c))
    out_sds = jax.ShapeDtypeStruct((m, MIX), F32)
    return pl.pallas_call(
        functools.partial(_rwkv_prep_kernel, seq_len, tm),
        out_shape=(out_sds,) * 8,
        grid=(m // tm, nc),
        in_specs=[
            cur(0), cur(nc), cur(2 * nc), cur_l,
            before(0), before(nc), before(2 * nc), before_l,
            state(0), state(nc), state(2 * nc), state_l,
            vec(0), vec(nc), vec(2 * nc), vec_l,
            vec(0), vec(0), vec(0), vec(0), vec(0),
            pl.BlockSpec((LANES, LANES), lambda i, c: (0, c)),
            pl.BlockSpec((LANES, LANES), lambda i, c: (0, c)),
            pl.BlockSpec((2 * LANES, LANES), lambda i, c: (0, c)),
            pl.BlockSpec((LANES, LANES), lambda i, c: (0, 0)),
        ],
        out_specs=(out_spec,) * 8,
        compiler_params=_cparams(("parallel", "parallel"), 48),
        name="rwkv_prep",
    )(z, z, z, z, z, z, z, z, shift_rows, shift_rows, shift_rows, shift_rows,
      mu, mu, mu, mu, w0, a0, k_k, k_a, r_k, w2, a2, g2, ones)


def _rwkv_scan_kernel(tb, r_ref, w_ref, k_ref, v_ref, a_ref, b_ref, s0_ref, y_ref, sf_ref, s_ref):
    tblk = pl.program_id(1)

    @pl.when(tblk == 0)
    def _():
        s_ref[...] = s0_ref[0]

    lane = lax.broadcasted_iota(jnp.int32, (HEAD, LANES), 1)
    sub = lax.broadcasted_iota(jnp.int32, (HEAD, LANES), 0)
    lo = lane < HEAD
    diag = (lane % HEAD) == sub

    def head_sums(x):
        sa = jnp.sum(jnp.where(lo, x, 0.0), axis=1, keepdims=True)
        sb = jnp.sum(jnp.where(lo, 0.0, x), axis=1, keepdims=True)
        return jnp.where(lo, sa, sb)

    def step(t, carry):
        row = lambda ref: ref[0, pl.ds(t, 1), :]
        r_t, w_t, k_t, v_t, a_t, b_t = (row(x) for x in (r_ref, w_ref, k_ref, v_ref, a_ref, b_ref))
        w_t = jnp.exp(w_t)
        ys = []
        for p in range(N_PAIRS):
            sl = slice(p * LANES, (p + 1) * LANES)
            s = s_ref[p]
            sa = head_sums(s * a_t[:, sl])
            vc = head_sums(jnp.where(diag, v_t[:, sl], 0.0))
            s = s * w_t[:, sl] + sa * b_t[:, sl] + vc * k_t[:, sl]
            s_ref[p] = s
            yc = head_sums(s * r_t[:, sl])
            ys.append(jnp.sum(jnp.where(diag, yc, 0.0), axis=0, keepdims=True))
        y_ref[0, pl.ds(t, 1), :] = jnp.concatenate(ys, axis=1)
        return carry

    lax.fori_loop(0, tb, step, 0)

    @pl.when(tblk == pl.num_programs(1) - 1)
    def _():
        sf_ref[0] = s_ref[...]


def _rwkv_scan(r, w, k, v, a, b, s0, tb):
    bsz, t, _ = r.shape
    s0p = s0.reshape(bsz, N_PAIRS, 2, HEAD, HEAD).transpose(0, 1, 3, 2, 4).reshape(bsz, N_PAIRS, HEAD, LANES)
    seq = pl.BlockSpec((1, tb, MIX), lambda i, j: (i, j, 0))
    st = pl.BlockSpec((1, N_PAIRS, HEAD, LANES), lambda i, j: (i, 0, 0, 0))
    y, sf = pl.pallas_call(
        functools.partial(_rwkv_scan_kernel, tb),
        out_shape=(jax.ShapeDtypeStruct((bsz, t, MIX), F32),
                   jax.ShapeDtypeStruct((bsz, N_PAIRS, HEAD, LANES), F32)),
        grid=(bsz, t // tb),
        in_specs=[seq] * 6 + [st],
        out_specs=(seq, st),
        scratch_shapes=[pltpu.VMEM((N_PAIRS, HEAD, LANES), F32)],
        compiler_params=_cparams(("parallel", "arbitrary"), 48),
        name="rwkv_scan",
    )(r, w, k, v, a, b, s0p)
    sf = sf.reshape(bsz, N_PAIRS, HEAD, 2, HEAD).transpose(0, 1, 3, 2, 4).reshape(bsz, N_HEADS, HEAD, HEAD)
    return y, sf


CHUNK = 64
PAIR_GROUP = 12


def _rwkv_chunk_kernel(r_ref, w_ref, k_ref, v_ref, a_ref, b_ref, s0_ref, y_ref, sf_ref, s_ref):
    L = CHUNK
    assert 2 * L == LANES

    @pl.when(pl.program_id(0) == 0)
    def _():
        s_ref[...] = s0_ref[...]

    ri = lax.broadcasted_iota(jnp.int32, (L, L), 0)
    ci = lax.broadcasted_iota(jnp.int32, (L, L), 1)
    tri = (ci <= ri).astype(BF16)
    i2 = lax.broadcasted_iota(jnp.int32, (LANES, LANES), 0)
    j2 = lax.broadcasted_iota(jnp.int32, (LANES, LANES), 1)
    top, left = i2 < L, j2 < L
    s_idx, t_idx = j2 & (L - 1), i2 & (L - 1)
    is_top = jnp.where(top, 1, 0)
    keep_a = s_idx < t_idx + (1 - is_top)
    keep_b = s_idx < t_idx + is_top
    ul = top & left
    lr = jnp.logical_not(top | left)
    same_head = ul | lr
    eye = (i2 == j2).astype(F32)
    lo = lax.broadcasted_iota(jnp.int32, (1, LANES), 1) < HEAD
    zeros = jnp.zeros((L, LANES), F32)
    cat = lambda x, y: jnp.concatenate([x, y], axis=0)
    dot = lambda x, y: jnp.dot(x.astype(BF16), y.astype(BF16), preferred_element_type=F32)
    dot_nt = lambda x, y: lax.dot_general(x.astype(BF16), y.astype(BF16), _NT, preferred_element_type=F32)

    def setup(p):
        sl = slice(p * LANES, (p + 1) * LANES)
        r, lw, k, v, a, b = (x[:, sl] for x in (r_ref, w_ref, k_ref, v_ref, a_ref, b_ref))
        hi, mid, low = _split3(lw)
        c = dot(tri, hi) + dot(tri, mid) + dot(tri, low)
        c_last = c[L - 1:L, :]
        at, rt = a * jnp.exp(c - lw), r * jnp.exp(c)
        em = jnp.exp(-c)
        kh, bh = k * em, b * em
        erest = jnp.exp(c_last - c)

        s = s_ref[p]
        s_hi = s.astype(BF16)
        s_lo = s - s_hi.astype(F32)
        ar = cat(at, rt)
        ps = dot_nt(ar, s_hi) + dot_nt(ar, s_lo)

        m_a = jnp.where(keep_a, dot_nt(jnp.where(lo, ar, 0.0), cat(bh, kh)), 0.0)
        m_b = jnp.where(keep_b, dot_nt(jnp.where(lo, 0.0, cat(rt, at)), cat(kh, bh)), 0.0)
        vm_a, vm_b = jnp.where(lo, v, 0.0), jnp.where(lo, 0.0, v)
        rhs = ps[:L] + dot(m_a[:L], cat(zeros, vm_a)) + dot(m_b[L:], cat(vm_b, zeros))
        n = jnp.where(ul, m_a, 0.0) + jnp.where(lr, m_b, 0.0)
        return dict(sl=sl, p=p, v=v, s=s, ps=ps, m_a=m_a, m_b=m_b, vm_a=vm_a, vm_b=vm_b, rhs=rhs, n=n,
                    kb=cat(k * erest, b * erest), g_last=jnp.exp(c_last))

    def finish(d, inv):
        rhs = d["rhs"]
        u2 = dot(inv, cat(jnp.where(lo, rhs, 0.0), jnp.where(lo, 0.0, rhs)))
        y_ref[:, d["sl"]] = (d["ps"][L:] + dot(d["m_a"][L:], cat(u2[:L], d["vm_a"]))
                             + dot(d["m_b"][:L], cat(d["vm_b"], u2[L:])))
        vu = cat(d["v"], u2[:L] + u2[L:])
        upd = dot(vu.T, d["kb"])
        s_ref[d["p"]] = d["s"] * d["g_last"] + jnp.where(same_head, upd, 0.0)

    for g0 in range(0, N_PAIRS, PAIR_GROUP):
        ds = [setup(p) for p in range(g0, g0 + PAIR_GROUP)]
        qs = [d["n"] for d in ds]
        invs = [eye + d["n"] for d in ds]
        for _ in range(5):
            qs = [dot(q, q) for q in qs]
            invs = [inv + dot(inv, q) for inv, q in zip(invs, qs)]
        for d, inv in zip(ds, invs):
            finish(d, inv)

    @pl.when(pl.program_id(0) == pl.num_programs(0) - 1)
    def _():
        sf_ref[...] = s_ref[...]


def _rwkv_chunked(r, w, k, v, a, b, s0):
    t = r.shape[0]
    pair_eye = jnp.eye(2, dtype=F32)[None, :, None, :, None]
    s0p = (s0.reshape(N_PAIRS, 2, HEAD, 1, HEAD) * pair_eye).reshape(N_PAIRS, LANES, LANES)
    seq = pl.BlockSpec((CHUNK, MIX), lambda i: (i, 0))
    st = pl.BlockSpec((N_PAIRS, LANES, LANES), lambda i: (0, 0, 0))
    y, sf = pl.pallas_call(
        _rwkv_chunk_kernel,
        out_shape=(jax.ShapeDtypeStruct((t, MIX), F32), jax.ShapeDtypeStruct((N_PAIRS, LANES, LANES), F32)),
        grid=(t // CHUNK,),
        in_specs=[seq] * 6 + [st],
        out_specs=(seq, st),
        scratch_shapes=[pltpu.VMEM((N_PAIRS, LANES, LANES), F32)],
        compiler_params=_cparams(("arbitrary",), 48),
        name="rwkv_chunked",
    )(r, w, k, v, a, b, s0p)
    sf = sf.reshape(N_PAIRS, 2, HEAD, 2, HEAD)
    sf = jnp.stack([sf[:, 0, :, 0], sf[:, 1, :, 1]], axis=1).reshape(N_HEADS, HEAD, HEAD)
    return y, sf


def _rwkv_post_kernel(y_ref, bonus_ref, g_ref, lw_ref, lb_ref, ones_ref, o_ref):
    y = y_ref[...]
    ones = ones_ref[...]
    m = _dot_exact_rhs(y, ones) * (1.0 / HEAD)
    d = y - m
    var = _dot_exact_rhs(d * d, ones) * (1.0 / HEAD)
    yn = d * lax.rsqrt(var + GN_EPS) * lw_ref[...] + lb_ref[...]
    o_ref[...] = (yn + bonus_ref[...]) * g_ref[...]


def _rwkv_post(y, bonus, g, lnx_w, lnx_b, ones, tm):
    m = y.shape[0]
    blk = pl.BlockSpec((tm, LANES), lambda i, c: (i, c))
    vec = pl.BlockSpec((1, LANES), lambda i, c: (0, c))
    return pl.pallas_call(
        _rwkv_post_kernel,
        out_shape=jax.ShapeDtypeStruct((m, MIX), F32),
        grid=(m // tm, MIX // LANES),
        in_specs=[blk, blk, blk, vec, vec, pl.BlockSpec((LANES, LANES), lambda i, c: (0, 0))],
        out_specs=blk,
        compiler_params=_cparams(("parallel", "parallel"), 32),
        name="rwkv_post",
    )(y, bonus, g, lnx_w, lnx_b, ones)


def _sb_tile(z, carry, upper2, allowed):
    sp = jnp.log(1.0 + jnp.exp(-jnp.abs(z)))
    ls = jnp.minimum(z, 0.0) - sp
    l1m = ls - z
    if allowed is not None:
        l1m = jnp.where(allowed, l1m, 0.0)
    hi = l1m.astype(BF16)
    lo = (l1m - hi.astype(F32)).astype(BF16)
    suffix = jnp.dot(jnp.concatenate([hi, lo], axis=1), upper2, preferred_element_type=F32)
    a = jnp.exp(ls + suffix + carry)
    if allowed is not None:
        a = jnp.where(allowed, a, 0.0)
    return a, carry + jnp.sum(l1m, axis=1, keepdims=True)


def _sb_prompt_kernel(tq, q_ref, k_ref, v_ref, up_ref, o_ref, kb_ref, va_ref, vb_ref, acc_ref):
    qi = pl.program_id(1)
    lane = lax.broadcasted_iota(jnp.int32, (1, LANES), 1)
    lo = lane < HEAD

    @pl.when(qi == 0)
    def _():
        kb_ref[...] = k_ref[...].astype(BF16)
        v = v_ref[...]
        va_ref[...] = jnp.where(lo, v, 0.0).astype(BF16)
        vb_ref[...] = jnp.where(lo, 0.0, v).astype(BF16)

    q = q_ref[...] * (HEAD ** -0.5)
    qa = jnp.where(lo, q, 0.0).astype(BF16)
    qb = jnp.where(lo, 0.0, q).astype(BF16)
    upper2 = up_ref[...]

    def logits(j):
        kblk = kb_ref[pl.ds(pl.multiple_of(j * tq, tq), tq), :]
        return (lax.dot_general(qa, kblk, _NT, preferred_element_type=F32),
                lax.dot_general(qb, kblk, _NT, preferred_element_type=F32))

    def attend(j, za, zb, ca, cb, allowed):
        ks = pl.multiple_of(j * tq, tq)
        pa, ca = _sb_tile(za, ca, upper2, allowed)
        pb, cb = _sb_tile(zb, cb, upper2, allowed)
        acc_ref[...] += (jnp.dot(pa.astype(BF16), va_ref[pl.ds(ks, tq), :], preferred_element_type=F32)
                         + jnp.dot(pb.astype(BF16), vb_ref[pl.ds(ks, tq), :], preferred_element_type=F32))
        return ca, cb

    acc_ref[...] = jnp.zeros_like(acc_ref)
    zero = jnp.zeros((tq, 1), F32)
    below_diag = (lax.broadcasted_iota(jnp.int32, (tq, tq), 1) < lax.broadcasted_iota(jnp.int32, (tq, tq), 0))
    za, zb = logits(qi)
    ca, cb = attend(qi, za, zb, zero, zero, below_diag)

    def body(n, carry):
        ca, cb, za, zb = carry
        j = qi - 1 - n
        zna, znb = logits(jnp.maximum(j - 1, 0))
        ca, cb = attend(j, za, zb, ca, cb, None)
        return ca, cb, zna, znb

    za, zb = logits(jnp.maximum(qi - 1, 0))
    lax.fori_loop(0, qi, body, (ca, cb, za, zb))
    o_ref[...] = acc_ref[...]


def _sb_prompt(z, upper, tq):
    s = z.shape[0]
    nc = MIX // LANES
    return pl.pallas_call(
        functools.partial(_sb_prompt_kernel, tq),
        out_shape=jax.ShapeDtypeStruct((s, MIX), F32),
        grid=(nc, s // tq),
        in_specs=[
            pl.BlockSpec((tq, LANES), lambda c, i: (i, c)),
            pl.BlockSpec((s, LANES), lambda c, i: (0, nc + c)),
            pl.BlockSpec((s, LANES), lambda c, i: (0, 2 * nc + c)),
            pl.BlockSpec((2 * tq, tq), lambda c, i: (0, 0)),
        ],
        out_specs=pl.BlockSpec((tq, LANES), lambda c, i: (i, c)),
        scratch_shapes=[pltpu.VMEM((s, LANES), BF16)] * 3 + [pltpu.VMEM((tq, LANES), F32)],
        compiler_params=_cparams(("parallel", "arbitrary"), 48),
        name="sb_prompt",
    )(z, z, z, upper)


QROWS = 8


def _sb_sample_kernel(n_pages, pt_ref, q_ref, kn_ref, vn_ref, kc_ref, vc_ref, up_ref, o_ref,
                      acc_ref, carry_ref):
    p = pl.program_id(1)
    upper2 = up_ref[...]
    rows = N_HEADS * QROWS

    def tile(k_ref, v_ref, allowed):
        head = lambda ref, h: ref[0, pl.ds(h, PAGE, stride=N_HEADS), :].astype(BF16)
        zs = [lax.dot_general(q_ref[0, h], head(k_ref, h), _NT, preferred_element_type=F32)
              for h in range(N_HEADS)]
        a, carry = _sb_tile(jnp.concatenate(zs, axis=0), carry_ref[...], upper2, allowed)
        carry_ref[...] = carry
        for h in range(N_HEADS):
            acc_ref[h] += jnp.dot(a[h * QROWS:(h + 1) * QROWS].astype(BF16), head(v_ref, h),
                                  preferred_element_type=F32)

    @pl.when(p == 0)
    def _():
        acc_ref[...] = jnp.zeros_like(acc_ref)
        carry_ref[...] = jnp.zeros_like(carry_ref)
        key = lax.broadcasted_iota(jnp.int32, (rows, PAGE), 1)
        qry = lax.broadcasted_iota(jnp.int32, (rows, PAGE), 0) & (QROWS - 1)
        tile(kn_ref, vn_ref, key < qry)

    @pl.when(p > 0)
    def _():
        tile(kc_ref, vc_ref, None)

    @pl.when(p == n_pages)
    def _():
        o_ref[0] = acc_ref[...]


def _sb_sample(q, k_new, v_new, cache_k, cache_v, page_table, upper2):
    bsz = q.shape[0]
    n_pages = page_table.shape[1]

    def page(b, p, pt):
        return (pt[b, jnp.clip(n_pages - p, 0, n_pages - 1)], 0, 0)

    own = lambda b, p, pt: (b, 0, 0, 0)
    kv_block = (1, PAGE * N_HEADS, HEAD)
    grid_spec = pltpu.PrefetchScalarGridSpec(
        num_scalar_prefetch=1,
        grid=(bsz, n_pages + 1),
        in_specs=[
            pl.BlockSpec((1, N_HEADS, QROWS, HEAD), own),
            pl.BlockSpec(kv_block, lambda b, p, pt: (b, 0, 0)),
            pl.BlockSpec(kv_block, lambda b, p, pt: (b, 0, 0)),
            pl.BlockSpec(kv_block, page),
            pl.BlockSpec(kv_block, page),
            pl.BlockSpec((2 * PAGE, PAGE), lambda b, p, pt: (0, 0)),
        ],
        out_specs=pl.BlockSpec((1, N_HEADS, QROWS, HEAD), own),
        scratch_shapes=[pltpu.VMEM((N_HEADS, QROWS, HEAD), F32), pltpu.VMEM((N_HEADS * QROWS, 1), F32)],
    )
    return pl.pallas_call(
        functools.partial(_sb_sample_kernel, n_pages),
        out_shape=jax.ShapeDtypeStruct((bsz, N_HEADS, QROWS, HEAD), F32),
        grid_spec=grid_spec,
        compiler_params=_cparams(("parallel", "arbitrary"), 32),
        name="sb_sample",
    )(page_table, q, k_new, v_new, cache_k, cache_v, upper2)


def _rmsnorm_kernel(x_ref, g_ref, o_ref):
    o_ref[...] = _rms_rows(x_ref[...], g_ref[...])


def _rmsnorm(x, g, tm):
    m, d = x.shape
    return pl.pallas_call(
        _rmsnorm_kernel,
        out_shape=jax.ShapeDtypeStruct((m, d), F32),
        grid=(m // tm,),
        in_specs=[pl.BlockSpec((tm, d), lambda i: (i, 0)), pl.BlockSpec((1, d), lambda i: (0, 0))],
        out_specs=pl.BlockSpec((tm, d), lambda i: (i, 0)),
        compiler_params=_cparams(("parallel",), 32),
        name="rmsnorm",
    )(x, g.reshape(1, d))


def _pad_cols(x, width):
    return jnp.pad(x, ((0, 0), (0, width - x.shape[1])))


def _rwkv_cols(x):
    o = 3 * MIX
    parts = [x[:, :o],
             _pad_cols(x[:, o:o + DECAY_LORA], LANES),
             _pad_cols(x[:, o + DECAY_LORA:o + DECAY_LORA + ICLR_LORA], LANES),
             _pad_cols(x[:, o + DECAY_LORA + ICLR_LORA:RWKV_PROJ], 2 * LANES)]
    if x.shape[1] > RWKV_PROJ:
        parts.append(x[:, RWKV_PROJ:])
    return jnp.concatenate(parts, axis=1)


def _rwkv_uncols(x):
    o = 3 * MIX
    return jnp.concatenate([x[:, :o], x[:, o:o + DECAY_LORA], x[:, o + LANES:o + LANES + ICLR_LORA],
                            x[:, o + 2 * LANES:o + 2 * LANES + GATE_LORA]], axis=1)


def _pad_rows(x, rows):
    return jnp.pad(x, ((0, rows - x.shape[0]), (0, 0)))


def _tile(m, pref):
    return pref if m % pref == 0 else m


def kernel(x_prompt, x_sample, mem_prompt, state_rwkv_wkv, state_rwkv_shift, cache_sb_k, cache_sb_v, page_table, cache_mem_k, cache_mem_v, norm_mix, norm_ffn, norm_final, mem_norm, w_mem_kv, w_out, w_ffn_up, w_ffn_down, rwkv_w_in, rwkv_mu, rwkv_w0, rwkv_w2, rwkv_a0, rwkv_a2, rwkv_g2, rwkv_k_k, rwkv_k_a, rwkv_r_k, rwkv_lnx_w, rwkv_lnx_b, sb_w_in, sb_b_qk):
    bp, s, d = x_prompt.shape
    db, t_new, _ = x_sample.shape
    assert bp == 1, "the prompt path handles a single sequence"
    depth = norm_mix.shape[0]
    ms = db * t_new
    n_mem = mem_prompt.shape[1]
    xp = x_prompt.reshape(s, d)
    xs = x_sample.reshape(ms, d)
    tmp = _tile(s, 512)
    tms = ms

    idx = lax.broadcasted_iota(jnp.int32, (LANES, LANES), 0)
    jdx = lax.broadcasted_iota(jnp.int32, (LANES, LANES), 1)
    head_ones = ((idx // HEAD) == (jdx // HEAD)).astype(BF16)
    tq = _tile(s, 256)
    iq = lax.broadcasted_iota(jnp.int32, (tq, tq), 0)
    jq = lax.broadcasted_iota(jnp.int32, (tq, tq), 1)
    upper_q = jnp.tile((iq > jq).astype(BF16), (2, 1))
    upper_p = jnp.tile((idx > jdx).astype(BF16), (2, 1))

    mem_k, mem_v = [], []
    wkv_p, shift_p, wkv_s, shift_s = [], [], [], []
    sbk_p, sbv_p, sbk_s, sbv_s = [], [], [], []
    for i in range(depth):
        j = i // 2
        kv = _rms_matmul(mem_prompt.reshape(n_mem, d), mem_norm[i], w_mem_kv[i].astype(BF16),
                         jnp.zeros((2 * MEM_W,), F32), n_mem, 2 * MEM_W)
        mk, mv = kv[:, :MEM_W], kv[:, MEM_W:]
        mem_k.append(mk.reshape(1, n_mem, MEM_HEADS, MEM_HD))
        mem_v.append(mv.reshape(1, n_mem, MEM_HEADS, MEM_HD))
        cmk = cache_mem_k[i].reshape(db, n_mem, MEM_W)
        cmv = cache_mem_v[i].reshape(db, n_mem, MEM_W)
        if i % 2 == 0:
            w_in = _rwkv_cols(rwkv_w_in[j]).astype(BF16)
            n_in = w_in.shape[1]
            zero_b = jnp.zeros((n_in,), F32)
            zp = _rms_matmul(xp, norm_mix[i], w_in, zero_b, tmp, 512)
            zs = _rms_matmul(xs, norm_mix[i], w_in, zero_b, tms, 512)
            shift_p.append(_rwkv_uncols(zp[s - 1:s]))
            shift_s.append(_rwkv_uncols(zs.reshape(db, t_new, n_in)[:, t_new - 1]))
            wide = 3 * MIX + LORA_PAD
            mu = _rwkv_cols(rwkv_mu[j][None])
            vecs = [x.reshape(1, MIX) for x in (rwkv_w0[j], rwkv_a0[j], rwkv_k_k[j], rwkv_k_a[j], rwkv_r_k[j])]
            lora = (_pad_rows(rwkv_w2[j], LANES).astype(BF16), _pad_rows(rwkv_a2[j], LANES).astype(BF16),
                    _pad_rows(rwkv_g2[j], 2 * LANES).astype(BF16))
            lnx = (rwkv_lnx_w[j].reshape(1, MIX), rwkv_lnx_b[j].reshape(1, MIX))

            def rwkv(z, shift_rows, seq_len, tm, s0, nb, tb, tpost):
                r, w, k, v, a, b, g, bonus = _rwkv_prep(z, shift_rows, seq_len, tm, mu, *vecs, *lora, head_ones)
                if nb == 1 and seq_len % CHUNK == 0:
                    y, s_fin = _rwkv_chunked(r, w, k, v, a, b, s0[0])
                    s_fin = s_fin[None]
                else:
                    sh = lambda x: x.reshape(nb, seq_len, MIX)
                    y, s_fin = _rwkv_scan(sh(r), sh(w), sh(k), sh(v), sh(a), sh(b), s0, tb)
                return _rwkv_post(y.reshape(nb * seq_len, MIX), bonus, g, *lnx, head_ones, tpost), s_fin

            zero_shift = jnp.zeros((8, wide), F32)
            zero_state = jnp.zeros((1, N_HEADS, HEAD, HEAD), F32)
            mix_p, st_p = rwkv(zp, zero_shift, s, tmp, zero_state, 1, _tile(s, 256), tmp)
            shift_rows = jnp.repeat(_rwkv_cols(state_rwkv_shift[j]), t_new, axis=0)
            mix_s, st_s = rwkv(zs, shift_rows, t_new, tms, state_rwkv_wkv[j], db, t_new, tms)
            wkv_p.append(st_p)
            wkv_s.append(st_s)
            q_off = wide
        else:
            w_in = sb_w_in[j].astype(BF16)
            bias = jnp.concatenate([sb_b_qk[j], jnp.zeros((w_in.shape[1] - 2 * MIX,), F32)])
            zp = _rms_matmul(xp, norm_mix[i], w_in, bias, tmp, 512)
            zs = _rms_matmul(xs, norm_mix[i], w_in, bias, tms, 512)
            mix_p = _sb_prompt(zp, upper_q, tq)
            assert t_new < QROWS
            heads = lambda x: x.reshape(db, t_new, N_HEADS, HEAD)
            q_s, k_s, v_s = heads(zs[:, :MIX]), heads(zs[:, MIX:2 * MIX]), heads(zs[:, 2 * MIX:3 * MIX])
            q_hq = jnp.pad((q_s * (HEAD ** -0.5)).transpose(0, 2, 1, 3),
                           ((0, 0), (0, 0), (0, QROWS - t_new), (0, 0))).astype(BF16)
            n_pool = cache_sb_k.shape[1]
            rows = lambda x: x.reshape(x.shape[0], PAGE * N_HEADS, HEAD)
            pad_new = lambda x: rows(jnp.pad(x, ((0, 0), (0, PAGE - t_new), (0, 0), (0, 0))))
            o_hq = _sb_sample(q_hq, pad_new(k_s), pad_new(v_s), rows(cache_sb_k[j]), rows(cache_sb_v[j]),
                              page_table, upper_p)
            mix_s = o_hq[:, :, :t_new].transpose(0, 2, 1, 3).reshape(ms, MIX)
            sbk_p.append(zp[:, MIX:2 * MIX].reshape(1, s // PAGE, PAGE, N_HEADS, HEAD))
            sbv_p.append(zp[:, 2 * MIX:3 * MIX].reshape(1, s // PAGE, PAGE, N_HEADS, HEAD))
            sbk_s.append(k_s)
            sbv_s.append(v_s)
            q_off = 3 * MIX
        cp = _mem_attend(zp[:, q_off:q_off + MEM_W][None], mk[None], mv[None], tmp)[0]
        cs = _mem_attend(zs[:, q_off:q_off + MEM_W].reshape(db, t_new, MEM_W), cmk, cmv, t_new).reshape(ms, MEM_W)
        wo = w_out[i].astype(BF16)
        xp = _outproj(xp, mix_p, cp, wo[:MIX], wo[MIX:], tmp)
        xs = _outproj(xs, mix_s, cs, wo[:MIX], wo[MIX:], tms)
        wu, wd = w_ffn_up[i].astype(BF16), w_ffn_down[i].astype(BF16)
        xp = _ffn(xp, norm_ffn[i], wu, wd, tmp, 512)
        xs = _ffn(xs, norm_ffn[i], wu, wd, tms, 512)
    y_prompt = _rmsnorm(xp, norm_final, tmp).reshape(1, s, d)
    y_sample = _rmsnorm(xs, norm_final, tms).reshape(db, t_new, d)
    return (y_prompt, y_sample, jnp.stack(mem_k), jnp.stack(mem_v),
            jnp.stack(wkv_p), jnp.stack(shift_p), jnp.stack(wkv_s), jnp.stack(shift_s),
            jnp.stack(sbk_p), jnp.stack(sbv_p), jnp.stack(sbk_s), jnp.stack(sbv_s))
```

```python
import functools

import jax
import jax.numpy as jnp
from jax import lax
from jax.experimental import pallas as pl
from jax.experimental.pallas import tpu as pltpu

F32 = jnp.float32
BF16 = jnp.bfloat16

HEAD = 64
N_HEADS = 24
MIX = N_HEADS * HEAD
N_PAIRS = N_HEADS // 2
LANES = 128
MEM_HEADS = 4
MEM_HD = 128
MEM_W = MEM_HEADS * MEM_HD
DECAY_LORA = 64
ICLR_LORA = 64
GATE_LORA = 224
LORA_PAD = 512
RWKV_PROJ = 3 * MIX + DECAY_LORA + ICLR_LORA + GATE_LORA
RMS_EPS = 1e-6
GN_EPS = 64e-5
PAGE = 128

_NT = (((1,), (1,)), ((), ()))


def _cparams(sem, vmem_mb):
    return pltpu.CompilerParams(dimension_semantics=sem, vmem_limit_bytes=vmem_mb << 20)


def _split3(x):
    hi = x.astype(BF16)
    r1 = x - hi.astype(F32)
    mid = r1.astype(BF16)
    lo = (r1 - mid.astype(F32)).astype(BF16)
    return hi, mid, lo


def _dot_exact_rhs(x, ones_bf16):
    hi, mid, lo = _split3(x)
    d = lambda a: jnp.dot(a, ones_bf16, preferred_element_type=F32)
    return d(hi) + d(mid) + d(lo)


def _rms_rows(x, g):
    ms = jnp.mean(x * x, axis=-1, keepdims=True)
    return x * lax.rsqrt(ms + RMS_EPS) * g


def _rms_matmul_kernel(x_ref, g_ref, w_ref, b_ref, o_ref, xn_ref):
    @pl.when(pl.program_id(1) == 0)
    def _():
        xn_ref[...] = _rms_rows(x_ref[...], g_ref[...]).astype(BF16)

    o_ref[...] = jnp.dot(xn_ref[...], w_ref[...], preferred_element_type=F32) + b_ref[...]


def _rms_matmul(x, g, w, b, tm, tn):
    m, d = x.shape
    n = w.shape[1]
    return pl.pallas_call(
        _rms_matmul_kernel,
        out_shape=jax.ShapeDtypeStruct((m, n), F32),
        grid=(m // tm, n // tn),
        in_specs=[
            pl.BlockSpec((tm, d), lambda i, j: (i, 0)),
            pl.BlockSpec((1, d), lambda i, j: (0, 0)),
            pl.BlockSpec((d, tn), lambda i, j: (0, j)),
            pl.BlockSpec((1, tn), lambda i, j: (0, j)),
        ],
        out_specs=pl.BlockSpec((tm, tn), lambda i, j: (i, j)),
        scratch_shapes=[pltpu.VMEM((tm, d), BF16)],
        compiler_params=_cparams(("parallel", "arbitrary"), 48),
        name="rms_matmul",
    )(x, g.reshape(1, d), w, b.reshape(1, n))


def _ffn_kernel(x_ref, g_ref, wu_ref, wd_ref, o_ref, xn_ref):
    @pl.when(pl.program_id(1) == 0)
    def _():
        x = x_ref[...]
        xn_ref[...] = _rms_rows(x, g_ref[...]).astype(BF16)
        o_ref[...] = x

    h = jnp.maximum(jnp.dot(xn_ref[...], wu_ref[...], preferred_element_type=F32), 0.0)
    o_ref[...] += jnp.dot((h * h).astype(BF16), wd_ref[...], preferred_element_type=F32)


def _ffn(x, g, wu, wd, tm, tf):
    m, d = x.shape
    f = wu.shape[1]
    return pl.pallas_call(
        _ffn_kernel,
        out_shape=jax.ShapeDtypeStruct((m, d), F32),
        grid=(m // tm, f // tf),
        in_specs=[
            pl.BlockSpec((tm, d), lambda i, j: (i, 0)),
            pl.BlockSpec((1, d), lambda i, j: (0, 0)),
            pl.BlockSpec((d, tf), lambda i, j: (0, j)),
            pl.BlockSpec((tf, d), lambda i, j: (j, 0)),
        ],
        out_specs=pl.BlockSpec((tm, d), lambda i, j: (i, 0)),
        scratch_shapes=[pltpu.VMEM((tm, d), BF16)],
        compiler_params=_cparams(("parallel", "arbitrary"), 48),
        name="ffn",
    )(x, g.reshape(1, d), wu, wd)


def _outproj_kernel(x_ref, mix_ref, c_ref, w1_ref, w2_ref, o_ref):
    acc = jnp.dot(mix_ref[...].astype(BF16), w1_ref[...], preferred_element_type=F32)
    acc += jnp.dot(c_ref[...].astype(BF16), w2_ref[...], preferred_element_type=F32)
    o_ref[...] = x_ref[...] + acc


def _outproj(x, mix, c, w1, w2, tm):
    m, d = x.shape
    return pl.pallas_call(
        _outproj_kernel,
        out_shape=jax.ShapeDtypeStruct((m, d), F32),
        grid=(m // tm,),
        in_specs=[
            pl.BlockSpec((tm, d), lambda i: (i, 0)),
            pl.BlockSpec((tm, MIX), lambda i: (i, 0)),
            pl.BlockSpec((tm, MEM_W), lambda i: (i, 0)),
            pl.BlockSpec((MIX, d), lambda i: (0, 0)),
            pl.BlockSpec((MEM_W, d), lambda i: (0, 0)),
        ],
        out_specs=pl.BlockSpec((tm, d), lambda i: (i, 0)),
        compiler_params=_cparams(("parallel",), 48),
        name="outproj",
    )(x, mix, c, w1, w2)


def _mem_attend_kernel(q_ref, k_ref, v_ref, o_ref):
    q = q_ref[0]
    for h in range(MEM_HEADS):
        sl = slice(h * MEM_HD, (h + 1) * MEM_HD)
        s = lax.dot_general(q[:, sl].astype(BF16), k_ref[0, :, sl].astype(BF16), _NT,
                            preferred_element_type=F32) * (MEM_HD ** -0.5)
        e = jnp.exp(s - jnp.max(s, axis=-1, keepdims=True))
        p = e / jnp.sum(e, axis=-1, keepdims=True)
        o_ref[0, :, sl] = jnp.dot(p.astype(BF16), v_ref[0, :, sl].astype(BF16),
                                  preferred_element_type=F32)


def _mem_attend(q, mk, mv, tt):
    b, t, _ = q.shape
    nm = mk.shape[1]
    return pl.pallas_call(
        _mem_attend_kernel,
        out_shape=jax.ShapeDtypeStruct((b, t, MEM_W), F32),
        grid=(b, t // tt),
        in_specs=[
            pl.BlockSpec((1, tt, MEM_W), lambda i, j: (i, j, 0)),
            pl.BlockSpec((1, nm, MEM_W), lambda i, j: (i, 0, 0)),
            pl.BlockSpec((1, nm, MEM_W), lambda i, j: (i, 0, 0)),
        ],
        out_specs=pl.BlockSpec((1, tt, MEM_W), lambda i, j: (i, j, 0)),
        compiler_params=_cparams(("parallel", "parallel"), 32),
        name="mem_attend",
    )(q, mk, mv)


def _rwkv_prep_kernel(seq_len, tm,
                      pr_ref, pk_ref, pv_ref, pl_ref,
                      br_ref, bk_ref, bv_ref, bl_ref,
                      sr_ref, sk_ref, sv_ref, sl_ref,
                      mur_ref, muk_ref, muv_ref, mul_ref,
                      w0_ref, a0_ref, kk_ref, ka_ref, rk_ref,
                      w2_ref, a2_ref, g2_ref, ones_ref,
                      r_out, w_out, k_out, v_out, a_out, b_out, g_out, bonus_out):
    i = pl.program_id(0)
    row = lax.broadcasted_iota(jnp.int32, (tm, 1), 0)

    def shifted(cur_ref, before_ref, state_ref, mu_ref):
        cur = cur_ref[...]
        rolled = pltpu.roll(cur, 1, 0)
        if seq_len >= tm:
            first = jnp.where(i == 0, state_ref[0:1, :], before_ref[7:8, :])
            prev = jnp.where(row == 0, first, rolled)
        else:
            prev = jnp.where(row % seq_len == 0, state_ref[...], rolled)
        return cur + (prev - cur) * mu_ref[...]

    r = shifted(pr_ref, br_ref, sr_ref, mur_ref)
    k = shifted(pk_ref, bk_ref, sk_ref, muk_ref)
    v = shifted(pv_ref, bv_ref, sv_ref, muv_ref)
    lo = shifted(pl_ref, bl_ref, sl_ref, mul_ref)

    xw = jnp.tanh(lo[:, 0:128]).astype(BF16)
    xa = lo[:, 128:256].astype(BF16)
    xg = jax.nn.sigmoid(lo[:, 256:512]).astype(BF16)
    wl = w0_ref[...] + jnp.dot(xw, w2_ref[...], preferred_element_type=F32)
    w_log = -jax.nn.softplus(-wl) - 0.5
    log_decay = -jnp.exp(w_log)
    a = jax.nn.sigmoid(a0_ref[...] + jnp.dot(xa, a2_ref[...], preferred_element_type=F32))
    g = jnp.dot(xg, g2_ref[...], preferred_element_type=F32)

    ones = ones_ref[...]
    kk = k * kk_ref[...]
    nrm = jnp.sqrt(_dot_exact_rhs(kk * kk, ones))
    kk = kk / jnp.maximum(nrm, 1e-12)
    k2 = k * (1.0 + (a - 1.0) * ka_ref[...])
    bonus = _dot_exact_rhs(r * k2 * rk_ref[...], ones) * v

    r_out[...] = r
    w_out[...] = log_decay
    k_out[...] = k2
    v_out[...] = v
    a_out[...] = -kk
    b_out[...] = kk * a
    g_out[...] = g
    bonus_out[...] = bonus


def _rwkv_prep(z, shift_rows, seq_len, tm, mu, w0, a0, k_k, k_a, r_k, w2, a2, g2, ones):
    m = z.shape[0]
    nc = MIX // LANES
    lb = 3 * MIX // LORA_PAD
    tb = tm // 8
    srows = shift_rows.shape[0] if seq_len >= tm else tm

    def cur(off):
        return pl.BlockSpec((tm, LANES), lambda i, c: (i, off + c))

    def before(off):
        return pl.BlockSpec((8, LANES), lambda i, c: (jnp.maximum(i * tb - 1, 0), off + c))

    def state(off):
        if seq_len >= tm:
            return pl.BlockSpec((srows, LANES), lambda i, c: (0, off + c))
        return pl.BlockSpec((tm, LANES), lambda i, c: (i, off + c))

    def vec(off):
        return pl.BlockSpec((1, LANES), lambda i, c: (0, off + c))

    cur_l = pl.BlockSpec((tm, LORA_PAD), lambda i, c: (i, lb))
    before_l = pl.BlockSpec((8, LORA_PAD), lambda i, c: (jnp.maximum(i * tb - 1, 0), lb))
    if seq_len >= tm:
        state_l = pl.BlockSpec((srows, LORA_PAD), lambda i, c: (0, lb))
    else:
        state_l = pl.BlockSpec((tm, LORA_PAD), lambda i, c: (i, lb))
    vec_l = pl.BlockSpec((1, LORA_PAD), lambda i, c: (0, lb))

    out_spec = pl.BlockSpec((tm, LANES), lambda i, c: (i, c))
    out_sds = jax.ShapeDtypeStruct((m, MIX), F32)
    return pl.pallas_call(
        functools.partial(_rwkv_prep_kernel, seq_len, tm),
        out_shape=(out_sds,) * 8,
        grid=(m // tm, nc),
        in_specs=[
            cur(0), cur(nc), cur(2 * nc), cur_l,
            before(0), before(nc), before(2 * nc), before_l,
            state(0), state(nc), state(2 * nc), state_l,
            vec(0), vec(nc), vec(2 * nc), vec_l,
            vec(0), vec(0), vec(0), vec(0), vec(0),
            pl.BlockSpec((LANES, LANES), lambda i, c: (0, c)),
            pl.BlockSpec((LANES, LANES), lambda i, c: (0, c)),
            pl.BlockSpec((2 * LANES, LANES), lambda i, c: (0, c)),
            pl.BlockSpec((LANES, LANES), lambda i, c: (0, 0)),
        ],
        out_specs=(out_spec,) * 8,
        compiler_params=_cparams(("parallel", "parallel"), 48),
        name="rwkv_prep",
    )(z, z, z, z, z, z, z, z, shift_rows, shift_rows, shift_rows, shift_rows,
      mu, mu, mu, mu, w0, a0, k_k, k_a, r_k, w2, a2, g2, ones)


def _rwkv_scan_kernel(tb, r_ref, w_ref, k_ref, v_ref, a_ref, b_ref, s0_ref, y_ref, sf_ref, s_ref):
    tblk = pl.program_id(1)

    @pl.when(tblk == 0)
    def _():
        s_ref[...] = s0_ref[0]

    lane = lax.broadcasted_iota(jnp.int32, (HEAD, LANES), 1)
    sub = lax.broadcasted_iota(jnp.int32, (HEAD, LANES), 0)
    lo = lane < HEAD
    diag = (lane % HEAD) == sub

    def head_sums(x):
        sa = jnp.sum(jnp.where(lo, x, 0.0), axis=1, keepdims=True)
        sb = jnp.sum(jnp.where(lo, 0.0, x), axis=1, keepdims=True)
        return jnp.where(lo, sa, sb)

    def step(t, carry):
        row = lambda ref: ref[0, pl.ds(t, 1), :]
        r_t, w_t, k_t, v_t, a_t, b_t = (row(x) for x in (r_ref, w_ref, k_ref, v_ref, a_ref, b_ref))
        w_t = jnp.exp(w_t)
        ys = []
        for p in range(N_PAIRS):
            sl = slice(p * LANES, (p + 1) * LANES)
            s = s_ref[p]
            sa = head_sums(s * a_t[:, sl])
            vc = head_sums(jnp.where(diag, v_t[:, sl], 0.0))
            s = s * w_t[:, sl] + sa * b_t[:, sl] + vc * k_t[:, sl]
            s_ref[p] = s
            yc = head_sums(s * r_t[:, sl])
            ys.append(jnp.sum(jnp.where(diag, yc, 0.0), axis=0, keepdims=True))
        y_ref[0, pl.ds(t, 1), :] = jnp.concatenate(ys, axis=1)
        return carry

    lax.fori_loop(0, tb, step, 0)

    @pl.when(tblk == pl.num_programs(1) - 1)
    def _():
        sf_ref[0] = s_ref[...]


def _rwkv_scan(r, w, k, v, a, b, s0, tb):
    bsz, t, _ = r.shape
    s0p = s0.reshape(bsz, N_PAIRS, 2, HEAD, HEAD).transpose(0, 1, 3, 2, 4).reshape(bsz, N_PAIRS, HEAD, LANES)
    seq = pl.BlockSpec((1, tb, MIX), lambda i, j: (i, j, 0))
    st = pl.BlockSpec((1, N_PAIRS, HEAD, LANES), lambda i, j: (i, 0, 0, 0))
    y, sf = pl.pallas_call(
        functools.partial(_rwkv_scan_kernel, tb),
        out_shape=(jax.ShapeDtypeStruct((bsz, t, MIX), F32),
                   jax.ShapeDtypeStruct((bsz, N_PAIRS, HEAD, LANES), F32)),
        grid=(bsz, t // tb),
        in_specs=[seq] * 6 + [st],
        out_specs=(seq, st),
        scratch_shapes=[pltpu.VMEM((N_PAIRS, HEAD, LANES), F32)],
        compiler_params=_cparams(("parallel", "arbitrary"), 48),
        name="rwkv_scan",
    )(r, w, k, v, a, b, s0p)
    sf = sf.reshape(bsz, N_PAIRS, HEAD, 2, HEAD).transpose(0, 1, 3, 2, 4).reshape(bsz, N_HEADS, HEAD, HEAD)
    return y, sf


CHUNK = 64
PAIR_GROUP = 12


def _rwkv_chunk_kernel(r_ref, w_ref, k_ref, v_ref, a_ref, b_ref, s0_ref, y_ref, sf_ref, s_ref):
    L = CHUNK
    assert 2 * L == LANES

    @pl.when(pl.program_id(0) == 0)
    def _():
        s_ref[...] = s0_ref[...]

    ri = lax.broadcasted_iota(jnp.int32, (L, L), 0)
    ci = lax.broadcasted_iota(jnp.int32, (L, L), 1)
    tri = (ci <= ri).astype(BF16)
    i2 = lax.broadcasted_iota(jnp.int32, (LANES, LANES), 0)
    j2 = lax.broadcasted_iota(jnp.int32, (LANES, LANES), 1)
    top, left = i2 < L, j2 < L
    s_idx, t_idx = j2 & (L - 1), i2 & (L - 1)
    is_top = jnp.where(top, 1, 0)
    keep_a = s_idx < t_idx + (1 - is_top)
    keep_b = s_idx < t_idx + is_top
    ul = top & left
    lr = jnp.logical_not(top | left)
    same_head = ul | lr
    eye = (i2 == j2).astype(F32)
    lo = lax.broadcasted_iota(jnp.int32, (1, LANES), 1) < HEAD
    zeros = jnp.zeros((L, LANES), F32)
    cat = lambda x, y: jnp.concatenate([x, y], axis=0)
    dot = lambda x, y: jnp.dot(x.astype(BF16), y.astype(BF16), preferred_element_type=F32)
    dot_nt = lambda x, y: lax.dot_general(x.astype(BF16), y.astype(BF16), _NT, preferred_element_type=F32)

    def setup(p):
        sl = slice(p * LANES, (p + 1) * LANES)
        r, lw, k, v, a, b = (x[:, sl] for x in (r_ref, w_ref, k_ref, v_ref, a_ref, b_ref))
        hi, mid, low = _split3(lw)
        c = dot(tri, hi) + dot(tri, mid) + dot(tri, low)
        c_last = c[L - 1:L, :]
        at, rt = a * jnp.exp(c - lw), r * jnp.exp(c)
        em = jnp.exp(-c)
        kh, bh = k * em, b * em
        erest = jnp.exp(c_last - c)

        s = s_ref[p]
        s_hi = s.astype(BF16)
        s_lo = s - s_hi.astype(F32)
        ar = cat(at, rt)
        ps = dot_nt(ar, s_hi) + dot_nt(ar, s_lo)

        m_a = jnp.where(keep_a, dot_nt(jnp.where(lo, ar, 0.0), cat(bh, kh)), 0.0)
        m_b = jnp.where(keep_b, dot_nt(jnp.where(lo, 0.0, cat(rt, at)), cat(kh, bh)), 0.0)
        vm_a, vm_b = jnp.where(lo, v, 0.0), jnp.where(lo, 0.0, v)
        rhs = ps[:L] + dot(m_a[:L], cat(zeros, vm_a)) + dot(m_b[L:], cat(vm_b, zeros))
        n = jnp.where(ul, m_a, 0.0) + jnp.where(lr, m_b, 0.0)
        return dict(sl=sl, p=p, v=v, s=s, ps=ps, m_a=m_a, m_b=m_b, vm_a=vm_a, vm_b=vm_b, rhs=rhs, n=n,
                    kb=cat(k * erest, b * erest), g_last=jnp.exp(c_last))

    def finish(d, inv):
        rhs = d["rhs"]
        u2 = dot(inv, cat(jnp.where(lo, rhs, 0.0), jnp.where(lo, 0.0, rhs)))
        y_ref[:, d["sl"]] = (d["ps"][L:] + dot(d["m_a"][L:], cat(u2[:L], d["vm_a"]))
                             + dot(d["m_b"][:L], cat(d["vm_b"], u2[L:])))
        vu = cat(d["v"], u2[:L] + u2[L:])
        upd = dot(vu.T, d["kb"])
        s_ref[d["p"]] = d["s"] * d["g_last"] + jnp.where(same_head, upd, 0.0)

    for g0 in range(0, N_PAIRS, PAIR_GROUP):
        ds = [setup(p) for p in range(g0, g0 + PAIR_GROUP)]
        qs = [d["n"] for d in ds]
        invs = [eye + d["n"] for d in ds]
        for _ in range(5):
            qs = [dot(q, q) for q in qs]
            invs = [inv + dot(inv, q) for inv, q in zip(invs, qs)]
        for d, inv in zip(ds, invs):
            finish(d, inv)

    @pl.when(pl.program_id(0) == pl.num_programs(0) - 1)
    def _():
        sf_ref[...] = s_ref[...]


def _rwkv_chunked(r, w, k, v, a, b, s0):
    t = r.shape[0]
    pair_eye = jnp.eye(2, dtype=F32)[None, :, None, :, None]
    s0p = (s0.reshape(N_PAIRS, 2, HEAD, 1, HEAD) * pair_eye).reshape(N_PAIRS, LANES, LANES)
    seq = pl.BlockSpec((CHUNK, MIX), lambda i: (i, 0))
    st = pl.BlockSpec((N_PAIRS, LANES, LANES), lambda i: (0, 0, 0))
    y, sf = pl.pallas_call(
        _rwkv_chunk_kernel,
        out_shape=(jax.ShapeDtypeStruct((t, MIX), F32), jax.ShapeDtypeStruct((N_PAIRS, LANES, LANES), F32)),
        grid=(t // CHUNK,),
        in_specs=[seq] * 6 + [st],
        out_specs=(seq, st),
        scratch_shapes=[pltpu.VMEM((N_PAIRS, LANES, LANES), F32)],
        compiler_params=_cparams(("arbitrary",), 48),
        name="rwkv_chunked",
    )(r, w, k, v, a, b, s0p)
    sf = sf.reshape(N_PAIRS, 2, HEAD, 2, HEAD)
    sf = jnp.stack([sf[:, 0, :, 0], sf[:, 1, :, 1]], axis=1).reshape(N_HEADS, HEAD, HEAD)
    return y, sf


def _rwkv_post_kernel(y_ref, bonus_ref, g_ref, lw_ref, lb_ref, ones_ref, o_ref):
    y = y_ref[...]
    ones = ones_ref[...]
    m = _dot_exact_rhs(y, ones) * (1.0 / HEAD)
    d = y - m
    var = _dot_exact_rhs(d * d, ones) * (1.0 / HEAD)
    yn = d * lax.rsqrt(var + GN_EPS) * lw_ref[...] + lb_ref[...]
    o_ref[...] = (yn + bonus_ref[...]) * g_ref[...]


def _rwkv_post(y, bonus, g, lnx_w, lnx_b, ones, tm):
    m = y.shape[0]
    blk = pl.BlockSpec((tm, LANES), lambda i, c: (i, c))
    vec = pl.BlockSpec((1, LANES), lambda i, c: (0, c))
    return pl.pallas_call(
        _rwkv_post_kernel,
        out_shape=jax.ShapeDtypeStruct((m, MIX), F32),
        grid=(m // tm, MIX // LANES),
        in_specs=[blk, blk, blk, vec, vec, pl.BlockSpec((LANES, LANES), lambda i, c: (0, 0))],
        out_specs=blk,
        compiler_params=_cparams(("parallel", "parallel"), 32),
        name="rwkv_post",
    )(y, bonus, g, lnx_w, lnx_b, ones)


def _sb_tile(z, carry, lower, allowed, two_pass):
    neg_abs = pltpu.bitcast(pltpu.bitcast(z, jnp.uint32) | jnp.uint32(0x80000000), F32)
    sp = jnp.maximum(z, 0.0) + jnp.log(1.0 + jnp.exp(neg_abs))
    if allowed is not None:
        sp = jnp.where(allowed, sp, 0.0)
    lhs = sp.astype(BF16)
    if two_pass:
        lhs = jnp.concatenate([lhs, (sp - lhs.astype(F32)).astype(BF16)], axis=1)
    inside = jnp.dot(lhs, lower, preferred_element_type=F32)
    a = jnp.exp(z - inside - carry)
    if allowed is not None:
        a = jnp.where(allowed, a, 0.0)
    return a, carry + inside[:, 0:1]


def _sb_prompt_kernel(tq, q_ref, k_ref, v_ref, up_ref, o_ref, kb_ref, va_ref, vb_ref, acc_ref):
    qi = pl.program_id(1)
    lane = lax.broadcasted_iota(jnp.int32, (1, LANES), 1)
    lo = lane < HEAD

    @pl.when(qi == 0)
    def _():
        kb_ref[...] = k_ref[...].astype(BF16)
        v = v_ref[...]
        va_ref[...] = jnp.where(lo, v, 0.0).astype(BF16)
        vb_ref[...] = jnp.where(lo, 0.0, v).astype(BF16)

    q = q_ref[...] * (HEAD ** -0.5)
    qa = jnp.where(lo, q, 0.0).astype(BF16)
    qb = jnp.where(lo, 0.0, q).astype(BF16)
    lower = up_ref[...]

    def logits(j):
        kblk = kb_ref[pl.ds(pl.multiple_of(j * tq, tq), tq), :]
        return (lax.dot_general(qa, kblk, _NT, preferred_element_type=F32),
                lax.dot_general(qb, kblk, _NT, preferred_element_type=F32))

    def accumulate(j, pa, pb):
        ks = pl.multiple_of(j * tq, tq)
        acc_ref[...] += (jnp.dot(pa.astype(BF16), va_ref[pl.ds(ks, tq), :], preferred_element_type=F32)
                         + jnp.dot(pb.astype(BF16), vb_ref[pl.ds(ks, tq), :], preferred_element_type=F32))

    acc_ref[...] = jnp.zeros_like(acc_ref)
    zero = jnp.zeros((tq, 1), F32)
    below_diag = (lax.broadcasted_iota(jnp.int32, (tq, tq), 1) < lax.broadcasted_iota(jnp.int32, (tq, tq), 0))
    za, zb = logits(qi)
    pa, ca = _sb_tile(za, zero, lower, below_diag, False)
    pb, cb = _sb_tile(zb, zero, lower, below_diag, False)
    accumulate(qi, pa, pb)

    def body(n, carry):
        ca, cb, za, zb = carry
        j = qi - 1 - n
        zna, znb = logits(jnp.maximum(j - 1, 0))
        pa, ca = _sb_tile(za, ca, lower, None, False)
        pb, cb = _sb_tile(zb, cb, lower, None, False)
        accumulate(j, pa, pb)
        return ca, cb, zna, znb

    za, zb = logits(jnp.maximum(qi - 1, 0))
    lax.fori_loop(0, qi, body, (ca, cb, za, zb))
    o_ref[...] = acc_ref[...]


def _sb_prompt(z, upper, tq):
    s = z.shape[0]
    nc = MIX // LANES
    return pl.pallas_call(
        functools.partial(_sb_prompt_kernel, tq),
        out_shape=jax.ShapeDtypeStruct((s, MIX), F32),
        grid=(nc, s // tq),
        in_specs=[
            pl.BlockSpec((tq, LANES), lambda c, i: (i, c)),
            pl.BlockSpec((s, LANES), lambda c, i: (0, nc + c)),
            pl.BlockSpec((s, LANES), lambda c, i: (0, 2 * nc + c)),
            pl.BlockSpec((tq, tq), lambda c, i: (0, 0)),
        ],
        out_specs=pl.BlockSpec((tq, LANES), lambda c, i: (i, c)),
        scratch_shapes=[pltpu.VMEM((s, LANES), BF16)] * 3 + [pltpu.VMEM((tq, LANES), F32)],
        compiler_params=_cparams(("parallel", "arbitrary"), 48),
        name="sb_prompt",
    )(z, z, z, upper)


QROWS = 8


def _sb_sample_kernel(n_pages, pt_ref, q_ref, kn_ref, vn_ref, kc_ref, vc_ref, up_ref, o_ref,
                      acc_ref, carry_ref):
    p = pl.program_id(1)
    lower2 = up_ref[...]
    rows = N_HEADS * QROWS

    def tile(k_ref, v_ref, allowed):
        zs = [jnp.dot(q_ref[0, h], k_ref[0, h].astype(BF16), preferred_element_type=F32)
              for h in range(N_HEADS)]
        a, carry = _sb_tile(jnp.concatenate(zs, axis=0), carry_ref[...], lower2, allowed, True)
        carry_ref[...] = carry
        for h in range(N_HEADS):
            acc_ref[h] += lax.dot_general(a[h * QROWS:(h + 1) * QROWS].astype(BF16), v_ref[0, h].astype(BF16),
                                          _NT, preferred_element_type=F32)

    @pl.when(p == 0)
    def _():
        acc_ref[...] = jnp.zeros_like(acc_ref)
        carry_ref[...] = jnp.zeros_like(carry_ref)
        key = lax.broadcasted_iota(jnp.int32, (rows, PAGE), 1)
        qry = lax.broadcasted_iota(jnp.int32, (rows, PAGE), 0) & (QROWS - 1)
        tile(kn_ref, vn_ref, key < qry)

    @pl.when(p > 0)
    def _():
        tile(kc_ref, vc_ref, None)

    @pl.when(p == n_pages)
    def _():
        o_ref[0] = acc_ref[...]


def _sb_sample(q, k_new, v_new, cache_k, cache_v, page_table, lower2):
    bsz = q.shape[0]
    n_pages = page_table.shape[1]

    def page(b, p, pt):
        return (pt[b, jnp.clip(n_pages - p, 0, n_pages - 1)], 0, 0, 0)

    own = lambda b, p, pt: (b, 0, 0, 0)
    kv_block = (1, N_HEADS, HEAD, PAGE)
    grid_spec = pltpu.PrefetchScalarGridSpec(
        num_scalar_prefetch=1,
        grid=(bsz, n_pages + 1),
        in_specs=[
            pl.BlockSpec((1, N_HEADS, QROWS, HEAD), own),
            pl.BlockSpec(kv_block, own),
            pl.BlockSpec(kv_block, own),
            pl.BlockSpec(kv_block, page),
            pl.BlockSpec(kv_block, page),
            pl.BlockSpec((2 * PAGE, PAGE), lambda b, p, pt: (0, 0)),
        ],
        out_specs=pl.BlockSpec((1, N_HEADS, QROWS, HEAD), own),
        scratch_shapes=[pltpu.VMEM((N_HEADS, QROWS, HEAD), F32), pltpu.VMEM((N_HEADS * QROWS, 1), F32)],
    )
    return pl.pallas_call(
        functools.partial(_sb_sample_kernel, n_pages),
        out_shape=jax.ShapeDtypeStruct((bsz, N_HEADS, QROWS, HEAD), F32),
        grid_spec=grid_spec,
        compiler_params=_cparams(("parallel", "arbitrary"), 32),
        name="sb_sample",
    )(page_table, q, k_new, v_new, cache_k, cache_v, lower2)


def _rmsnorm_kernel(x_ref, g_ref, o_ref):
    o_ref[...] = _rms_rows(x_ref[...], g_ref[...])


def _rmsnorm(x, g, tm):
    m, d = x.shape
    return pl.pallas_call(
        _rmsnorm_kernel,
        out_shape=jax.ShapeDtypeStruct((m, d), F32),
        grid=(m // tm,),
        in_specs=[pl.BlockSpec((tm, d), lambda i: (i, 0)), pl.BlockSpec((1, d), lambda i: (0, 0))],
        out_specs=pl.BlockSpec((tm, d), lambda i: (i, 0)),
        compiler_params=_cparams(("parallel",), 32),
        name="rmsnorm",
    )(x, g.reshape(1, d))


def _pad_cols(x, width):
    return jnp.pad(x, ((0, 0), (0, width - x.shape[1])))


def _rwkv_cols(x):
    o = 3 * MIX
    parts = [x[:, :o],
             _pad_cols(x[:, o:o + DECAY_LORA], LANES),
             _pad_cols(x[:, o + DECAY_LORA:o + DECAY_LORA + ICLR_LORA], LANES),
             _pad_cols(x[:, o + DECAY_LORA + ICLR_LORA:RWKV_PROJ], 2 * LANES)]
    if x.shape[1] > RWKV_PROJ:
        parts.append(x[:, RWKV_PROJ:])
    return jnp.concatenate(parts, axis=1)


def _rwkv_uncols(x):
    o = 3 * MIX
    return jnp.concatenate([x[:, :o], x[:, o:o + DECAY_LORA], x[:, o + LANES:o + LANES + ICLR_LORA],
                            x[:, o + 2 * LANES:o + 2 * LANES + GATE_LORA]], axis=1)


def _pad_rows(x, rows):
    return jnp.pad(x, ((0, rows - x.shape[0]), (0, 0)))


def _tile(m, pref):
    return pref if m % pref == 0 else m


def kernel(x_prompt, x_sample, mem_prompt, state_rwkv_wkv, state_rwkv_shift, cache_sb_k, cache_sb_v, page_table, cache_mem_k, cache_mem_v, norm_mix, norm_ffn, norm_final, mem_norm, w_mem_kv, w_out, w_ffn_up, w_ffn_down, rwkv_w_in, rwkv_mu, rwkv_w0, rwkv_w2, rwkv_a0, rwkv_a2, rwkv_g2, rwkv_k_k, rwkv_k_a, rwkv_r_k, rwkv_lnx_w, rwkv_lnx_b, sb_w_in, sb_b_qk):
    bp, s, d = x_prompt.shape
    db, t_new, _ = x_sample.shape
    assert bp == 1, "the prompt path handles a single sequence"
    depth = norm_mix.shape[0]
    ms = db * t_new
    n_mem = mem_prompt.shape[1]
    xp = x_prompt.reshape(s, d)
    xs = x_sample.reshape(ms, d)
    tmp = _tile(s, 512)
    tms = ms

    idx = lax.broadcasted_iota(jnp.int32, (LANES, LANES), 0)
    jdx = lax.broadcasted_iota(jnp.int32, (LANES, LANES), 1)
    head_ones = ((idx // HEAD) == (jdx // HEAD)).astype(BF16)
    tq = _tile(s, 256)
    iq = lax.broadcasted_iota(jnp.int32, (tq, tq), 0)
    jq = lax.broadcasted_iota(jnp.int32, (tq, tq), 1)
    lower_q = (iq >= jq).astype(BF16)
    lower_p = jnp.tile((idx >= jdx).astype(BF16), (2, 1))

    mem_k, mem_v = [], []
    wkv_p, shift_p, wkv_s, shift_s = [], [], [], []
    sbk_p, sbv_p, sbk_s, sbv_s = [], [], [], []
    for i in range(depth):
        j = i // 2
        kv = _rms_matmul(mem_prompt.reshape(n_mem, d), mem_norm[i], w_mem_kv[i].astype(BF16),
                         jnp.zeros((2 * MEM_W,), F32), n_mem, 2 * MEM_W)
        mk, mv = kv[:, :MEM_W], kv[:, MEM_W:]
        mem_k.append(mk.reshape(1, n_mem, MEM_HEADS, MEM_HD))
        mem_v.append(mv.reshape(1, n_mem, MEM_HEADS, MEM_HD))
        cmk = cache_mem_k[i].reshape(db, n_mem, MEM_W)
        cmv = cache_mem_v[i].reshape(db, n_mem, MEM_W)
        if i % 2 == 0:
            w_in = _rwkv_cols(rwkv_w_in[j]).astype(BF16)
            n_in = w_in.shape[1]
            zero_b = jnp.zeros((n_in,), F32)
            zp = _rms_matmul(xp, norm_mix[i], w_in, zero_b, tmp, 512)
            zs = _rms_matmul(xs, norm_mix[i], w_in, zero_b, tms, 512)
            shift_p.append(_rwkv_uncols(zp[s - 1:s]))
            shift_s.append(_rwkv_uncols(zs.reshape(db, t_new, n_in)[:, t_new - 1]))
            wide = 3 * MIX + LORA_PAD
            mu = _rwkv_cols(rwkv_mu[j][None])
            vecs = [x.reshape(1, MIX) for x in (rwkv_w0[j], rwkv_a0[j], rwkv_k_k[j], rwkv_k_a[j], rwkv_r_k[j])]
            lora = (_pad_rows(rwkv_w2[j], LANES).astype(BF16), _pad_rows(rwkv_a2[j], LANES).astype(BF16),
                    _pad_rows(rwkv_g2[j], 2 * LANES).astype(BF16))
            lnx = (rwkv_lnx_w[j].reshape(1, MIX), rwkv_lnx_b[j].reshape(1, MIX))

            def rwkv(z, shift_rows, seq_len, tm, s0, nb, tb, tpost):
                r, w, k, v, a, b, g, bonus = _rwkv_prep(z, shift_rows, seq_len, tm, mu, *vecs, *lora, head_ones)
                if nb == 1 and seq_len % CHUNK == 0:
                    y, s_fin = _rwkv_chunked(r, w, k, v, a, b, s0[0])
                    s_fin = s_fin[None]
                else:
                    sh = lambda x: x.reshape(nb, seq_len, MIX)
                    y, s_fin = _rwkv_scan(sh(r), sh(w), sh(k), sh(v), sh(a), sh(b), s0, tb)
                return _rwkv_post(y.reshape(nb * seq_len, MIX), bonus, g, *lnx, head_ones, tpost), s_fin

            zero_shift = jnp.zeros((8, wide), F32)
            zero_state = jnp.zeros((1, N_HEADS, HEAD, HEAD), F32)
            mix_p, st_p = rwkv(zp, zero_shift, s, tmp, zero_state, 1, _tile(s, 256), tmp)
            shift_rows = jnp.repeat(_rwkv_cols(state_rwkv_shift[j]), t_new, axis=0)
            mix_s, st_s = rwkv(zs, shift_rows, t_new, tms, state_rwkv_wkv[j], db, t_new, tms)
            wkv_p.append(st_p)
            wkv_s.append(st_s)
            q_off = wide
        else:
            w_in = sb_w_in[j].astype(BF16)
            bias = jnp.concatenate([sb_b_qk[j], jnp.zeros((w_in.shape[1] - 2 * MIX,), F32)])
            zp = _rms_matmul(xp, norm_mix[i], w_in, bias, tmp, 512)
            zs = _rms_matmul(xs, norm_mix[i], w_in, bias, tms, 512)
            mix_p = _sb_prompt(zp, lower_q, tq)
            assert t_new < QROWS
            heads = lambda x: x.reshape(db, t_new, N_HEADS, HEAD)
            q_s, k_s, v_s = heads(zs[:, :MIX]), heads(zs[:, MIX:2 * MIX]), heads(zs[:, 2 * MIX:3 * MIX])
            q_hq = jnp.pad((q_s * (HEAD ** -0.5)).transpose(0, 2, 1, 3),
                           ((0, 0), (0, 0), (0, QROWS - t_new), (0, 0))).astype(BF16)
            keys_last = lambda x: x.transpose(0, 2, 3, 1)
            pad_new = lambda x: keys_last(jnp.pad(x, ((0, 0), (0, PAGE - t_new), (0, 0), (0, 0))))
            o_hq = _sb_sample(q_hq, pad_new(k_s), pad_new(v_s), keys_last(cache_sb_k[j]),
                              keys_last(cache_sb_v[j]), page_table, lower_p)
            mix_s = o_hq[:, :, :t_new].transpose(0, 2, 1, 3).reshape(ms, MIX)
            sbk_p.append(zp[:, MIX:2 * MIX].reshape(1, s // PAGE, PAGE, N_HEADS, HEAD))
            sbv_p.append(zp[:, 2 * MIX:3 * MIX].reshape(1, s // PAGE, PAGE, N_HEADS, HEAD))
            sbk_s.append(k_s)
            sbv_s.append(v_s)
            q_off = 3 * MIX
        cp = _mem_attend(zp[:, q_off:q_off + MEM_W][None], mk[None], mv[None], tmp)[0]
        cs = _mem_attend(zs[:, q_off:q_off + MEM_W].reshape(db, t_new, MEM_W), cmk, cmv, t_new).reshape(ms, MEM_W)
        wo = w_out[i].astype(BF16)
        xp = _outproj(xp, mix_p, cp, wo[:MIX], wo[MIX:], tmp)
        xs = _outproj(xs, mix_s, cs, wo[:MIX], wo[MIX:], tms)
        wu, wd = w_ffn_up[i].astype(BF16), w_ffn_down[i].astype(BF16)
        xp = _ffn(xp, norm_ffn[i], wu, wd, tmp, 512)
        xs = _ffn(xs, norm_ffn[i], wu, wd, tms, 512)
    y_prompt = _rmsnorm(xp, norm_final, tmp).reshape(1, s, d)
    y_sample = _rmsnorm(xs, norm_final, tms).reshape(db, t_new, d)
    return (y_prompt, y_sample, jnp.stack(mem_k), jnp.stack(mem_v),
            jnp.stack(wkv_p), jnp.stack(shift_p), jnp.stack(wkv_s), jnp.stack(shift_s),
            jnp.stack(sbk_p), jnp.stack(sbv_p), jnp.stack(sbk_s), jnp.stack(sbv_s))
```

```python
import functools

import jax
import jax.numpy as jnp
from jax import lax
from jax.experimental import pallas as pl
from jax.experimental.pallas import tpu as pltpu

F32 = jnp.float32
BF16 = jnp.bfloat16

HEAD = 64
N_HEADS = 24
MIX = N_HEADS * HEAD
N_PAIRS = N_HEADS // 2
LANES = 128
MEM_HEADS = 4
MEM_HD = 128
MEM_W = MEM_HEADS * MEM_HD
DECAY_LORA = 64
ICLR_LORA = 64
GATE_LORA = 224
LORA_PAD = 512
RWKV_PROJ = 3 * MIX + DECAY_LORA + ICLR_LORA + GATE_LORA
RMS_EPS = 1e-6
GN_EPS = 64e-5
PAGE = 128

_NT = (((1,), (1,)), ((), ()))


def _cparams(sem, vmem_mb):
    return pltpu.CompilerParams(dimension_semantics=sem, vmem_limit_bytes=vmem_mb << 20)


def _split3(x):
    hi = x.astype(BF16)
    r1 = x - hi.astype(F32)
    mid = r1.astype(BF16)
    lo = (r1 - mid.astype(F32)).astype(BF16)
    return hi, mid, lo


def _dot_exact_rhs(x, ones_bf16):
    hi, mid, lo = _split3(x)
    d = lambda a: jnp.dot(a, ones_bf16, preferred_element_type=F32)
    return d(hi) + d(mid) + d(lo)


def _rms_rows(x, g):
    ms = jnp.mean(x * x, axis=-1, keepdims=True)
    return x * lax.rsqrt(ms + RMS_EPS) * g


def _rms_matmul_kernel(x_ref, g_ref, w_ref, b_ref, o_ref, xn_ref):
    @pl.when(pl.program_id(1) == 0)
    def _():
        xn_ref[...] = _rms_rows(x_ref[...], g_ref[...]).astype(BF16)

    o_ref[...] = jnp.dot(xn_ref[...], w_ref[...], preferred_element_type=F32) + b_ref[...]


def _rms_matmul(x, g, w, b, tm, tn):
    m, d = x.shape
    n = w.shape[1]
    return pl.pallas_call(
        _rms_matmul_kernel,
        out_shape=jax.ShapeDtypeStruct((m, n), F32),
        grid=(m // tm, n // tn),
        in_specs=[
            pl.BlockSpec((tm, d), lambda i, j: (i, 0)),
            pl.BlockSpec((1, d), lambda i, j: (0, 0)),
            pl.BlockSpec((d, tn), lambda i, j: (0, j)),
            pl.BlockSpec((1, tn), lambda i, j: (0, j)),
        ],
        out_specs=pl.BlockSpec((tm, tn), lambda i, j: (i, j)),
        scratch_shapes=[pltpu.VMEM((tm, d), BF16)],
        compiler_params=_cparams(("parallel", "arbitrary"), 48),
        name="rms_matmul",
    )(x, g.reshape(1, d), w, b.reshape(1, n))


def _ffn_kernel(x_ref, g_ref, wu_ref, wd_ref, o_ref, xn_ref):
    @pl.when(pl.program_id(1) == 0)
    def _():
        x = x_ref[...]
        xn_ref[...] = _rms_rows(x, g_ref[...]).astype(BF16)
        o_ref[...] = x

    h = jnp.maximum(jnp.dot(xn_ref[...], wu_ref[...], preferred_element_type=F32), 0.0)
    o_ref[...] += jnp.dot((h * h).astype(BF16), wd_ref[...], preferred_element_type=F32)


def _ffn(x, g, wu, wd, tm, tf):
    m, d = x.shape
    f = wu.shape[1]
    return pl.pallas_call(
        _ffn_kernel,
        out_shape=jax.ShapeDtypeStruct((m, d), F32),
        grid=(m // tm, f // tf),
        in_specs=[
            pl.BlockSpec((tm, d), lambda i, j: (i, 0)),
            pl.BlockSpec((1, d), lambda i, j: (0, 0)),
            pl.BlockSpec((d, tf), lambda i, j: (0, j)),
            pl.BlockSpec((tf, d), lambda i, j: (j, 0)),
        ],
        out_specs=pl.BlockSpec((tm, d), lambda i, j: (i, 0)),
        scratch_shapes=[pltpu.VMEM((tm, d), BF16)],
        compiler_params=_cparams(("parallel", "arbitrary"), 48),
        name="ffn",
    )(x, g.reshape(1, d), wu, wd)


def _outproj_kernel(x_ref, mix_ref, c_ref, w1_ref, w2_ref, o_ref):
    acc = jnp.dot(mix_ref[...].astype(BF16), w1_ref[...], preferred_element_type=F32)
    acc += jnp.dot(c_ref[...].astype(BF16), w2_ref[...], preferred_element_type=F32)
    o_ref[...] = x_ref[...] + acc


def _outproj(x, mix, c, w1, w2, tm):
    m, d = x.shape
    return pl.pallas_call(
        _outproj_kernel,
        out_shape=jax.ShapeDtypeStruct((m, d), F32),
        grid=(m // tm,),
        in_specs=[
            pl.BlockSpec((tm, d), lambda i: (i, 0)),
            pl.BlockSpec((tm, MIX), lambda i: (i, 0)),
            pl.BlockSpec((tm, MEM_W), lambda i: (i, 0)),
            pl.BlockSpec((MIX, d), lambda i: (0, 0)),
            pl.BlockSpec((MEM_W, d), lambda i: (0, 0)),
        ],
        out_specs=pl.BlockSpec((tm, d), lambda i: (i, 0)),
        compiler_params=_cparams(("parallel",), 48),
        name="outproj",
    )(x, mix, c, w1, w2)


def _mem_attend_kernel(q_ref, k_ref, v_ref, o_ref):
    q = q_ref[0]
    for h in range(MEM_HEADS):
        sl = slice(h * MEM_HD, (h + 1) * MEM_HD)
        s = lax.dot_general(q[:, sl].astype(BF16), k_ref[0, :, sl].astype(BF16), _NT,
                            preferred_element_type=F32) * (MEM_HD ** -0.5)
        e = jnp.exp(s - jnp.max(s, axis=-1, keepdims=True))
        p = e / jnp.sum(e, axis=-1, keepdims=True)
        o_ref[0, :, sl] = jnp.dot(p.astype(BF16), v_ref[0, :, sl].astype(BF16),
                                  preferred_element_type=F32)


def _mem_attend(q, mk, mv, tt):
    b, t, _ = q.shape
    nm = mk.shape[1]
    return pl.pallas_call(
        _mem_attend_kernel,
        out_shape=jax.ShapeDtypeStruct((b, t, MEM_W), F32),
        grid=(b, t // tt),
        in_specs=[
            pl.BlockSpec((1, tt, MEM_W), lambda i, j: (i, j, 0)),
            pl.BlockSpec((1, nm, MEM_W), lambda i, j: (i, 0, 0)),
            pl.BlockSpec((1, nm, MEM_W), lambda i, j: (i, 0, 0)),
        ],
        out_specs=pl.BlockSpec((1, tt, MEM_W), lambda i, j: (i, j, 0)),
        compiler_params=_cparams(("parallel", "parallel"), 32),
        name="mem_attend",
    )(q, mk, mv)


def _rwkv_prep_kernel(seq_len, tm,
                      pr_ref, pk_ref, pv_ref, pl_ref,
                      br_ref, bk_ref, bv_ref, bl_ref,
                      sr_ref, sk_ref, sv_ref, sl_ref,
                      mur_ref, muk_ref, muv_ref, mul_ref,
                      w0_ref, a0_ref, kk_ref, ka_ref, rk_ref,
                      w2_ref, a2_ref, g2_ref, ones_ref,
                      r_out, w_out, k_out, v_out, a_out, b_out, g_out, bonus_out):
    i = pl.program_id(0)
    row = lax.broadcasted_iota(jnp.int32, (tm, 1), 0)

    def shifted(cur_ref, before_ref, state_ref, mu_ref):
        cur = cur_ref[...]
        rolled = pltpu.roll(cur, 1, 0)
        if seq_len >= tm:
            first = jnp.where(i == 0, state_ref[0:1, :], before_ref[7:8, :])
            prev = jnp.where(row == 0, first, rolled)
        else:
            prev = jnp.where(row % seq_len == 0, state_ref[...], rolled)
        return cur + (prev - cur) * mu_ref[...]

    r = shifted(pr_ref, br_ref, sr_ref, mur_ref)
    k = shifted(pk_ref, bk_ref, sk_ref, muk_ref)
    v = shifted(pv_ref, bv_ref, sv_ref, muv_ref)
    lo = shifted(pl_ref, bl_ref, sl_ref, mul_ref)

    xw = jnp.tanh(lo[:, 0:128]).astype(BF16)
    xa = lo[:, 128:256].astype(BF16)
    xg = jax.nn.sigmoid(lo[:, 256:512]).astype(BF16)
    wl = w0_ref[...] + jnp.dot(xw, w2_ref[...], preferred_element_type=F32)
    w_log = -jax.nn.softplus(-wl) - 0.5
    log_decay = -jnp.exp(w_log)
    a = jax.nn.sigmoid(a0_ref[...] + jnp.dot(xa, a2_ref[...], preferred_element_type=F32))
    g = jnp.dot(xg, g2_ref[...], preferred_element_type=F32)

    ones = ones_ref[...]
    kk = k * kk_ref[...]
    nrm = jnp.sqrt(_dot_exact_rhs(kk * kk, ones))
    kk = kk / jnp.maximum(nrm, 1e-12)
    k2 = k * (1.0 + (a - 1.0) * ka_ref[...])
    bonus = _dot_exact_rhs(r * k2 * rk_ref[...], ones) * v

    r_out[...] = r
    w_out[...] = log_decay
    k_out[...] = k2
    v_out[...] = v
    a_out[...] = -kk
    b_out[...] = kk * a
    g_out[...] = g
    bonus_out[...] = bonus


def _rwkv_prep(z, shift_rows, seq_len, tm, mu, w0, a0, k_k, k_a, r_k, w2, a2, g2, ones):
    m = z.shape[0]
    nc = MIX // LANES
    lb = 3 * MIX // LORA_PAD
    tb = tm // 8
    srows = shift_rows.shape[0] if seq_len >= tm else tm

    def cur(off):
        return pl.BlockSpec((tm, LANES), lambda i, c: (i, off + c))

    def before(off):
        return pl.BlockSpec((8, LANES), lambda i, c: (jnp.maximum(i * tb - 1, 0), off + c))

    def state(off):
        if seq_len >= tm:
            return pl.BlockSpec((srows, LANES), lambda i, c: (0, off + c))
        return pl.BlockSpec((tm, LANES), lambda i, c: (i, off + c))

    def vec(off):
        return pl.BlockSpec((1, LANES), lambda i, c: (0, off + c))

    cur_l = pl.BlockSpec((tm, LORA_PAD), lambda i, c: (i, lb))
    before_l = pl.BlockSpec((8, LORA_PAD), lambda i, c: (jnp.maximum(i * tb - 1, 0), lb))
    if seq_len >= tm:
        state_l = pl.BlockSpec((srows, LORA_PAD), lambda i, c: (0, lb))
    else:
        state_l = pl.BlockSpec((tm, LORA_PAD), lambda i, c: (i, lb))
    vec_l = pl.BlockSpec((1, LORA_PAD), lambda i, c: (0, lb))

    out_spec = pl.BlockSpec((tm, LANES), lambda i, c: (i, c))
    out_sds = jax.ShapeDtypeStruct((m, MIX), F32)
    return pl.pallas_call(
        functools.partial(_rwkv_prep_kernel, seq_len, tm),
        out_shape=(out_sds,) * 8,
        grid=(m // tm, nc),
        in_specs=[
            cur(0), cur(nc), cur(2 * nc), cur_l,
            before(0), before(nc), before(2 * nc), before_l,
            state(0), state(nc), state(2 * nc), state_l,
            vec(0), vec(nc), vec(2 * nc), vec_l,
            vec(0), vec(0), vec(0), vec(0), vec(0),
            pl.BlockSpec((LANES, LANES), lambda i, c: (0, c)),
            pl.BlockSpec((LANES, LANES), lambda i, c: (0, c)),
            pl.BlockSpec((2 * LANES, LANES), lambda i, c: (0, c)),
            pl.BlockSpec((LANES, LANES), lambda i, c: (0, 0)),
        ],
        out_specs=(out_spec,) * 8,
        compiler_params=_cparams(("parallel", "parallel"), 48),
        name="rwkv_prep",
    )(z, z, z, z, z, z, z, z, shift_rows, shift_rows, shift_rows, shift_rows,
      mu, mu, mu, mu, w0, a0, k_k, k_a, r_k, w2, a2, g2, ones)


def _rwkv_scan_kernel(tb, r_ref, w_ref, k_ref, v_ref, a_ref, b_ref, s0_ref, y_ref, sf_ref, s_ref):
    tblk = pl.program_id(1)

    @pl.when(tblk == 0)
    def _():
        s_ref[...] = s0_ref[0]

    lane = lax.broadcasted_iota(jnp.int32, (HEAD, LANES), 1)
    sub = lax.broadcasted_iota(jnp.int32, (HEAD, LANES), 0)
    lo = lane < HEAD
    diag = (lane % HEAD) == sub

    def head_sums(x):
        sa = jnp.sum(jnp.where(lo, x, 0.0), axis=1, keepdims=True)
        sb = jnp.sum(jnp.where(lo, 0.0, x), axis=1, keepdims=True)
        return jnp.where(lo, sa, sb)

    def step(t, carry):
        row = lambda ref: ref[0, pl.ds(t, 1), :]
        r_t, w_t, k_t, v_t, a_t, b_t = (row(x) for x in (r_ref, w_ref, k_ref, v_ref, a_ref, b_ref))
        w_t = jnp.exp(w_t)
        ys = []
        for p in range(N_PAIRS):
            sl = slice(p * LANES, (p + 1) * LANES)
            s = s_ref[p]
            sa = head_sums(s * a_t[:, sl])
            vc = head_sums(jnp.where(diag, v_t[:, sl], 0.0))
            s = s * w_t[:, sl] + sa * b_t[:, sl] + vc * k_t[:, sl]
            s_ref[p] = s
            yc = head_sums(s * r_t[:, sl])
            ys.append(jnp.sum(jnp.where(diag, yc, 0.0), axis=0, keepdims=True))
        y_ref[0, pl.ds(t, 1), :] = jnp.concatenate(ys, axis=1)
        return carry

    lax.fori_loop(0, tb, step, 0)

    @pl.when(tblk == pl.num_programs(1) - 1)
    def _():
        sf_ref[0] = s_ref[...]


def _rwkv_scan(r, w, k, v, a, b, s0, tb):
    bsz, t, _ = r.shape
    s0p = s0.reshape(bsz, N_PAIRS, 2, HEAD, HEAD).transpose(0, 1, 3, 2, 4).reshape(bsz, N_PAIRS, HEAD, LANES)
    seq = pl.BlockSpec((1, tb, MIX), lambda i, j: (i, j, 0))
    st = pl.BlockSpec((1, N_PAIRS, HEAD, LANES), lambda i, j: (i, 0, 0, 0))
    y, sf = pl.pallas_call(
        functools.partial(_rwkv_scan_kernel, tb),
        out_shape=(jax.ShapeDtypeStruct((bsz, t, MIX), F32),
                   jax.ShapeDtypeStruct((bsz, N_PAIRS, HEAD, LANES), F32)),
        grid=(bsz, t // tb),
        in_specs=[seq] * 6 + [st],
        out_specs=(seq, st),
        scratch_shapes=[pltpu.VMEM((N_PAIRS, HEAD, LANES), F32)],
        compiler_params=_cparams(("parallel", "arbitrary"), 48),
        name="rwkv_scan",
    )(r, w, k, v, a, b, s0p)
    sf = sf.reshape(bsz, N_PAIRS, HEAD, 2, HEAD).transpose(0, 1, 3, 2, 4).reshape(bsz, N_HEADS, HEAD, HEAD)
    return y, sf


CHUNK = 64
PAIR_GROUP = 12


def _rwkv_chunk_kernel(r_ref, w_ref, k_ref, v_ref, a_ref, b_ref, s0_ref, y_ref, sf_ref, s_ref):
    L = CHUNK
    assert 2 * L == LANES

    @pl.when(pl.program_id(0) == 0)
    def _():
        s_ref[...] = s0_ref[...]

    ri = lax.broadcasted_iota(jnp.int32, (L, L), 0)
    ci = lax.broadcasted_iota(jnp.int32, (L, L), 1)
    tri = (ci <= ri).astype(BF16)
    i2 = lax.broadcasted_iota(jnp.int32, (LANES, LANES), 0)
    j2 = lax.broadcasted_iota(jnp.int32, (LANES, LANES), 1)
    top, left = i2 < L, j2 < L
    s_idx, t_idx = j2 & (L - 1), i2 & (L - 1)
    is_top = jnp.where(top, 1, 0)
    keep_a = s_idx < t_idx + (1 - is_top)
    keep_b = s_idx < t_idx + is_top
    ul = top & left
    lr = jnp.logical_not(top | left)
    same_head = ul | lr
    eye = (i2 == j2).astype(F32)
    lo = lax.broadcasted_iota(jnp.int32, (1, LANES), 1) < HEAD
    zeros = jnp.zeros((L, LANES), F32)
    cat = lambda x, y: jnp.concatenate([x, y], axis=0)
    dot = lambda x, y: jnp.dot(x.astype(BF16), y.astype(BF16), preferred_element_type=F32)
    dot_nt = lambda x, y: lax.dot_general(x.astype(BF16), y.astype(BF16), _NT, preferred_element_type=F32)

    def setup(p):
        sl = slice(p * LANES, (p + 1) * LANES)
        r, lw, k, v, a, b = (x[:, sl] for x in (r_ref, w_ref, k_ref, v_ref, a_ref, b_ref))
        hi, mid, low = _split3(lw)
        c = dot(tri, hi) + dot(tri, mid) + dot(tri, low)
        c_last = c[L - 1:L, :]
        at, rt = a * jnp.exp(c - lw), r * jnp.exp(c)
        em = jnp.exp(-c)
        kh, bh = k * em, b * em
        erest = jnp.exp(c_last - c)

        s = s_ref[p]
        s_hi = s.astype(BF16)
        s_lo = s - s_hi.astype(F32)
        ar = cat(at, rt)
        ps = dot_nt(ar, s_hi) + dot_nt(ar, s_lo)

        m_a = jnp.where(keep_a, dot_nt(jnp.where(lo, ar, 0.0), cat(bh, kh)), 0.0)
        m_b = jnp.where(keep_b, dot_nt(jnp.where(lo, 0.0, cat(rt, at)), cat(kh, bh)), 0.0)
        vm_a, vm_b = jnp.where(lo, v, 0.0), jnp.where(lo, 0.0, v)
        rhs = ps[:L] + dot(m_a[:L], cat(zeros, vm_a)) + dot(m_b[L:], cat(vm_b, zeros))
        n = jnp.where(ul, m_a, 0.0) + jnp.where(lr, m_b, 0.0)
        return dict(sl=sl, p=p, v=v, s=s, ps=ps, m_a=m_a, m_b=m_b, vm_a=vm_a, vm_b=vm_b, rhs=rhs, n=n,
                    kb=cat(k * erest, b * erest), g_last=jnp.exp(c_last))

    def finish(d, inv):
        rhs = d["rhs"]
        u2 = dot(inv, cat(jnp.where(lo, rhs, 0.0), jnp.where(lo, 0.0, rhs)))
        y_ref[:, d["sl"]] = (d["ps"][L:] + dot(d["m_a"][L:], cat(u2[:L], d["vm_a"]))
                             + dot(d["m_b"][:L], cat(d["vm_b"], u2[L:])))
        vu = cat(d["v"], u2[:L] + u2[L:])
        upd = dot(vu.T, d["kb"])
        s_ref[d["p"]] = d["s"] * d["g_last"] + jnp.where(same_head, upd, 0.0)

    for g0 in range(0, N_PAIRS, PAIR_GROUP):
        ds = [setup(p) for p in range(g0, g0 + PAIR_GROUP)]
        qs = [d["n"] for d in ds]
        invs = [eye + d["n"] for d in ds]
        for _ in range(5):
            qs = [dot(q, q) for q in qs]
            invs = [inv + dot(inv, q) for inv, q in zip(invs, qs)]
        for d, inv in zip(ds, invs):
            finish(d, inv)

    @pl.when(pl.program_id(0) == pl.num_programs(0) - 1)
    def _():
        sf_ref[...] = s_ref[...]


def _rwkv_chunked(r, w, k, v, a, b, s0):
    t = r.shape[0]
    pair_eye = jnp.eye(2, dtype=F32)[None, :, None, :, None]
    s0p = (s0.reshape(N_PAIRS, 2, HEAD, 1, HEAD) * pair_eye).reshape(N_PAIRS, LANES, LANES)
    seq = pl.BlockSpec((CHUNK, MIX), lambda i: (i, 0))
    st = pl.BlockSpec((N_PAIRS, LANES, LANES), lambda i: (0, 0, 0))
    y, sf = pl.pallas_call(
        _rwkv_chunk_kernel,
        out_shape=(jax.ShapeDtypeStruct((t, MIX), F32), jax.ShapeDtypeStruct((N_PAIRS, LANES, LANES), F32)),
        grid=(t // CHUNK,),
        in_specs=[seq] * 6 + [st],
        out_specs=(seq, st),
        scratch_shapes=[pltpu.VMEM((N_PAIRS, LANES, LANES), F32)],
        compiler_params=_cparams(("arbitrary",), 48),
        name="rwkv_chunked",
    )(r, w, k, v, a, b, s0p)
    sf = sf.reshape(N_PAIRS, 2, HEAD, 2, HEAD)
    sf = jnp.stack([sf[:, 0, :, 0], sf[:, 1, :, 1]], axis=1).reshape(N_HEADS, HEAD, HEAD)
    return y, sf


def _rwkv_post_kernel(y_ref, bonus_ref, g_ref, lw_ref, lb_ref, ones_ref, o_ref):
    y = y_ref[...]
    ones = ones_ref[...]
    m = _dot_exact_rhs(y, ones) * (1.0 / HEAD)
    d = y - m
    var = _dot_exact_rhs(d * d, ones) * (1.0 / HEAD)
    yn = d * lax.rsqrt(var + GN_EPS) * lw_ref[...] + lb_ref[...]
    o_ref[...] = (yn + bonus_ref[...]) * g_ref[...]


def _rwkv_post(y, bonus, g, lnx_w, lnx_b, ones, tm):
    m = y.shape[0]
    blk = pl.BlockSpec((tm, LANES), lambda i, c: (i, c))
    vec = pl.BlockSpec((1, LANES), lambda i, c: (0, c))
    return pl.pallas_call(
        _rwkv_post_kernel,
        out_shape=jax.ShapeDtypeStruct((m, MIX), F32),
        grid=(m // tm, MIX // LANES),
        in_specs=[blk, blk, blk, vec, vec, pl.BlockSpec((LANES, LANES), lambda i, c: (0, 0))],
        out_specs=blk,
        compiler_params=_cparams(("parallel", "parallel"), 32),
        name="rwkv_post",
    )(y, bonus, g, lnx_w, lnx_b, ones)


def _sb_inside(z, lower, allowed, two_pass):
    neg_abs = pltpu.bitcast(pltpu.bitcast(z, jnp.uint32) | jnp.uint32(0x80000000), F32)
    sp = jnp.maximum(z, 0.0) + jnp.log(1.0 + jnp.exp(neg_abs))
    ls = z - sp
    if allowed is not None:
        sp = jnp.where(allowed, sp, 0.0)
    lhs = sp.astype(BF16)
    if two_pass:
        lhs = jnp.concatenate([lhs, (sp - lhs.astype(F32)).astype(BF16)], axis=1)
    right = jnp.dot(lhs, lower, preferred_element_type=F32)
    return ls, right, right[:, 0:1] + sp[:, 0:1]


def _sb_weights(scores, carry, allowed):
    ls, right, total = scores
    a = jnp.exp(ls - right - carry)
    if allowed is not None:
        a = jnp.where(allowed, a, 0.0)
    return a, carry + total


def _sb_tile(z, carry, lower, allowed, two_pass):
    return _sb_weights(_sb_inside(z, lower, allowed, two_pass), carry, allowed)


def _sb_prompt_kernel(tq, q_ref, k_ref, v_ref, up_ref, o_ref, kb_ref, va_ref, vb_ref, acc_ref,
                      z_ref, p_ref, c_ref):
    qi = pl.program_id(1)
    lane = lax.broadcasted_iota(jnp.int32, (1, LANES), 1)
    lo = lane < HEAD

    @pl.when(qi == 0)
    def _():
        kb_ref[...] = k_ref[...].astype(BF16)
        v = v_ref[...]
        va_ref[...] = jnp.where(lo, v, 0.0).astype(BF16)
        vb_ref[...] = jnp.where(lo, 0.0, v).astype(BF16)

    q = q_ref[...] * (HEAD ** -0.5)
    qa = jnp.where(lo, q, 0.0).astype(BF16)
    qb = jnp.where(lo, 0.0, q).astype(BF16)
    lower = up_ref[...]

    def logits_to(j, slot):
        kblk = kb_ref[pl.ds(pl.multiple_of(jnp.maximum(j, 0) * tq, tq), tq), :]
        z_ref[slot, 0] = lax.dot_general(qa, kblk, _NT, preferred_element_type=F32)
        z_ref[slot, 1] = lax.dot_general(qb, kblk, _NT, preferred_element_type=F32)

    def accumulate(j, slot):
        ks = pl.multiple_of(j * tq, tq)
        acc_ref[...] += (jnp.dot(p_ref[slot, 0], va_ref[pl.ds(ks, tq), :], preferred_element_type=F32)
                         + jnp.dot(p_ref[slot, 1], vb_ref[pl.ds(ks, tq), :], preferred_element_type=F32))

    def weights_to(za, zb, slot, allowed):
        pa, ca = _sb_tile(za, c_ref[0], lower, allowed, False)
        pb, cb = _sb_tile(zb, c_ref[1], lower, allowed, False)
        p_ref[slot, 0] = pa.astype(BF16)
        p_ref[slot, 1] = pb.astype(BF16)
        c_ref[0] = ca
        c_ref[1] = cb

    def step(j, src, dst):
        accumulate(j + 1, src)
        za, zb = z_ref[src, 0], z_ref[src, 1]
        logits_to(j - 1, dst)
        weights_to(za, zb, dst, None)

    acc_ref[...] = jnp.zeros_like(acc_ref)
    c_ref[...] = jnp.zeros_like(c_ref)
    below_diag = (lax.broadcasted_iota(jnp.int32, (tq, tq), 1) < lax.broadcasted_iota(jnp.int32, (tq, tq), 0))
    logits_to(qi, 1)
    za, zb = z_ref[1, 0], z_ref[1, 1]
    logits_to(qi - 1, 0)
    weights_to(za, zb, 0, below_diag)

    odd = qi & 1

    @pl.when(odd == 1)
    def _():
        step(qi - 1, 0, 0)

    def pair(n, carry):
        j = qi - 1 - odd - 2 * n
        step(j, 0, 1)
        step(j - 1, 1, 0)
        return carry

    lax.fori_loop(0, qi >> 1, pair, 0)
    accumulate(0, 0)
    o_ref[...] = acc_ref[...]


def _sb_prompt(z, upper, tq):
    s = z.shape[0]
    nc = MIX // LANES
    return pl.pallas_call(
        functools.partial(_sb_prompt_kernel, tq),
        out_shape=jax.ShapeDtypeStruct((s, MIX), F32),
        grid=(nc, s // tq),
        in_specs=[
            pl.BlockSpec((tq, LANES), lambda c, i: (i, c)),
            pl.BlockSpec((s, LANES), lambda c, i: (0, nc + c)),
            pl.BlockSpec((s, LANES), lambda c, i: (0, 2 * nc + c)),
            pl.BlockSpec((tq, tq), lambda c, i: (0, 0)),
        ],
        out_specs=pl.BlockSpec((tq, LANES), lambda c, i: (i, c)),
        scratch_shapes=[pltpu.VMEM((s, LANES), BF16)] * 3
                       + [pltpu.VMEM((tq, LANES), F32),
                          pltpu.VMEM((2, 2, tq, tq), F32),
                          pltpu.VMEM((2, 2, tq, tq), BF16),
                          pltpu.VMEM((2, tq, 1), F32)],
        compiler_params=_cparams(("parallel", "arbitrary"), 48),
        name="sb_prompt",
    )(z, z, z, upper)


QROWS = 8
SB_PAGE_GROUP = 4


def _sb_sample_kernel(group, pt_ref, q_ref, kn_ref, vn_ref, *refs):
    kc_refs, vc_refs = refs[:group], refs[group:2 * group]
    up_ref, o_ref, acc_ref, carry_ref = refs[2 * group:]
    p = pl.program_id(1)
    lower2 = up_ref[...]
    rows = N_HEADS * QROWS

    def tiles(k_refs, v_refs, carry, allowed):
        zs = [jnp.concatenate([jnp.dot(q_ref[0, h], k_ref[0, h].astype(BF16), preferred_element_type=F32)
                               for h in range(N_HEADS)], axis=0) for k_ref in k_refs]
        scores = [_sb_inside(z, lower2, allowed, True) for z in zs]
        total = None
        for score, v_ref in zip(scores, v_refs):
            a, carry = _sb_weights(score, carry, allowed)
            outs = [lax.dot_general(a[h * QROWS:(h + 1) * QROWS].astype(BF16), v_ref[0, h].astype(BF16),
                                    _NT, preferred_element_type=F32) for h in range(N_HEADS)]
            total = outs if total is None else [t + o for t, o in zip(total, outs)]
        return total, carry

    @pl.when(p == 0)
    def _():
        key = lax.broadcasted_iota(jnp.int32, (rows, PAGE), 1)
        qry = lax.broadcasted_iota(jnp.int32, (rows, PAGE), 0) & (QROWS - 1)
        outs, carry = tiles([kn_ref], [vn_ref], jnp.zeros((rows, 1), F32), key < qry)
        carry_ref[...] = carry
        for h in range(N_HEADS):
            acc_ref[h] = outs[h]

    @pl.when(p > 0)
    def _():
        outs, carry = tiles(kc_refs, vc_refs, carry_ref[...], None)
        carry_ref[...] = carry
        for h in range(N_HEADS):
            acc_ref[h] += outs[h]

    @pl.when(p == pl.num_programs(1) - 1)
    def _():
        o_ref[0] = acc_ref[...]


def _sb_sample(q, k_new, v_new, cache_k, cache_v, page_table, lower2):
    bsz = q.shape[0]
    n_pages = page_table.shape[1]
    group = next(g for g in (SB_PAGE_GROUP, 2, 1) if n_pages % g == 0)

    def page(g):
        return lambda b, p, pt: (pt[b, n_pages - 1 - (jnp.maximum(p, 1) - 1) * group - g], 0, 0, 0)

    own = lambda b, p, pt: (b, 0, 0, 0)
    kv_block = (1, N_HEADS, HEAD, PAGE)
    grid_spec = pltpu.PrefetchScalarGridSpec(
        num_scalar_prefetch=1,
        grid=(bsz, n_pages // group + 1),
        in_specs=[pl.BlockSpec((1, N_HEADS, QROWS, HEAD), own), pl.BlockSpec(kv_block, own),
                  pl.BlockSpec(kv_block, own)]
                 + [pl.BlockSpec(kv_block, page(g)) for g in range(group)] * 2
                 + [pl.BlockSpec((2 * PAGE, PAGE), lambda b, p, pt: (0, 0))],
        out_specs=pl.BlockSpec((1, N_HEADS, QROWS, HEAD), own),
        scratch_shapes=[pltpu.VMEM((N_HEADS, QROWS, HEAD), F32), pltpu.VMEM((N_HEADS * QROWS, 1), F32)],
    )
    return pl.pallas_call(
        functools.partial(_sb_sample_kernel, group),
        out_shape=jax.ShapeDtypeStruct((bsz, N_HEADS, QROWS, HEAD), F32),
        grid_spec=grid_spec,
        compiler_params=_cparams(("parallel", "arbitrary"), 48),
        name="sb_sample",
    )(page_table, q, k_new, v_new, *([cache_k] * group), *([cache_v] * group), lower2)


def _rmsnorm_kernel(x_ref, g_ref, o_ref):
    o_ref[...] = _rms_rows(x_ref[...], g_ref[...])


def _rmsnorm(x, g, tm):
    m, d = x.shape
    return pl.pallas_call(
        _rmsnorm_kernel,
        out_shape=jax.ShapeDtypeStruct((m, d), F32),
        grid=(m // tm,),
        in_specs=[pl.BlockSpec((tm, d), lambda i: (i, 0)), pl.BlockSpec((1, d), lambda i: (0, 0))],
        out_specs=pl.BlockSpec((tm, d), lambda i: (i, 0)),
        compiler_params=_cparams(("parallel",), 32),
        name="rmsnorm",
    )(x, g.reshape(1, d))


def _pad_cols(x, width):
    return jnp.pad(x, ((0, 0), (0, width - x.shape[1])))


def _rwkv_cols(x):
    o = 3 * MIX
    parts = [x[:, :o],
             _pad_cols(x[:, o:o + DECAY_LORA], LANES),
             _pad_cols(x[:, o + DECAY_LORA:o + DECAY_LORA + ICLR_LORA], LANES),
             _pad_cols(x[:, o + DECAY_LORA + ICLR_LORA:RWKV_PROJ], 2 * LANES)]
    if x.shape[1] > RWKV_PROJ:
        parts.append(x[:, RWKV_PROJ:])
    return jnp.concatenate(parts, axis=1)


def _rwkv_uncols(x):
    o = 3 * MIX
    return jnp.concatenate([x[:, :o], x[:, o:o + DECAY_LORA], x[:, o + LANES:o + LANES + ICLR_LORA],
                            x[:, o + 2 * LANES:o + 2 * LANES + GATE_LORA]], axis=1)


def _pad_rows(x, rows):
    return jnp.pad(x, ((0, rows - x.shape[0]), (0, 0)))


def _tile(m, pref):
    return pref if m % pref == 0 else m


def kernel(x_prompt, x_sample, mem_prompt, state_rwkv_wkv, state_rwkv_shift, cache_sb_k, cache_sb_v, page_table, cache_mem_k, cache_mem_v, norm_mix, norm_ffn, norm_final, mem_norm, w_mem_kv, w_out, w_ffn_up, w_ffn_down, rwkv_w_in, rwkv_mu, rwkv_w0, rwkv_w2, rwkv_a0, rwkv_a2, rwkv_g2, rwkv_k_k, rwkv_k_a, rwkv_r_k, rwkv_lnx_w, rwkv_lnx_b, sb_w_in, sb_b_qk):
    bp, s, d = x_prompt.shape
    db, t_new, _ = x_sample.shape
    assert bp == 1, "the prompt path handles a single sequence"
    depth = norm_mix.shape[0]
    ms = db * t_new
    n_mem = mem_prompt.shape[1]
    xp = x_prompt.reshape(s, d)
    xs = x_sample.reshape(ms, d)
    tmp = _tile(s, 512)
    tms = ms
    tmp_in = _tile(s, 1024)
    col_tile = lambda n: 1024 if n % 1024 == 0 else 512

    idx = lax.broadcasted_iota(jnp.int32, (LANES, LANES), 0)
    jdx = lax.broadcasted_iota(jnp.int32, (LANES, LANES), 1)
    head_ones = ((idx // HEAD) == (jdx // HEAD)).astype(BF16)
    tq = _tile(s, 256)
    iq = lax.broadcasted_iota(jnp.int32, (tq, tq), 0)
    jq = lax.broadcasted_iota(jnp.int32, (tq, tq), 1)
    lower_q = (iq > jq).astype(BF16)
    lower_p = jnp.tile((idx > jdx).astype(BF16), (2, 1))

    mem_k, mem_v = [], []
    wkv_p, shift_p, wkv_s, shift_s = [], [], [], []
    sbk_p, sbv_p, sbk_s, sbv_s = [], [], [], []
    for i in range(depth):
        j = i // 2
        kv = _rms_matmul(mem_prompt.reshape(n_mem, d), mem_norm[i], w_mem_kv[i].astype(BF16),
                         jnp.zeros((2 * MEM_W,), F32), n_mem, 2 * MEM_W)
        mk, mv = kv[:, :MEM_W], kv[:, MEM_W:]
        mem_k.append(mk.reshape(1, n_mem, MEM_HEADS, MEM_HD))
        mem_v.append(mv.reshape(1, n_mem, MEM_HEADS, MEM_HD))
        cmk = cache_mem_k[i].reshape(db, n_mem, MEM_W)
        cmv = cache_mem_v[i].reshape(db, n_mem, MEM_W)
        if i % 2 == 0:
            w_in = _rwkv_cols(rwkv_w_in[j]).astype(BF16)
            n_in = w_in.shape[1]
            zero_b = jnp.zeros((n_in,), F32)
            zp = _rms_matmul(xp, norm_mix[i], w_in, zero_b, tmp_in, col_tile(n_in))
            zs = _rms_matmul(xs, norm_mix[i], w_in, zero_b, tms, col_tile(n_in))
            shift_p.append(_rwkv_uncols(zp[s - 1:s]))
            shift_s.append(_rwkv_uncols(zs.reshape(db, t_new, n_in)[:, t_new - 1]))
            wide = 3 * MIX + LORA_PAD
            mu = _rwkv_cols(rwkv_mu[j][None])
            vecs = [x.reshape(1, MIX) for x in (rwkv_w0[j], rwkv_a0[j], rwkv_k_k[j], rwkv_k_a[j], rwkv_r_k[j])]
            lora = (_pad_rows(rwkv_w2[j], LANES).astype(BF16), _pad_rows(rwkv_a2[j], LANES).astype(BF16),
                    _pad_rows(rwkv_g2[j], 2 * LANES).astype(BF16))
            lnx = (rwkv_lnx_w[j].reshape(1, MIX), rwkv_lnx_b[j].reshape(1, MIX))

            def rwkv(z, shift_rows, seq_len, tm, s0, nb, tb, tpost):
                r, w, k, v, a, b, g, bonus = _rwkv_prep(z, shift_rows, seq_len, tm, mu, *vecs, *lora, head_ones)
                if nb == 1 and seq_len % CHUNK == 0:
                    y, s_fin = _rwkv_chunked(r, w, k, v, a, b, s0[0])
                    s_fin = s_fin[None]
                else:
                    sh = lambda x: x.reshape(nb, seq_len, MIX)
                    y, s_fin = _rwkv_scan(sh(r), sh(w), sh(k), sh(v), sh(a), sh(b), s0, tb)
                return _rwkv_post(y.reshape(nb * seq_len, MIX), bonus, g, *lnx, head_ones, tpost), s_fin

            zero_shift = jnp.zeros((8, wide), F32)
            zero_state = jnp.zeros((1, N_HEADS, HEAD, HEAD), F32)
            mix_p, st_p = rwkv(zp, zero_shift, s, tmp, zero_state, 1, _tile(s, 256), tmp)
            shift_rows = jnp.repeat(_rwkv_cols(state_rwkv_shift[j]), t_new, axis=0)
            mix_s, st_s = rwkv(zs, shift_rows, t_new, tms, state_rwkv_wkv[j], db, t_new, tms)
            wkv_p.append(st_p)
            wkv_s.append(st_s)
            q_off = wide
        else:
            w_in = sb_w_in[j].astype(BF16)
            bias = jnp.concatenate([sb_b_qk[j], jnp.zeros((w_in.shape[1] - 2 * MIX,), F32)])
            zp = _rms_matmul(xp, norm_mix[i], w_in, bias, tmp_in, col_tile(w_in.shape[1]))
            zs = _rms_matmul(xs, norm_mix[i], w_in, bias, tms, col_tile(w_in.shape[1]))
            mix_p = _sb_prompt(zp, lower_q, tq)
            assert t_new < QROWS
            heads = lambda x: x.reshape(db, t_new, N_HEADS, HEAD)
            q_s, k_s, v_s = heads(zs[:, :MIX]), heads(zs[:, MIX:2 * MIX]), heads(zs[:, 2 * MIX:3 * MIX])
            q_hq = jnp.pad((q_s * (HEAD ** -0.5)).transpose(0, 2, 1, 3),
                           ((0, 0), (0, 0), (0, QROWS - t_new), (0, 0))).astype(BF16)
            keys_last = lambda x: x.transpose(0, 2, 3, 1)
            pad_new = lambda x: keys_last(jnp.pad(x, ((0, 0), (0, PAGE - t_new), (0, 0), (0, 0))))
            o_hq = _sb_sample(q_hq, pad_new(k_s), pad_new(v_s), keys_last(cache_sb_k[j]),
                              keys_last(cache_sb_v[j]), page_table, lower_p)
            mix_s = o_hq[:, :, :t_new].transpose(0, 2, 1, 3).reshape(ms, MIX)
            sbk_p.append(zp[:, MIX:2 * MIX].reshape(1, s // PAGE, PAGE, N_HEADS, HEAD))
            sbv_p.append(zp[:, 2 * MIX:3 * MIX].reshape(1, s // PAGE, PAGE, N_HEADS, HEAD))
            sbk_s.append(k_s)
            sbv_s.append(v_s)
            q_off = 3 * MIX
        cp = _mem_attend(zp[:, q_off:q_off + MEM_W][None], mk[None], mv[None], tmp)[0]
        cs = _mem_attend(zs[:, q_off:q_off + MEM_W].reshape(db, t_new, MEM_W), cmk, cmv, t_new).reshape(ms, MEM_W)
        wo = w_out[i].astype(BF16)
        xp = _outproj(xp, mix_p, cp, wo[:MIX], wo[MIX:], tmp)
        xs = _outproj(xs, mix_s, cs, wo[:MIX], wo[MIX:], tms)
        wu, wd = w_ffn_up[i].astype(BF16), w_ffn_down[i].astype(BF16)
        xp = _ffn(xp, norm_ffn[i], wu, wd, tmp, col_tile(wu.shape[1]))
        xs = _ffn(xs, norm_ffn[i], wu, wd, tms, col_tile(wu.shape[1]))
    y_prompt = _rmsnorm(xp, norm_final, tmp).reshape(1, s, d)
    y_sample = _rmsnorm(xs, norm_final, tms).reshape(db, t_new, d)
    return (y_prompt, y_sample, jnp.stack(mem_k), jnp.stack(mem_v),
            jnp.stack(wkv_p), jnp.stack(shift_p), jnp.stack(wkv_s), jnp.stack(shift_s),
            jnp.stack(sbk_p), jnp.stack(sbv_p), jnp.stack(sbk_s), jnp.stack(sbv_s))
```

```python
import functools

import jax
import jax.numpy as jnp
from jax import lax
from jax.experimental import pallas as pl
from jax.experimental.pallas import tpu as pltpu

F32 = jnp.float32
BF16 = jnp.bfloat16

HEAD = 64
N_HEADS = 24
MIX = N_HEADS * HEAD
N_PAIRS = N_HEADS // 2
LANES = 128
MEM_HEADS = 4
MEM_HD = 128
MEM_W = MEM_HEADS * MEM_HD
DECAY_LORA = 64
ICLR_LORA = 64
GATE_LORA = 224
LORA_PAD = 512
RWKV_PROJ = 3 * MIX + DECAY_LORA + ICLR_LORA + GATE_LORA
RMS_EPS = 1e-6
GN_EPS = 64e-5
PAGE = 128

_NT = (((1,), (1,)), ((), ()))


def _cparams(sem, vmem_mb):
    return pltpu.CompilerParams(dimension_semantics=sem, vmem_limit_bytes=vmem_mb << 20)


def _split3(x):
    hi = x.astype(BF16)
    r1 = x - hi.astype(F32)
    mid = r1.astype(BF16)
    lo = (r1 - mid.astype(F32)).astype(BF16)
    return hi, mid, lo


def _dot_exact_rhs(x, ones_bf16):
    hi, mid, lo = _split3(x)
    d = lambda a: jnp.dot(a, ones_bf16, preferred_element_type=F32)
    return d(hi) + d(mid) + d(lo)


def _rms_rows(x, g):
    ms = jnp.mean(x * x, axis=-1, keepdims=True)
    return x * lax.rsqrt(ms + RMS_EPS) * g


def _rms_matmul_kernel(x_ref, g_ref, w_ref, b_ref, o_ref, xn_ref):
    @pl.when(pl.program_id(1) == 0)
    def _():
        xn_ref[...] = _rms_rows(x_ref[...], g_ref[...]).astype(BF16)

    o_ref[...] = jnp.dot(xn_ref[...], w_ref[...], preferred_element_type=F32) + b_ref[...]


def _rms_matmul(x, g, w, b, tm, tn):
    m, d = x.shape
    n = w.shape[1]
    return pl.pallas_call(
        _rms_matmul_kernel,
        out_shape=jax.ShapeDtypeStruct((m, n), F32),
        grid=(m // tm, n // tn),
        in_specs=[
            pl.BlockSpec((tm, d), lambda i, j: (i, 0)),
            pl.BlockSpec((1, d), lambda i, j: (0, 0)),
            pl.BlockSpec((d, tn), lambda i, j: (0, j)),
            pl.BlockSpec((1, tn), lambda i, j: (0, j)),
        ],
        out_specs=pl.BlockSpec((tm, tn), lambda i, j: (i, j)),
        scratch_shapes=[pltpu.VMEM((tm, d), BF16)],
        compiler_params=_cparams(("parallel", "arbitrary"), 48),
        name="rms_matmul",
    )(x, g.reshape(1, d), w, b.reshape(1, n))


def _ffn_kernel(x_ref, g_ref, wu_ref, wd_ref, o_ref, xn_ref):
    @pl.when(pl.program_id(1) == 0)
    def _():
        x = x_ref[...]
        xn_ref[...] = _rms_rows(x, g_ref[...]).astype(BF16)
        o_ref[...] = x

    h = jnp.maximum(jnp.dot(xn_ref[...], wu_ref[...], preferred_element_type=F32), 0.0)
    o_ref[...] += jnp.dot((h * h).astype(BF16), wd_ref[...], preferred_element_type=F32)


def _ffn(x, g, wu, wd, layer, tm, tf):
    m, d = x.shape
    f = wu.shape[2]
    return pl.pallas_call(
        _ffn_kernel,
        out_shape=jax.ShapeDtypeStruct((m, d), F32),
        grid=(m // tm, f // tf),
        in_specs=[
            pl.BlockSpec((tm, d), lambda i, j: (i, 0)),
            pl.BlockSpec((1, d), lambda i, j: (0, 0)),
            pl.BlockSpec((None, d, tf), lambda i, j: (layer, 0, j)),
            pl.BlockSpec((None, tf, d), lambda i, j: (layer, j, 0)),
        ],
        out_specs=pl.BlockSpec((tm, d), lambda i, j: (i, 0)),
        scratch_shapes=[pltpu.VMEM((tm, d), BF16)],
        compiler_params=_cparams(("parallel", "arbitrary"), 48),
        name="ffn",
    )(x, g.reshape(1, d), wu, wd)


def _outproj_kernel(x_ref, mix_ref, c_ref, w1_ref, w2_ref, o_ref):
    acc = jnp.dot(mix_ref[...].astype(BF16), w1_ref[...], preferred_element_type=F32)
    acc += jnp.dot(c_ref[...].astype(BF16), w2_ref[...], preferred_element_type=F32)
    o_ref[...] = x_ref[...] + acc


def _outproj(x, mix, c, w1, w2, tm):
    m, d = x.shape
    return pl.pallas_call(
        _outproj_kernel,
        out_shape=jax.ShapeDtypeStruct((m, d), F32),
        grid=(m // tm,),
        in_specs=[
            pl.BlockSpec((tm, d), lambda i: (i, 0)),
            pl.BlockSpec((tm, MIX), lambda i: (i, 0)),
            pl.BlockSpec((tm, MEM_W), lambda i: (i, 0)),
            pl.BlockSpec((MIX, d), lambda i: (0, 0)),
            pl.BlockSpec((MEM_W, d), lambda i: (0, 0)),
        ],
        out_specs=pl.BlockSpec((tm, d), lambda i: (i, 0)),
        compiler_params=_cparams(("parallel",), 48),
        name="outproj",
    )(x, mix, c, w1, w2)


def _mem_attend_kernel(head_rows, q_ref, k_ref, v_ref, o_ref):
    q = q_ref[0]
    for h in range(MEM_HEADS):
        sl = slice(h * MEM_HD, (h + 1) * MEM_HD)
        if head_rows:
            n_mem = k_ref.shape[1] // MEM_HEADS
            k = k_ref[0, pl.ds(h, n_mem, stride=MEM_HEADS), :]
            v = v_ref[0, pl.ds(h, n_mem, stride=MEM_HEADS), :]
        else:
            k, v = k_ref[0, :, sl], v_ref[0, :, sl]
        s = lax.dot_general(q[:, sl].astype(BF16), k.astype(BF16), _NT,
                            preferred_element_type=F32) * (MEM_HD ** -0.5)
        e = jnp.exp(s - jnp.max(s, axis=-1, keepdims=True))
        p = e / jnp.sum(e, axis=-1, keepdims=True)
        o_ref[0, :, sl] = jnp.dot(p.astype(BF16), v.astype(BF16), preferred_element_type=F32)


def _mem_attend(q, mk, mv, tt, first=0):
    b, t, _ = q.shape
    head_rows = mk.shape[2] == MEM_HD
    kv_spec = pl.BlockSpec((1,) + mk.shape[1:], lambda i, j: (first + i, 0, 0))
    return pl.pallas_call(
        functools.partial(_mem_attend_kernel, head_rows),
        out_shape=jax.ShapeDtypeStruct((b, t, MEM_W), F32),
        grid=(b, t // tt),
        in_specs=[pl.BlockSpec((1, tt, MEM_W), lambda i, j: (i, j, 0)), kv_spec, kv_spec],
        out_specs=pl.BlockSpec((1, tt, MEM_W), lambda i, j: (i, j, 0)),
        compiler_params=_cparams(("parallel", "parallel"), 32),
        name="mem_attend",
    )(q, mk, mv)


def _rwkv_prep_kernel(seq_len, tm,
                      pr_ref, pk_ref, pv_ref, pl_ref,
                      br_ref, bk_ref, bv_ref, bl_ref,
                      sr_ref, sk_ref, sv_ref, sl_ref,
                      mur_ref, muk_ref, muv_ref, mul_ref,
                      w0_ref, a0_ref, kk_ref, ka_ref, rk_ref,
                      w2_ref, a2_ref, g2_ref, ones_ref,
                      r_out, w_out, k_out, v_out, a_out, b_out, g_out, bonus_out):
    i = pl.program_id(0)
    row = lax.broadcasted_iota(jnp.int32, (tm, 1), 0)

    def shifted(cur_ref, before_ref, state_ref, mu_ref):
        cur = cur_ref[...]
        rolled = pltpu.roll(cur, 1, 0)
        if seq_len >= tm:
            first = jnp.where(i == 0, state_ref[0:1, :], before_ref[7:8, :])
            prev = jnp.where(row == 0, first, rolled)
        else:
            prev = jnp.where(row % seq_len == 0, state_ref[...], rolled)
        return cur + (prev - cur) * mu_ref[...]

    r = shifted(pr_ref, br_ref, sr_ref, mur_ref)
    k = shifted(pk_ref, bk_ref, sk_ref, muk_ref)
    v = shifted(pv_ref, bv_ref, sv_ref, muv_ref)
    lo = shifted(pl_ref, bl_ref, sl_ref, mul_ref)

    xw = jnp.tanh(lo[:, 0:128]).astype(BF16)
    xa = lo[:, 128:256].astype(BF16)
    xg = jax.nn.sigmoid(lo[:, 256:512]).astype(BF16)
    wl = w0_ref[...] + jnp.dot(xw, w2_ref[...], preferred_element_type=F32)
    w_log = -jax.nn.softplus(-wl) - 0.5
    log_decay = -jnp.exp(w_log)
    a = jax.nn.sigmoid(a0_ref[...] + jnp.dot(xa, a2_ref[...], preferred_element_type=F32))
    g = jnp.dot(xg, g2_ref[...], preferred_element_type=F32)

    ones = ones_ref[...]
    kk = k * kk_ref[...]
    nrm = jnp.sqrt(_dot_exact_rhs(kk * kk, ones))
    kk = kk / jnp.maximum(nrm, 1e-12)
    k2 = k * (1.0 + (a - 1.0) * ka_ref[...])
    bonus = _dot_exact_rhs(r * k2 * rk_ref[...], ones) * v

    r_out[...] = r
    w_out[...] = log_decay
    k_out[...] = k2
    v_out[...] = v
    a_out[...] = -kk
    b_out[...] = kk * a
    g_out[...] = g
    bonus_out[...] = bonus


def _rwkv_prep(z, shift_rows, seq_len, tm, mu, w0, a0, k_k, k_a, r_k, w2, a2, g2, ones):
    m = z.shape[0]
    nc = MIX // LANES
    lb = 3 * MIX // LORA_PAD
    tb = tm // 8
    srows = shift_rows.shape[0] if seq_len >= tm else tm

    def cur(off):
        return pl.BlockSpec((tm, LANES), lambda i, c: (i, off + c))

    def before(off):
        return pl.BlockSpec((8, LANES), lambda i, c: (jnp.maximum(i * tb - 1, 0), off + c))

    def state(off):
        if seq_len >= tm:
            return pl.BlockSpec((srows, LANES), lambda i, c: (0, off + c))
        return pl.BlockSpec((tm, LANES), lambda i, c: (i, off + c))

    def vec(off):
        return pl.BlockSpec((1, LANES), lambda i, c: (0, off + c))

    cur_l = pl.BlockSpec((tm, LORA_PAD), lambda i, c: (i, lb))
    before_l = pl.BlockSpec((8, LORA_PAD), lambda i, c: (jnp.maximum(i * tb - 1, 0), lb))
    if seq_len >= tm:
        state_l = pl.BlockSpec((srows, LORA_PAD), lambda i, c: (0, lb))
    else:
        state_l = pl.BlockSpec((tm, LORA_PAD), lambda i, c: (i, lb))
    vec_l = pl.BlockSpec((1, LORA_PAD), lambda i, c: (0, lb))

    out_spec = pl.BlockSpec((tm, LANES), lambda i, c: (i, c))
    out_sds = jax.ShapeDtypeStruct((m, MIX), F32)
    return pl.pallas_call(
        functools.partial(_rwkv_prep_kernel, seq_len, tm),
        out_shape=(out_sds,) * 8,
        grid=(m // tm, nc),
        in_specs=[
            cur(0), cur(nc), cur(2 * nc), cur_l,
            before(0), before(nc), before(2 * nc), before_l,
            state(0), state(nc), state(2 * nc), state_l,
            vec(0), vec(nc), vec(2 * nc), vec_l,
            vec(0), vec(0), vec(0), vec(0), vec(0),
            pl.BlockSpec((LANES, LANES), lambda i, c: (0, c)),
            pl.BlockSpec((LANES, LANES), lambda i, c: (0, c)),
            pl.BlockSpec((2 * LANES, LANES), lambda i, c: (0, c)),
            pl.BlockSpec((LANES, LANES), lambda i, c: (0, 0)),
        ],
        out_specs=(out_spec,) * 8,
        compiler_params=_cparams(("parallel", "parallel"), 48),
        name="rwkv_prep",
    )(z, z, z, z, z, z, z, z, shift_rows, shift_rows, shift_rows, shift_rows,
      mu, mu, mu, mu, w0, a0, k_k, k_a, r_k, w2, a2, g2, ones)


def _rwkv_scan_kernel(tb, r_ref, w_ref, k_ref, v_ref, a_ref, b_ref, s0_ref, y_ref, sf_ref, s_ref):
    tblk = pl.program_id(1)

    @pl.when(tblk == 0)
    def _():
        s_ref[...] = s0_ref[0]

    lane = lax.broadcasted_iota(jnp.int32, (HEAD, LANES), 1)
    sub = lax.broadcasted_iota(jnp.int32, (HEAD, LANES), 0)
    lo = lane < HEAD
    diag = (lane % HEAD) == sub

    def head_sums(x):
        sa = jnp.sum(jnp.where(lo, x, 0.0), axis=1, keepdims=True)
        sb = jnp.sum(jnp.where(lo, 0.0, x), axis=1, keepdims=True)
        return jnp.where(lo, sa, sb)

    def step(t, carry):
        row = lambda ref: ref[0, pl.ds(t, 1), :]
        r_t, w_t, k_t, v_t, a_t, b_t = (row(x) for x in (r_ref, w_ref, k_ref, v_ref, a_ref, b_ref))
        w_t = jnp.exp(w_t)
        ys = []
        for p in range(N_PAIRS):
            sl = slice(p * LANES, (p + 1) * LANES)
            s = s_ref[p]
            sa = head_sums(s * a_t[:, sl])
            vc = head_sums(jnp.where(diag, v_t[:, sl], 0.0))
            s = s * w_t[:, sl] + sa * b_t[:, sl] + vc * k_t[:, sl]
            s_ref[p] = s
            yc = head_sums(s * r_t[:, sl])
            ys.append(jnp.sum(jnp.where(diag, yc, 0.0), axis=0, keepdims=True))
        y_ref[0, pl.ds(t, 1), :] = jnp.concatenate(ys, axis=1)
        return carry

    lax.fori_loop(0, tb, step, 0)

    @pl.when(tblk == pl.num_programs(1) - 1)
    def _():
        sf_ref[0] = s_ref[...]


def _rwkv_scan(r, w, k, v, a, b, s0, tb):
    bsz, t, _ = r.shape
    s0p = s0.reshape(bsz, N_PAIRS, 2, HEAD, HEAD).transpose(0, 1, 3, 2, 4).reshape(bsz, N_PAIRS, HEAD, LANES)
    seq = pl.BlockSpec((1, tb, MIX), lambda i, j: (i, j, 0))
    st = pl.BlockSpec((1, N_PAIRS, HEAD, LANES), lambda i, j: (i, 0, 0, 0))
    y, sf = pl.pallas_call(
        functools.partial(_rwkv_scan_kernel, tb),
        out_shape=(jax.ShapeDtypeStruct((bsz, t, MIX), F32),
                   jax.ShapeDtypeStruct((bsz, N_PAIRS, HEAD, LANES), F32)),
        grid=(bsz, t // tb),
        in_specs=[seq] * 6 + [st],
        out_specs=(seq, st),
        scratch_shapes=[pltpu.VMEM((N_PAIRS, HEAD, LANES), F32)],
        compiler_params=_cparams(("parallel", "arbitrary"), 48),
        name="rwkv_scan",
    )(r, w, k, v, a, b, s0p)
    sf = sf.reshape(bsz, N_PAIRS, HEAD, 2, HEAD).transpose(0, 1, 3, 2, 4).reshape(bsz, N_HEADS, HEAD, HEAD)
    return y, sf


CHUNK = 64
PAIR_GROUP = 12


def _rwkv_chunk_kernel(r_ref, w_ref, k_ref, v_ref, a_ref, b_ref, s0_ref, y_ref, sf_ref, s_ref):
    L = CHUNK
    assert 2 * L == LANES

    @pl.when(pl.program_id(0) == 0)
    def _():
        s_ref[...] = s0_ref[...]

    ri = lax.broadcasted_iota(jnp.int32, (L, L), 0)
    ci = lax.broadcasted_iota(jnp.int32, (L, L), 1)
    tri = (ci <= ri).astype(BF16)
    i2 = lax.broadcasted_iota(jnp.int32, (LANES, LANES), 0)
    j2 = lax.broadcasted_iota(jnp.int32, (LANES, LANES), 1)
    top, left = i2 < L, j2 < L
    s_idx, t_idx = j2 & (L - 1), i2 & (L - 1)
    is_top = jnp.where(top, 1, 0)
    keep_a = s_idx < t_idx + (1 - is_top)
    keep_b = s_idx < t_idx + is_top
    ul = top & left
    lr = jnp.logical_not(top | left)
    same_head = ul | lr
    eye = (i2 == j2).astype(F32)
    lo = lax.broadcasted_iota(jnp.int32, (1, LANES), 1) < HEAD
    zeros = jnp.zeros((L, LANES), F32)
    cat = lambda x, y: jnp.concatenate([x, y], axis=0)
    dot = lambda x, y: jnp.dot(x.astype(BF16), y.astype(BF16), preferred_element_type=F32)
    dot_nt = lambda x, y: lax.dot_general(x.astype(BF16), y.astype(BF16), _NT, preferred_element_type=F32)

    pairs = range(N_PAIRS)
    each = lambda f, *cols: [f(*xs) for xs in zip(*cols)]
    cols = [[x[:, p * LANES:(p + 1) * LANES] for p in pairs] for x in (r_ref, w_ref, k_ref, v_ref, a_ref, b_ref)]
    r, lw, k, v, a, b = cols

    def log_cumdecay(lw):
        hi, mid, low = _split3(lw)
        return dot(tri, hi) + dot(tri, mid) + dot(tri, low)

    c = each(log_cumdecay, lw)
    c_last = [x[L - 1:L, :] for x in c]
    at = each(lambda a, c, lw: a * jnp.exp(c - lw), a, c, lw)
    rt = each(lambda r, c: r * jnp.exp(c), r, c)
    em = [jnp.exp(-x) for x in c]
    kh, bh = each(jnp.multiply, k, em), each(jnp.multiply, b, em)
    kb = each(lambda k, b, c, cl: cat(k, b) * jnp.exp(cat(cl - c, cl - c)), k, b, c, c_last)
    vm_a, vm_b = [jnp.where(lo, x, 0.0) for x in v], [jnp.where(lo, 0.0, x) for x in v]
    s = [s_ref[p] for p in pairs]

    def state_products(at, rt, s):
        s_hi = s.astype(BF16)
        ar = cat(at, rt)
        return dot_nt(ar, s_hi) + dot_nt(ar, s - s_hi.astype(F32))

    ps = each(state_products, at, rt, s)
    m_a = each(lambda at, rt, kh, bh: jnp.where(keep_a, dot_nt(jnp.where(lo, cat(at, rt), 0.0), cat(bh, kh)), 0.0),
               at, rt, kh, bh)
    m_b = each(lambda at, rt, kh, bh: jnp.where(keep_b, dot_nt(jnp.where(lo, 0.0, cat(rt, at)), cat(kh, bh)), 0.0),
               at, rt, kh, bh)
    rhs = each(lambda ps, m_a, m_b, vm_a, vm_b: ps[:L] + dot(m_a[:L], cat(zeros, vm_a)) + dot(m_b[L:], cat(vm_b, zeros)),
               ps, m_a, m_b, vm_a, vm_b)

    qs = each(lambda m_a, m_b: jnp.where(ul, m_a, 0.0) + jnp.where(lr, m_b, 0.0), m_a, m_b)
    invs = [eye + q for q in qs]
    for _ in range(5):
        qs = [dot(q, q) for q in qs]
        invs = each(lambda inv, q: inv + dot(inv, q), invs, qs)

    u2 = each(lambda inv, rhs: dot(inv, cat(jnp.where(lo, rhs, 0.0), jnp.where(lo, 0.0, rhs))), invs, rhs)
    y = each(lambda ps, m_a, m_b, u2, vm_a, vm_b: ps[L:] + dot(m_a[L:], cat(u2[:L], vm_a)) + dot(m_b[:L], cat(vm_b, u2[L:])),
             ps, m_a, m_b, u2, vm_a, vm_b)
    for p in pairs:
        y_ref[:, p * LANES:(p + 1) * LANES] = y[p]
    upd = each(lambda v, u2, kb: dot(cat(v, u2[:L] + u2[L:]).T, kb), v, u2, kb)
    for p in pairs:
        s_ref[p] = s[p] * jnp.exp(c_last[p]) + jnp.where(same_head, upd[p], 0.0)

    @pl.when(pl.program_id(0) == pl.num_programs(0) - 1)
    def _():
        sf_ref[...] = s_ref[...]


def _rwkv_chunked(r, w, k, v, a, b, s0):
    t = r.shape[0]
    pair_eye = jnp.eye(2, dtype=F32)[None, :, None, :, None]
    s0p = (s0.reshape(N_PAIRS, 2, HEAD, 1, HEAD) * pair_eye).reshape(N_PAIRS, LANES, LANES)
    seq = pl.BlockSpec((CHUNK, MIX), lambda i: (i, 0))
    st = pl.BlockSpec((N_PAIRS, LANES, LANES), lambda i: (0, 0, 0))
    y, sf = pl.pallas_call(
        _rwkv_chunk_kernel,
        out_shape=(jax.ShapeDtypeStruct((t, MIX), F32), jax.ShapeDtypeStruct((N_PAIRS, LANES, LANES), F32)),
        grid=(t // CHUNK,),
        in_specs=[seq] * 6 + [st],
        out_specs=(seq, st),
        scratch_shapes=[pltpu.VMEM((N_PAIRS, LANES, LANES), F32)],
        compiler_params=_cparams(("arbitrary",), 48),
        name="rwkv_chunked",
    )(r, w, k, v, a, b, s0p)
    sf = sf.reshape(N_PAIRS, 2, HEAD, 2, HEAD)
    sf = jnp.stack([sf[:, 0, :, 0], sf[:, 1, :, 1]], axis=1).reshape(N_HEADS, HEAD, HEAD)
    return y, sf


def _rwkv_post_kernel(y_ref, bonus_ref, g_ref, lw_ref, lb_ref, ones_ref, o_ref):
    y = y_ref[...]
    ones = ones_ref[...]
    m = _dot_exact_rhs(y, ones) * (1.0 / HEAD)
    d = y - m
    var = _dot_exact_rhs(d * d, ones) * (1.0 / HEAD)
    yn = d * lax.rsqrt(var + GN_EPS) * lw_ref[...] + lb_ref[...]
    o_ref[...] = (yn + bonus_ref[...]) * g_ref[...]


def _rwkv_post(y, bonus, g, lnx_w, lnx_b, ones, tm):
    m = y.shape[0]
    blk = pl.BlockSpec((tm, LANES), lambda i, c: (i, c))
    vec = pl.BlockSpec((1, LANES), lambda i, c: (0, c))
    return pl.pallas_call(
        _rwkv_post_kernel,
        out_shape=jax.ShapeDtypeStruct((m, MIX), F32),
        grid=(m // tm, MIX // LANES),
        in_specs=[blk, blk, blk, vec, vec, pl.BlockSpec((LANES, LANES), lambda i, c: (0, 0))],
        out_specs=blk,
        compiler_params=_cparams(("parallel", "parallel"), 32),
        name="rwkv_post",
    )(y, bonus, g, lnx_w, lnx_b, ones)


def _sb_inside(z, lower, allowed, two_pass):
    neg_abs = pltpu.bitcast(pltpu.bitcast(z, jnp.uint32) | jnp.uint32(0x80000000), F32)
    sp = jnp.maximum(z, 0.0) + jnp.log(1.0 + jnp.exp(neg_abs))
    ls = z - sp
    if allowed is not None:
        sp = jnp.where(allowed, sp, 0.0)
    lhs = sp.astype(BF16)
    if two_pass:
        lhs = jnp.concatenate([lhs, (sp - lhs.astype(F32)).astype(BF16)], axis=1)
    right = jnp.dot(lhs, lower, preferred_element_type=F32)
    return ls, right, right[:, 0:1] + sp[:, 0:1]


def _sb_weights(scores, carry, allowed):
    ls, right, total = scores
    a = jnp.exp(ls - right - carry)
    if allowed is not None:
        a = jnp.where(allowed, a, 0.0)
    return a, carry + total


def _sb_tile(z, carry, lower, allowed, two_pass):
    return _sb_weights(_sb_inside(z, lower, allowed, two_pass), carry, allowed)


def _sb_prompt_kernel(tq, q_ref, k_ref, v_ref, up_ref, o_ref, kt_ref, vt_ref, kb_ref, va_ref, vb_ref, acc_ref,
                      z_ref, p_ref, c_ref):
    qi = pl.program_id(1)
    lane = lax.broadcasted_iota(jnp.int32, (1, LANES), 1)
    lo = lane < HEAD

    @pl.when(qi == 0)
    def _():
        kb_ref[...] = k_ref[...].astype(BF16)
        v = v_ref[...]
        va_ref[...] = jnp.where(lo, v, 0.0).astype(BF16)
        vb_ref[...] = jnp.where(lo, 0.0, v).astype(BF16)

        def to_cache_layout(g, carry):
            rows = pl.ds(pl.multiple_of(g * PAGE, PAGE), PAGE)
            kt_ref[g] = k_ref[rows, :].T
            vt_ref[g] = v_ref[rows, :].T
            return carry

        lax.fori_loop(0, kt_ref.shape[0], to_cache_layout, 0)

    q = q_ref[...] * (HEAD ** -0.5)
    qa = jnp.where(lo, q, 0.0).astype(BF16)
    qb = jnp.where(lo, 0.0, q).astype(BF16)
    lower = up_ref[...]

    def logits_to(j, slot):
        kblk = kb_ref[pl.ds(pl.multiple_of(jnp.maximum(j, 0) * tq, tq), tq), :]
        z_ref[slot, 0] = lax.dot_general(qa, kblk, _NT, preferred_element_type=F32)
        z_ref[slot, 1] = lax.dot_general(qb, kblk, _NT, preferred_element_type=F32)

    def accumulate(j, slot):
        ks = pl.multiple_of(j * tq, tq)
        acc_ref[...] += (jnp.dot(p_ref[slot, 0], va_ref[pl.ds(ks, tq), :], preferred_element_type=F32)
                         + jnp.dot(p_ref[slot, 1], vb_ref[pl.ds(ks, tq), :], preferred_element_type=F32))

    def weights_to(za, zb, slot, allowed):
        pa, ca = _sb_tile(za, c_ref[0], lower, allowed, False)
        pb, cb = _sb_tile(zb, c_ref[1], lower, allowed, False)
        p_ref[slot, 0] = pa.astype(BF16)
        p_ref[slot, 1] = pb.astype(BF16)
        c_ref[0] = ca
        c_ref[1] = cb

    def step(j, src, dst):
        accumulate(j + 1, src)
        za, zb = z_ref[src, 0], z_ref[src, 1]
        logits_to(j - 1, dst)
        weights_to(za, zb, dst, None)

    acc_ref[...] = jnp.zeros_like(acc_ref)
    c_ref[...] = jnp.zeros_like(c_ref)
    below_diag = (lax.broadcasted_iota(jnp.int32, (tq, tq), 1) < lax.broadcasted_iota(jnp.int32, (tq, tq), 0))
    logits_to(qi, 1)
    za, zb = z_ref[1, 0], z_ref[1, 1]
    logits_to(qi - 1, 0)
    weights_to(za, zb, 0, below_diag)

    odd = qi & 1

    @pl.when(odd == 1)
    def _():
        step(qi - 1, 0, 0)

    def pair(n, carry):
        j = qi - 1 - odd - 2 * n
        step(j, 0, 1)
        step(j - 1, 1, 0)
        return carry

    lax.fori_loop(0, qi >> 1, pair, 0)
    accumulate(0, 0)
    o_ref[...] = acc_ref[...]


def _sb_prompt(z, upper, tq):
    s = z.shape[0]
    nc = MIX // LANES
    return pl.pallas_call(
        functools.partial(_sb_prompt_kernel, tq),
        out_shape=(jax.ShapeDtypeStruct((s, MIX), F32),) + (jax.ShapeDtypeStruct((s // PAGE, MIX, PAGE), F32),) * 2,
        grid=(nc, s // tq),
        in_specs=[
            pl.BlockSpec((tq, LANES), lambda c, i: (i, c)),
            pl.BlockSpec((s, LANES), lambda c, i: (0, nc + c)),
            pl.BlockSpec((s, LANES), lambda c, i: (0, 2 * nc + c)),
            pl.BlockSpec((tq, tq), lambda c, i: (0, 0)),
        ],
        out_specs=(pl.BlockSpec((tq, LANES), lambda c, i: (i, c)),)
                  + (pl.BlockSpec((s // PAGE, LANES, PAGE), lambda c, i: (0, c, 0)),) * 2,
        scratch_shapes=[pltpu.VMEM((s, LANES), BF16)] * 3
                       + [pltpu.VMEM((tq, LANES), F32),
                          pltpu.VMEM((2, 2, tq, tq), F32),
                          pltpu.VMEM((2, 2, tq, tq), BF16),
                          pltpu.VMEM((2, tq, 1), F32)],
        compiler_params=_cparams(("parallel", "arbitrary"), 56),
        name="sb_prompt",
    )(z, z, z, upper)


QROWS = 8
SB_PAGE_GROUP = 4


def _sb_sample_kernel(group, pt_ref, q_ref, kn_ref, vn_ref, *refs):
    kc_refs, vc_refs = refs[:group], refs[group:2 * group]
    up_ref, o_ref, acc_ref, carry_ref = refs[2 * group:]
    p = pl.program_id(1)
    lower2 = up_ref[...]
    rows = N_HEADS * QROWS

    def tiles(k_refs, v_refs, carry, allowed):
        zs = [jnp.concatenate([jnp.dot(q_ref[0, h], k_ref[0, h].astype(BF16), preferred_element_type=F32)
                               for h in range(N_HEADS)], axis=0) for k_ref in k_refs]
        scores = [_sb_inside(z, lower2, allowed, True) for z in zs]
        total = None
        for score, v_ref in zip(scores, v_refs):
            a, carry = _sb_weights(score, carry, allowed)
            outs = [lax.dot_general(a[h * QROWS:(h + 1) * QROWS].astype(BF16), v_ref[0, h].astype(BF16),
                                    _NT, preferred_element_type=F32) for h in range(N_HEADS)]
            total = outs if total is None else [t + o for t, o in zip(total, outs)]
        return total, carry

    @pl.when(p == 0)
    def _():
        key = lax.broadcasted_iota(jnp.int32, (rows, PAGE), 1)
        qry = lax.broadcasted_iota(jnp.int32, (rows, PAGE), 0) & (QROWS - 1)
        outs, carry = tiles([kn_ref], [vn_ref], jnp.zeros((rows, 1), F32), key < qry)
        carry_ref[...] = carry
        for h in range(N_HEADS):
            acc_ref[h] = outs[h]

    @pl.when(p > 0)
    def _():
        outs, carry = tiles(kc_refs, vc_refs, carry_ref[...], None)
        carry_ref[...] = carry
        for h in range(N_HEADS):
            acc_ref[h] += outs[h]

    @pl.when(p == pl.num_programs(1) - 1)
    def _():
        o_ref[0] = acc_ref[...]


def _sb_sample(q, k_new, v_new, cache_k, cache_v, page_table, lower2):
    bsz = q.shape[0]
    n_pages = page_table.shape[1]
    group = next(g for g in (SB_PAGE_GROUP, 2, 1) if n_pages % g == 0)

    def page(g):
        return lambda b, p, pt: (pt[b, n_pages - 1 - (jnp.maximum(p, 1) - 1) * group - g], 0, 0, 0)

    own = lambda b, p, pt: (b, 0, 0, 0)
    kv_block = (1, N_HEADS, HEAD, PAGE)
    grid_spec = pltpu.PrefetchScalarGridSpec(
        num_scalar_prefetch=1,
        grid=(bsz, n_pages // group + 1),
        in_specs=[pl.BlockSpec((1, N_HEADS, QROWS, HEAD), own), pl.BlockSpec(kv_block, own),
                  pl.BlockSpec(kv_block, own)]
                 + [pl.BlockSpec(kv_block, page(g)) for g in range(group)] * 2
                 + [pl.BlockSpec((2 * PAGE, PAGE), lambda b, p, pt: (0, 0))],
        out_specs=pl.BlockSpec((1, N_HEADS, QROWS, HEAD), own),
        scratch_shapes=[pltpu.VMEM((N_HEADS, QROWS, HEAD), F32), pltpu.VMEM((N_HEADS * QROWS, 1), F32)],
    )
    return pl.pallas_call(
        functools.partial(_sb_sample_kernel, group),
        out_shape=jax.ShapeDtypeStruct((bsz, N_HEADS, QROWS, HEAD), F32),
        grid_spec=grid_spec,
        compiler_params=_cparams(("parallel", "arbitrary"), 48),
        name="sb_sample",
    )(page_table, q, k_new, v_new, *([cache_k] * group), *([cache_v] * group), lower2)


def _rmsnorm_kernel(x_ref, g_ref, o_ref):
    o_ref[...] = _rms_rows(x_ref[...], g_ref[...])


def _rmsnorm(x, g, tm):
    m, d = x.shape
    return pl.pallas_call(
        _rmsnorm_kernel,
        out_shape=jax.ShapeDtypeStruct((m, d), F32),
        grid=(m // tm,),
        in_specs=[pl.BlockSpec((tm, d), lambda i: (i, 0)), pl.BlockSpec((1, d), lambda i: (0, 0))],
        out_specs=pl.BlockSpec((tm, d), lambda i: (i, 0)),
        compiler_params=_cparams(("parallel",), 32),
        name="rmsnorm",
    )(x, g.reshape(1, d))


def _pad_cols(x, width):
    return jnp.pad(x, ((0, 0), (0, width - x.shape[1])))


def _rwkv_cols(x):
    o = 3 * MIX
    parts = [x[:, :o],
             _pad_cols(x[:, o:o + DECAY_LORA], LANES),
             _pad_cols(x[:, o + DECAY_LORA:o + DECAY_LORA + ICLR_LORA], LANES),
             _pad_cols(x[:, o + DECAY_LORA + ICLR_LORA:RWKV_PROJ], 2 * LANES)]
    if x.shape[1] > RWKV_PROJ:
        parts.append(x[:, RWKV_PROJ:])
    return jnp.concatenate(parts, axis=1)


def _rwkv_uncols(x):
    o = 3 * MIX
    return jnp.concatenate([x[:, :o], x[:, o:o + DECAY_LORA], x[:, o + LANES:o + LANES + ICLR_LORA],
                            x[:, o + 2 * LANES:o + 2 * LANES + GATE_LORA]], axis=1)


def _pad_rows(x, rows):
    return jnp.pad(x, ((0, rows - x.shape[0]), (0, 0)))


def _tile(m, pref):
    return pref if m % pref == 0 else m


def kernel(x_prompt, x_sample, mem_prompt, state_rwkv_wkv, state_rwkv_shift, cache_sb_k, cache_sb_v, page_table, cache_mem_k, cache_mem_v, norm_mix, norm_ffn, norm_final, mem_norm, w_mem_kv, w_out, w_ffn_up, w_ffn_down, rwkv_w_in, rwkv_mu, rwkv_w0, rwkv_w2, rwkv_a0, rwkv_a2, rwkv_g2, rwkv_k_k, rwkv_k_a, rwkv_r_k, rwkv_lnx_w, rwkv_lnx_b, sb_w_in, sb_b_qk):
    bp, s, d = x_prompt.shape
    db, t_new, _ = x_sample.shape
    assert bp == 1, "the prompt path handles a single sequence"
    depth = norm_mix.shape[0]
    ms = db * t_new
    n_mem = mem_prompt.shape[1]
    xp = x_prompt.reshape(s, d)
    xs = x_sample.reshape(ms, d)
    tmp = _tile(s, 512)
    tms = ms
    tmp_in = _tile(s, 1024)
    col_tile = lambda n: 1024 if n % 1024 == 0 else 512

    idx = lax.broadcasted_iota(jnp.int32, (LANES, LANES), 0)
    jdx = lax.broadcasted_iota(jnp.int32, (LANES, LANES), 1)
    head_ones = ((idx // HEAD) == (jdx // HEAD)).astype(BF16)
    tq = _tile(s, 256)
    iq = lax.broadcasted_iota(jnp.int32, (tq, tq), 0)
    jq = lax.broadcasted_iota(jnp.int32, (tq, tq), 1)
    lower_q = (iq > jq).astype(BF16)
    lower_p = jnp.tile((idx > jdx).astype(BF16), (2, 1))

    wu_all, wd_all = w_ffn_up.astype(BF16), w_ffn_down.astype(BF16)
    cmk_rows = cache_mem_k.reshape(depth * db, n_mem * MEM_HEADS, MEM_HD)
    cmv_rows = cache_mem_v.reshape(depth * db, n_mem * MEM_HEADS, MEM_HD)
    mem_k, mem_v = [], []
    wkv_p, shift_p, wkv_s, shift_s = [], [], [], []
    sbk_p, sbv_p, sbk_s, sbv_s = [], [], [], []
    for i in range(depth):
        j = i // 2
        kv = _rms_matmul(mem_prompt.reshape(n_mem, d), mem_norm[i], w_mem_kv[i].astype(BF16),
                         jnp.zeros((2 * MEM_W,), F32), n_mem, 2 * MEM_W)
        mk, mv = kv[:, :MEM_W], kv[:, MEM_W:]
        mem_k.append(mk.reshape(1, n_mem, MEM_HEADS, MEM_HD))
        mem_v.append(mv.reshape(1, n_mem, MEM_HEADS, MEM_HD))
        if i % 2 == 0:
            w_in = _rwkv_cols(rwkv_w_in[j]).astype(BF16)
            n_in = w_in.shape[1]
            zero_b = jnp.zeros((n_in,), F32)
            zp = _rms_matmul(xp, norm_mix[i], w_in, zero_b, tmp_in, col_tile(n_in))
            zs = _rms_matmul(xs, norm_mix[i], w_in, zero_b, tms, col_tile(n_in))
            shift_p.append(_rwkv_uncols(zp[s - 1:s]))
            shift_s.append(_rwkv_uncols(zs.reshape(db, t_new, n_in)[:, t_new - 1]))
            wide = 3 * MIX + LORA_PAD
            mu = _rwkv_cols(rwkv_mu[j][None])
            vecs = [x.reshape(1, MIX) for x in (rwkv_w0[j], rwkv_a0[j], rwkv_k_k[j], rwkv_k_a[j], rwkv_r_k[j])]
            lora = (_pad_rows(rwkv_w2[j], LANES).astype(BF16), _pad_rows(rwkv_a2[j], LANES).astype(BF16),
                    _pad_rows(rwkv_g2[j], 2 * LANES).astype(BF16))
            lnx = (rwkv_lnx_w[j].reshape(1, MIX), rwkv_lnx_b[j].reshape(1, MIX))

            def rwkv(z, shift_rows, seq_len, tm, s0, nb, tb, tpost):
                r, w, k, v, a, b, g, bonus = _rwkv_prep(z, shift_rows, seq_len, tm, mu, *vecs, *lora, head_ones)
                if nb == 1 and seq_len % CHUNK == 0:
                    y, s_fin = _rwkv_chunked(r, w, k, v, a, b, s0[0])
                    s_fin = s_fin[None]
                else:
                    sh = lambda x: x.reshape(nb, seq_len, MIX)
                    y, s_fin = _rwkv_scan(sh(r), sh(w), sh(k), sh(v), sh(a), sh(b), s0, tb)
                return _rwkv_post(y.reshape(nb * seq_len, MIX), bonus, g, *lnx, head_ones, tpost), s_fin

            zero_shift = jnp.zeros((8, wide), F32)
            zero_state = jnp.zeros((1, N_HEADS, HEAD, HEAD), F32)
            mix_p, st_p = rwkv(zp, zero_shift, s, tmp, zero_state, 1, _tile(s, 256), tmp)
            shift_rows = jnp.repeat(_rwkv_cols(state_rwkv_shift[j]), t_new, axis=0)
            mix_s, st_s = rwkv(zs, shift_rows, t_new, tms, state_rwkv_wkv[j], db, t_new, tms)
            wkv_p.append(st_p)
            wkv_s.append(st_s)
            q_off = wide
        else:
            w_in = sb_w_in[j].astype(BF16)
            bias = jnp.concatenate([sb_b_qk[j], jnp.zeros((w_in.shape[1] - 2 * MIX,), F32)])
            zp = _rms_matmul(xp, norm_mix[i], w_in, bias, tmp_in, col_tile(w_in.shape[1]))
            zs = _rms_matmul(xs, norm_mix[i], w_in, bias, tms, col_tile(w_in.shape[1]))
            mix_p, kt_p, vt_p = _sb_prompt(zp, lower_q, tq)
            assert t_new < QROWS
            heads = lambda x: x.reshape(db, t_new, N_HEADS, HEAD)
            q_s, k_s, v_s = heads(zs[:, :MIX]), heads(zs[:, MIX:2 * MIX]), heads(zs[:, 2 * MIX:3 * MIX])
            q_hq = jnp.pad((q_s * (HEAD ** -0.5)).transpose(0, 2, 1, 3),
                           ((0, 0), (0, 0), (0, QROWS - t_new), (0, 0))).astype(BF16)
            keys_last = lambda x: x.transpose(0, 2, 3, 1)
            pad_new = lambda x: keys_last(jnp.pad(x, ((0, 0), (0, PAGE - t_new), (0, 0), (0, 0))))
            o_hq = _sb_sample(q_hq, pad_new(k_s), pad_new(v_s), keys_last(cache_sb_k[j]),
                              keys_last(cache_sb_v[j]), page_table, lower_p)
            mix_s = o_hq[:, :, :t_new].transpose(0, 2, 1, 3).reshape(ms, MIX)
            as_cache = lambda x: x.reshape(1, s // PAGE, N_HEADS, HEAD, PAGE).transpose(0, 1, 4, 2, 3)
            sbk_p.append(as_cache(kt_p))
            sbv_p.append(as_cache(vt_p))
            sbk_s.append(k_s)
            sbv_s.append(v_s)
            q_off = 3 * MIX
        cp = _mem_attend(zp[:, q_off:q_off + MEM_W][None], mk[None], mv[None], tmp)[0]
        cs = _mem_attend(zs[:, q_off:q_off + MEM_W].reshape(db, t_new, MEM_W), cmk_rows, cmv_rows, t_new,
                         first=i * db).reshape(ms, MEM_W)
        wo = w_out[i].astype(BF16)
        xp = _outproj(xp, mix_p, cp, wo[:MIX], wo[MIX:], tmp)
        xs = _outproj(xs, mix_s, cs, wo[:MIX], wo[MIX:], tms)
        xp = _ffn(xp, norm_ffn[i], wu_all, wd_all, i, tmp, col_tile(wu_all.shape[2]))
        xs = _ffn(xs, norm_ffn[i], wu_all, wd_all, i, tms, col_tile(wu_all.shape[2]))
    y_prompt = _rmsnorm(xp, norm_final, tmp).reshape(1, s, d)
    y_sample = _rmsnorm(xs, norm_final, tms).reshape(db, t_new, d)
    return (y_prompt, y_sample, jnp.stack(mem_k), jnp.stack(mem_v),
            jnp.stack(wkv_p), jnp.stack(shift_p), jnp.stack(wkv_s), jnp.stack(shift_s),
            jnp.stack(sbk_p), jnp.stack(sbv_p), jnp.stack(sbk_s), jnp.stack(sbv_s))
```

```python
import functools

import jax
import jax.numpy as jnp
from jax import lax
from jax.experimental import pallas as pl
from jax.experimental.pallas import tpu as pltpu

F32 = jnp.float32
BF16 = jnp.bfloat16

HEAD = 64
N_HEADS = 24
MIX = N_HEADS * HEAD
N_PAIRS = N_HEADS // 2
LANES = 128
MEM_HEADS = 4
MEM_HD = 128
MEM_W = MEM_HEADS * MEM_HD
DECAY_LORA = 64
ICLR_LORA = 64
GATE_LORA = 224
LORA_PAD = 512
RWKV_PROJ = 3 * MIX + DECAY_LORA + ICLR_LORA + GATE_LORA
RMS_EPS = 1e-6
GN_EPS = 64e-5
PAGE = 128

_NT = (((1,), (1,)), ((), ()))


def _cparams(sem, vmem_mb):
    return pltpu.CompilerParams(dimension_semantics=sem, vmem_limit_bytes=vmem_mb << 20)


def _split3(x):
    hi = x.astype(BF16)
    r1 = x - hi.astype(F32)
    mid = r1.astype(BF16)
    lo = (r1 - mid.astype(F32)).astype(BF16)
    return hi, mid, lo


def _dot_exact_rhs(x, ones_bf16):
    hi, mid, lo = _split3(x)
    d = lambda a: jnp.dot(a, ones_bf16, preferred_element_type=F32)
    return d(hi) + d(mid) + d(lo)


def _rms_rows(x, g):
    ms = jnp.mean(x * x, axis=-1, keepdims=True)
    return x * lax.rsqrt(ms + RMS_EPS) * g


def _rms_matmul_kernel(x_ref, g_ref, w_ref, b_ref, o_ref, xn_ref):
    @pl.when(pl.program_id(1) == 0)
    def _():
        xn_ref[...] = _rms_rows(x_ref[...], g_ref[...]).astype(BF16)

    o_ref[...] = jnp.dot(xn_ref[...], w_ref[...], preferred_element_type=F32) + b_ref[...]


def _rms_matmul(x, g, w, b, tm, tn):
    m, d = x.shape
    n = w.shape[1]
    return pl.pallas_call(
        _rms_matmul_kernel,
        out_shape=jax.ShapeDtypeStruct((m, n), F32),
        grid=(m // tm, n // tn),
        in_specs=[
            pl.BlockSpec((tm, d), lambda i, j: (i, 0)),
            pl.BlockSpec((1, d), lambda i, j: (0, 0)),
            pl.BlockSpec((d, tn), lambda i, j: (0, j)),
            pl.BlockSpec((1, tn), lambda i, j: (0, j)),
        ],
        out_specs=pl.BlockSpec((tm, tn), lambda i, j: (i, j)),
        scratch_shapes=[pltpu.VMEM((tm, d), BF16)],
        compiler_params=_cparams(("parallel", "arbitrary"), 48),
        name="rms_matmul",
    )(x, g.reshape(1, d), w, b.reshape(1, n))


def _ffn_kernel(x_ref, g_ref, wu_ref, wd_ref, o_ref, xn_ref):
    @pl.when(pl.program_id(1) == 0)
    def _():
        x = x_ref[...]
        xn_ref[...] = _rms_rows(x, g_ref[...]).astype(BF16)
        o_ref[...] = x

    h = jnp.maximum(jnp.dot(xn_ref[...], wu_ref[...], preferred_element_type=F32), 0.0)
    o_ref[...] += jnp.dot((h * h).astype(BF16), wd_ref[...], preferred_element_type=F32)


def _ffn(x, g, wu, wd, layer, tm, tf):
    m, d = x.shape
    f = wu.shape[2]
    return pl.pallas_call(
        _ffn_kernel,
        out_shape=jax.ShapeDtypeStruct((m, d), F32),
        grid=(m // tm, f // tf),
        in_specs=[
            pl.BlockSpec((tm, d), lambda i, j: (i, 0)),
            pl.BlockSpec((1, d), lambda i, j: (0, 0)),
            pl.BlockSpec((None, d, tf), lambda i, j: (layer, 0, j)),
            pl.BlockSpec((None, tf, d), lambda i, j: (layer, j, 0)),
        ],
        out_specs=pl.BlockSpec((tm, d), lambda i, j: (i, 0)),
        scratch_shapes=[pltpu.VMEM((tm, d), BF16)],
        compiler_params=_cparams(("parallel", "arbitrary"), 48),
        name="ffn",
    )(x, g.reshape(1, d), wu, wd)


def _outproj_kernel(x_ref, mix_ref, c_ref, w1_ref, w2_ref, o_ref):
    acc = jnp.dot(mix_ref[...].astype(BF16), w1_ref[...], preferred_element_type=F32)
    acc += jnp.dot(c_ref[...].astype(BF16), w2_ref[...], preferred_element_type=F32)
    o_ref[...] = x_ref[...] + acc


def _outproj(x, mix, c, w1, w2, tm):
    m, d = x.shape
    return pl.pallas_call(
        _outproj_kernel,
        out_shape=jax.ShapeDtypeStruct((m, d), F32),
        grid=(m // tm,),
        in_specs=[
            pl.BlockSpec((tm, d), lambda i: (i, 0)),
            pl.BlockSpec((tm, MIX), lambda i: (i, 0)),
            pl.BlockSpec((tm, MEM_W), lambda i: (i, 0)),
            pl.BlockSpec((MIX, d), lambda i: (0, 0)),
            pl.BlockSpec((MEM_W, d), lambda i: (0, 0)),
        ],
        out_specs=pl.BlockSpec((tm, d), lambda i: (i, 0)),
        compiler_params=_cparams(("parallel",), 48),
        name="outproj",
    )(x, mix, c, w1, w2)


def _mem_attend_kernel(head_rows, q_ref, k_ref, v_ref, o_ref):
    q = q_ref[0]
    heads = [slice(h * MEM_HD, (h + 1) * MEM_HD) for h in range(MEM_HEADS)]

    def head(ref, h):
        if head_rows:
            return ref[0, pl.ds(h, ref.shape[1] // MEM_HEADS, stride=MEM_HEADS), :].astype(BF16)
        return ref[0, :, heads[h]].astype(BF16)

    s = [lax.dot_general(q[:, sl].astype(BF16), head(k_ref, h), _NT, preferred_element_type=F32) * (MEM_HD ** -0.5)
         for h, sl in enumerate(heads)]
    e = [jnp.exp(x - jnp.max(x, axis=-1, keepdims=True)) for x in s]
    p = [x / jnp.sum(x, axis=-1, keepdims=True) for x in e]
    for h, sl in enumerate(heads):
        o_ref[0, :, sl] = jnp.dot(p[h].astype(BF16), head(v_ref, h), preferred_element_type=F32)


def _mem_attend(q, mk, mv, tt, first=0):
    b, t, _ = q.shape
    head_rows = mk.shape[2] == MEM_HD
    kv_spec = pl.BlockSpec((1,) + mk.shape[1:], lambda i, j: (first + i, 0, 0))
    return pl.pallas_call(
        functools.partial(_mem_attend_kernel, head_rows),
        out_shape=jax.ShapeDtypeStruct((b, t, MEM_W), F32),
        grid=(b, t // tt),
        in_specs=[pl.BlockSpec((1, tt, MEM_W), lambda i, j: (i, j, 0)), kv_spec, kv_spec],
        out_specs=pl.BlockSpec((1, tt, MEM_W), lambda i, j: (i, j, 0)),
        compiler_params=_cparams(("parallel", "parallel"), 32),
        name="mem_attend",
    )(q, mk, mv)


def _rwkv_prep_kernel(seq_len, tm,
                      pr_ref, pk_ref, pv_ref, pl_ref,
                      br_ref, bk_ref, bv_ref, bl_ref,
                      sr_ref, sk_ref, sv_ref, sl_ref,
                      mur_ref, muk_ref, muv_ref, mul_ref,
                      w0_ref, a0_ref, kk_ref, ka_ref, rk_ref,
                      w2_ref, a2_ref, g2_ref, ones_ref,
                      r_out, w_out, k_out, v_out, a_out, b_out, g_out, bonus_out):
    i = pl.program_id(0)
    row = lax.broadcasted_iota(jnp.int32, (tm, 1), 0)

    def shifted(cur_ref, before_ref, state_ref, mu_ref):
        cur = cur_ref[...]
        rolled = pltpu.roll(cur, 1, 0)
        if seq_len >= tm:
            first = jnp.where(i == 0, state_ref[0:1, :], before_ref[7:8, :])
            prev = jnp.where(row == 0, first, rolled)
        else:
            prev = jnp.where(row % seq_len == 0, state_ref[...], rolled)
        return cur + (prev - cur) * mu_ref[...]

    r = shifted(pr_ref, br_ref, sr_ref, mur_ref)
    k = shifted(pk_ref, bk_ref, sk_ref, muk_ref)
    v = shifted(pv_ref, bv_ref, sv_ref, muv_ref)
    lo = shifted(pl_ref, bl_ref, sl_ref, mul_ref)

    xw = jnp.tanh(lo[:, 0:128]).astype(BF16)
    xa = lo[:, 128:256].astype(BF16)
    xg = jax.nn.sigmoid(lo[:, 256:512]).astype(BF16)
    wl = w0_ref[...] + jnp.dot(xw, w2_ref[...], preferred_element_type=F32)
    w_log = -jax.nn.softplus(-wl) - 0.5
    log_decay = -jnp.exp(w_log)
    a = jax.nn.sigmoid(a0_ref[...] + jnp.dot(xa, a2_ref[...], preferred_element_type=F32))
    g = jnp.dot(xg, g2_ref[...], preferred_element_type=F32)

    ones = ones_ref[...]
    kk = k * kk_ref[...]
    nrm = jnp.sqrt(_dot_exact_rhs(kk * kk, ones))
    kk = kk / jnp.maximum(nrm, 1e-12)
    k2 = k * (1.0 + (a - 1.0) * ka_ref[...])
    bonus = _dot_exact_rhs(r * k2 * rk_ref[...], ones) * v

    r_out[...] = r
    w_out[...] = log_decay
    k_out[...] = k2
    v_out[...] = v
    a_out[...] = -kk
    b_out[...] = kk * a
    g_out[...] = g
    bonus_out[...] = bonus


def _rwkv_prep(z, shift_rows, seq_len, tm, mu, w0, a0, k_k, k_a, r_k, w2, a2, g2, ones):
    m = z.shape[0]
    nc = MIX // LANES
    lb = 3 * MIX // LORA_PAD
    tb = tm // 8
    srows = shift_rows.shape[0] if seq_len >= tm else tm

    def cur(off):
        return pl.BlockSpec((tm, LANES), lambda i, c: (i, off + c))

    def before(off):
        return pl.BlockSpec((8, LANES), lambda i, c: (jnp.maximum(i * tb - 1, 0), off + c))

    def state(off):
        if seq_len >= tm:
            return pl.BlockSpec((srows, LANES), lambda i, c: (0, off + c))
        return pl.BlockSpec((tm, LANES), lambda i, c: (i, off + c))

    def vec(off):
        return pl.BlockSpec((1, LANES), lambda i, c: (0, off + c))

    cur_l = pl.BlockSpec((tm, LORA_PAD), lambda i, c: (i, lb))
    before_l = pl.BlockSpec((8, LORA_PAD), lambda i, c: (jnp.maximum(i * tb - 1, 0), lb))
    if seq_len >= tm:
        state_l = pl.BlockSpec((srows, LORA_PAD), lambda i, c: (0, lb))
    else:
        state_l = pl.BlockSpec((tm, LORA_PAD), lambda i, c: (i, lb))
    vec_l = pl.BlockSpec((1, LORA_PAD), lambda i, c: (0, lb))

    out_spec = pl.BlockSpec((tm, LANES), lambda i, c: (i, c))
    out_sds = jax.ShapeDtypeStruct((m, MIX), F32)
    return pl.pallas_call(
        functools.partial(_rwkv_prep_kernel, seq_len, tm),
        out_shape=(out_sds,) * 8,
        grid=(m // tm, nc),
        in_specs=[
            cur(0), cur(nc), cur(2 * nc), cur_l,
            before(0), before(nc), before(2 * nc), before_l,
            state(0), state(nc), state(2 * nc), state_l,
            vec(0), vec(nc), vec(2 * nc), vec_l,
            vec(0), vec(0), vec(0), vec(0), vec(0),
            pl.BlockSpec((LANES, LANES), lambda i, c: (0, c)),
            pl.BlockSpec((LANES, LANES), lambda i, c: (0, c)),
            pl.BlockSpec((2 * LANES, LANES), lambda i, c: (0, c)),
            pl.BlockSpec((LANES, LANES), lambda i, c: (0, 0)),
        ],
        out_specs=(out_spec,) * 8,
        compiler_params=_cparams(("parallel", "parallel"), 48),
        name="rwkv_prep",
    )(z, z, z, z, z, z, z, z, shift_rows, shift_rows, shift_rows, shift_rows,
      mu, mu, mu, mu, w0, a0, k_k, k_a, r_k, w2, a2, g2, ones)


def _rwkv_scan_kernel(tb, r_ref, w_ref, k_ref, v_ref, a_ref, b_ref, s0_ref, y_ref, sf_ref, s_ref):
    tblk = pl.program_id(1)

    @pl.when(tblk == 0)
    def _():
        for p in range(N_PAIRS):
            s_ref[p] = jnp.concatenate([s0_ref[0, 2 * p], s0_ref[0, 2 * p + 1]], axis=1)

    lane = lax.broadcasted_iota(jnp.int32, (HEAD, LANES), 1)
    sub = lax.broadcasted_iota(jnp.int32, (HEAD, LANES), 0)
    lo = lane < HEAD
    diag = (lane % HEAD) == sub

    def head_sums(x):
        sa = jnp.sum(jnp.where(lo, x, 0.0), axis=1, keepdims=True)
        sb = jnp.sum(jnp.where(lo, 0.0, x), axis=1, keepdims=True)
        return jnp.where(lo, sa, sb)

    def step(t, carry):
        row = lambda ref: ref[0, pl.ds(t, 1), :]
        r_t, w_t, k_t, v_t, a_t, b_t = (row(x) for x in (r_ref, w_ref, k_ref, v_ref, a_ref, b_ref))
        w_t = jnp.exp(w_t)
        pairs = [slice(p * LANES, (p + 1) * LANES) for p in range(N_PAIRS)]
        s = [s_ref[p] for p in range(N_PAIRS)]
        sa = [head_sums(s[p] * a_t[:, sl]) for p, sl in enumerate(pairs)]
        vc = [head_sums(jnp.where(diag, v_t[:, sl], 0.0)) for sl in pairs]
        s = [s[p] * w_t[:, sl] + sa[p] * b_t[:, sl] + vc[p] * k_t[:, sl] for p, sl in enumerate(pairs)]
        for p in range(N_PAIRS):
            s_ref[p] = s[p]
        yc = [head_sums(s[p] * r_t[:, sl]) for p, sl in enumerate(pairs)]
        ys = [jnp.sum(jnp.where(diag, y, 0.0), axis=0, keepdims=True) for y in yc]
        y_ref[0, pl.ds(t, 1), :] = jnp.concatenate(ys, axis=1)
        return carry

    lax.fori_loop(0, tb, step, 0)

    @pl.when(tblk == pl.num_programs(1) - 1)
    def _():
        for p in range(N_PAIRS):
            sf_ref[0, 2 * p] = s_ref[p, :, :HEAD]
            sf_ref[0, 2 * p + 1] = s_ref[p, :, HEAD:]


def _rwkv_scan(r, w, k, v, a, b, s0, tb):
    bsz, t, _ = r.shape
    seq = pl.BlockSpec((1, tb, MIX), lambda i, j: (i, j, 0))
    st = pl.BlockSpec((1, N_HEADS, HEAD, HEAD), lambda i, j: (i, 0, 0, 0))
    return pl.pallas_call(
        functools.partial(_rwkv_scan_kernel, tb),
        out_shape=(jax.ShapeDtypeStruct((bsz, t, MIX), F32),
                   jax.ShapeDtypeStruct((bsz, N_HEADS, HEAD, HEAD), F32)),
        grid=(bsz, t // tb),
        in_specs=[seq] * 6 + [st],
        out_specs=(seq, st),
        scratch_shapes=[pltpu.VMEM((N_PAIRS, HEAD, LANES), F32)],
        compiler_params=_cparams(("parallel", "arbitrary"), 48),
        name="rwkv_scan",
    )(r, w, k, v, a, b, s0)


CHUNK = 64
PAIR_GROUP = 12


def _rwkv_chunk_kernel(r_ref, w_ref, k_ref, v_ref, a_ref, b_ref, s0_ref, y_ref, sf_ref, s_ref):
    L = CHUNK
    assert 2 * L == LANES

    @pl.when(pl.program_id(0) == 0)
    def _():
        s_ref[...] = s0_ref[...]

    ri = lax.broadcasted_iota(jnp.int32, (L, L), 0)
    ci = lax.broadcasted_iota(jnp.int32, (L, L), 1)
    tri = (ci <= ri).astype(BF16)
    i2 = lax.broadcasted_iota(jnp.int32, (LANES, LANES), 0)
    j2 = lax.broadcasted_iota(jnp.int32, (LANES, LANES), 1)
    top, left = i2 < L, j2 < L
    s_idx, t_idx = j2 & (L - 1), i2 & (L - 1)
    is_top = jnp.where(top, 1, 0)
    keep_a = s_idx < t_idx + (1 - is_top)
    keep_b = s_idx < t_idx + is_top
    ul = top & left
    lr = jnp.logical_not(top | left)
    same_head = ul | lr
    eye = (i2 == j2).astype(F32)
    lo = lax.broadcasted_iota(jnp.int32, (1, LANES), 1) < HEAD
    zeros = jnp.zeros((L, LANES), F32)
    cat = lambda x, y: jnp.concatenate([x, y], axis=0)
    dot = lambda x, y: jnp.dot(x.astype(BF16), y.astype(BF16), preferred_element_type=F32)
    dot_nt = lambda x, y: lax.dot_general(x.astype(BF16), y.astype(BF16), _NT, preferred_element_type=F32)

    pairs = range(N_PAIRS)
    each = lambda f, *cols: [f(*xs) for xs in zip(*cols)]
    cols = [[x[:, p * LANES:(p + 1) * LANES] for p in pairs] for x in (r_ref, w_ref, k_ref, v_ref, a_ref, b_ref)]
    r, lw, k, v, a, b = cols

    def log_cumdecay(lw):
        hi, mid, low = _split3(lw)
        return dot(tri, hi) + dot(tri, mid) + dot(tri, low)

    c = each(log_cumdecay, lw)
    c_last = [x[L - 1:L, :] for x in c]
    at = each(lambda a, c, lw: a * jnp.exp(c - lw), a, c, lw)
    rt = each(lambda r, c: r * jnp.exp(c), r, c)
    em = [jnp.exp(-x) for x in c]
    kh, bh = each(jnp.multiply, k, em), each(jnp.multiply, b, em)
    kb = each(lambda k, b, c, cl: cat(k, b) * jnp.exp(cat(cl - c, cl - c)), k, b, c, c_last)
    vm_a, vm_b = [jnp.where(lo, x, 0.0) for x in v], [jnp.where(lo, 0.0, x) for x in v]
    s = [s_ref[p] for p in pairs]

    def state_products(at, rt, s):
        s_hi = s.astype(BF16)
        ar = cat(at, rt)
        return dot_nt(ar, s_hi) + dot_nt(ar, s - s_hi.astype(F32))

    ps = each(state_products, at, rt, s)
    m_a = each(lambda at, rt, kh, bh: jnp.where(keep_a, dot_nt(jnp.where(lo, cat(at, rt), 0.0), cat(bh, kh)), 0.0),
               at, rt, kh, bh)
    m_b = each(lambda at, rt, kh, bh: jnp.where(keep_b, dot_nt(jnp.where(lo, 0.0, cat(rt, at)), cat(kh, bh)), 0.0),
               at, rt, kh, bh)
    rhs = each(lambda ps, m_a, m_b, vm_a, vm_b: ps[:L] + dot(m_a[:L], cat(zeros, vm_a)) + dot(m_b[L:], cat(vm_b, zeros)),
               ps, m_a, m_b, vm_a, vm_b)

    qs = each(lambda m_a, m_b: jnp.where(ul, m_a, 0.0) + jnp.where(lr, m_b, 0.0), m_a, m_b)
    invs = [eye + q for q in qs]
    for _ in range(5):
        qs = [dot(q, q) for q in qs]
        invs = each(lambda inv, q: inv + dot(inv, q), invs, qs)

    u2 = each(lambda inv, rhs: dot(inv, cat(jnp.where(lo, rhs, 0.0), jnp.where(lo, 0.0, rhs))), invs, rhs)
    y = each(lambda ps, m_a, m_b, u2, vm_a, vm_b: ps[L:] + dot(m_a[L:], cat(u2[:L], vm_a)) + dot(m_b[:L], cat(vm_b, u2[L:])),
             ps, m_a, m_b, u2, vm_a, vm_b)
    for p in pairs:
        y_ref[:, p * LANES:(p + 1) * LANES] = y[p]
    upd = each(lambda v, u2, kb: dot(cat(v, u2[:L] + u2[L:]).T, kb), v, u2, kb)
    for p in pairs:
        s_ref[p] = s[p] * jnp.exp(c_last[p]) + jnp.where(same_head, upd[p], 0.0)

    @pl.when(pl.program_id(0) == pl.num_programs(0) - 1)
    def _():
        sf_ref[...] = s_ref[...]


def _rwkv_chunked(r, w, k, v, a, b, s0):
    t = r.shape[0]
    pair_eye = jnp.eye(2, dtype=F32)[None, :, None, :, None]
    s0p = (s0.reshape(N_PAIRS, 2, HEAD, 1, HEAD) * pair_eye).reshape(N_PAIRS, LANES, LANES)
    seq = pl.BlockSpec((CHUNK, MIX), lambda i: (i, 0))
    st = pl.BlockSpec((N_PAIRS, LANES, LANES), lambda i: (0, 0, 0))
    y, sf = pl.pallas_call(
        _rwkv_chunk_kernel,
        out_shape=(jax.ShapeDtypeStruct((t, MIX), F32), jax.ShapeDtypeStruct((N_PAIRS, LANES, LANES), F32)),
        grid=(t // CHUNK,),
        in_specs=[seq] * 6 + [st],
        out_specs=(seq, st),
        scratch_shapes=[pltpu.VMEM((N_PAIRS, LANES, LANES), F32)],
        compiler_params=_cparams(("arbitrary",), 48),
        name="rwkv_chunked",
    )(r, w, k, v, a, b, s0p)
    sf = sf.reshape(N_PAIRS, 2, HEAD, 2, HEAD)
    sf = jnp.stack([sf[:, 0, :, 0], sf[:, 1, :, 1]], axis=1).reshape(N_HEADS, HEAD, HEAD)
    return y, sf


def _rwkv_post_kernel(y_ref, bonus_ref, g_ref, lw_ref, lb_ref, ones_ref, o_ref):
    y = y_ref[...]
    ones = ones_ref[...]
    m = _dot_exact_rhs(y, ones) * (1.0 / HEAD)
    d = y - m
    var = _dot_exact_rhs(d * d, ones) * (1.0 / HEAD)
    yn = d * lax.rsqrt(var + GN_EPS) * lw_ref[...] + lb_ref[...]
    o_ref[...] = (yn + bonus_ref[...]) * g_ref[...]


def _rwkv_post(y, bonus, g, lnx_w, lnx_b, ones, tm):
    m = y.shape[0]
    blk = pl.BlockSpec((tm, LANES), lambda i, c: (i, c))
    vec = pl.BlockSpec((1, LANES), lambda i, c: (0, c))
    return pl.pallas_call(
        _rwkv_post_kernel,
        out_shape=jax.ShapeDtypeStruct((m, MIX), F32),
        grid=(m // tm, MIX // LANES),
        in_specs=[blk, blk, blk, vec, vec, pl.BlockSpec((LANES, LANES), lambda i, c: (0, 0))],
        out_specs=blk,
        compiler_params=_cparams(("parallel", "parallel"), 32),
        name="rwkv_post",
    )(y, bonus, g, lnx_w, lnx_b, ones)


def _sb_inside(z, lower, allowed, two_pass):
    neg_abs = pltpu.bitcast(pltpu.bitcast(z, jnp.uint32) | jnp.uint32(0x80000000), F32)
    sp = jnp.maximum(z, 0.0) + jnp.log(1.0 + jnp.exp(neg_abs))
    ls = z - sp
    if allowed is not None:
        sp = jnp.where(allowed, sp, 0.0)
    lhs = sp.astype(BF16)
    if two_pass:
        lhs = jnp.concatenate([lhs, (sp - lhs.astype(F32)).astype(BF16)], axis=1)
    right = jnp.dot(lhs, lower, preferred_element_type=F32)
    return ls, right, right[:, 0:1] + sp[:, 0:1]


def _sb_weights(scores, carry, allowed):
    ls, right, total = scores
    a = jnp.exp(ls - right - carry)
    if allowed is not None:
        a = jnp.where(allowed, a, 0.0)
    return a, carry + total


def _sb_tile(z, carry, lower, allowed, two_pass):
    return _sb_weights(_sb_inside(z, lower, allowed, two_pass), carry, allowed)


def _sb_prompt_kernel(tq, q_ref, k_ref, v_ref, up_ref, o_ref, kt_ref, vt_ref, kb_ref, va_ref, vb_ref, acc_ref,
                      z_ref, p_ref, c_ref):
    qi = pl.program_id(1)
    lane = lax.broadcasted_iota(jnp.int32, (1, LANES), 1)
    lo = lane < HEAD

    @pl.when(qi == 0)
    def _():
        kb_ref[...] = k_ref[...].astype(BF16)
        v = v_ref[...]
        va_ref[...] = jnp.where(lo, v, 0.0).astype(BF16)
        vb_ref[...] = jnp.where(lo, 0.0, v).astype(BF16)

        def to_cache_layout(g, carry):
            rows = pl.ds(pl.multiple_of(g * PAGE, PAGE), PAGE)
            kt_ref[g] = k_ref[rows, :].T
            vt_ref[g] = v_ref[rows, :].T
            return carry

        lax.fori_loop(0, kt_ref.shape[0], to_cache_layout, 0)

    q = q_ref[...] * (HEAD ** -0.5)
    qa = jnp.where(lo, q, 0.0).astype(BF16)
    qb = jnp.where(lo, 0.0, q).astype(BF16)
    lower = up_ref[...]

    def logits_to(j, slot):
        kblk = kb_ref[pl.ds(pl.multiple_of(jnp.maximum(j, 0) * tq, tq), tq), :]
        z_ref[slot, 0] = lax.dot_general(qa, kblk, _NT, preferred_element_type=F32)
        z_ref[slot, 1] = lax.dot_general(qb, kblk, _NT, preferred_element_type=F32)

    def accumulate(j, slot):
        ks = pl.multiple_of(j * tq, tq)
        acc_ref[...] += (jnp.dot(p_ref[slot, 0], va_ref[pl.ds(ks, tq), :], preferred_element_type=F32)
                         + jnp.dot(p_ref[slot, 1], vb_ref[pl.ds(ks, tq), :], preferred_element_type=F32))

    def weights_to(za, zb, slot, allowed):
        pa, ca = _sb_tile(za, c_ref[0], lower, allowed, False)
        pb, cb = _sb_tile(zb, c_ref[1], lower, allowed, False)
        p_ref[slot, 0] = pa.astype(BF16)
        p_ref[slot, 1] = pb.astype(BF16)
        c_ref[0] = ca
        c_ref[1] = cb

    def step(j, src, dst):
        accumulate(j + 1, src)
        za, zb = z_ref[src, 0], z_ref[src, 1]
        logits_to(j - 1, dst)
        weights_to(za, zb, dst, None)

    acc_ref[...] = jnp.zeros_like(acc_ref)
    c_ref[...] = jnp.zeros_like(c_ref)
    below_diag = (lax.broadcasted_iota(jnp.int32, (tq, tq), 1) < lax.broadcasted_iota(jnp.int32, (tq, tq), 0))
    logits_to(qi, 1)
    za, zb = z_ref[1, 0], z_ref[1, 1]
    logits_to(qi - 1, 0)
    weights_to(za, zb, 0, below_diag)

    odd = qi & 1

    @pl.when(odd == 1)
    def _():
        step(qi - 1, 0, 0)

    def pair(n, carry):
        j = qi - 1 - odd - 2 * n
        step(j, 0, 1)
        step(j - 1, 1, 0)
        return carry

    lax.fori_loop(0, qi >> 1, pair, 0)
    accumulate(0, 0)
    o_ref[...] = acc_ref[...]


def _sb_prompt(z, upper, tq):
    s = z.shape[0]
    nc = MIX // LANES
    return pl.pallas_call(
        functools.partial(_sb_prompt_kernel, tq),
        out_shape=(jax.ShapeDtypeStruct((s, MIX), F32),) + (jax.ShapeDtypeStruct((s // PAGE, MIX, PAGE), F32),) * 2,
        grid=(nc, s // tq),
        in_specs=[
            pl.BlockSpec((tq, LANES), lambda c, i: (i, c)),
            pl.BlockSpec((s, LANES), lambda c, i: (0, nc + c)),
            pl.BlockSpec((s, LANES), lambda c, i: (0, 2 * nc + c)),
            pl.BlockSpec((tq, tq), lambda c, i: (0, 0)),
        ],
        out_specs=(pl.BlockSpec((tq, LANES), lambda c, i: (i, c)),)
                  + (pl.BlockSpec((s // PAGE, LANES, PAGE), lambda c, i: (0, c, 0)),) * 2,
        scratch_shapes=[pltpu.VMEM((s, LANES), BF16)] * 3
                       + [pltpu.VMEM((tq, LANES), F32),
                          pltpu.VMEM((2, 2, tq, tq), F32),
                          pltpu.VMEM((2, 2, tq, tq), BF16),
                          pltpu.VMEM((2, tq, 1), F32)],
        compiler_params=_cparams(("parallel", "arbitrary"), 56),
        name="sb_prompt",
    )(z, z, z, upper)


QROWS = 8
SB_PAGE_GROUP = 8


def _sb_sample_kernel(group, pt_ref, q_ref, kn_ref, vn_ref, *refs):
    kc_refs, vc_refs = refs[:group], refs[group:2 * group]
    up_ref, o_ref, acc_ref, carry_ref = refs[2 * group:]
    p = pl.program_id(1)
    lower2 = up_ref[...]
    rows = N_HEADS * QROWS

    def tiles(k_refs, v_refs, carry, allowed):
        zs = [jnp.concatenate([jnp.dot(q_ref[0, h], k_ref[0, h].astype(BF16), preferred_element_type=F32)
                               for h in range(N_HEADS)], axis=0) for k_ref in k_refs]
        scores = [_sb_inside(z, lower2, allowed, True) for z in zs]
        total = None
        for score, v_ref in zip(scores, v_refs):
            a, carry = _sb_weights(score, carry, allowed)
            outs = [lax.dot_general(a[h * QROWS:(h + 1) * QROWS].astype(BF16), v_ref[0, h].astype(BF16),
                                    _NT, preferred_element_type=F32) for h in range(N_HEADS)]
            total = outs if total is None else [t + o for t, o in zip(total, outs)]
        return total, carry

    @pl.when(p == 0)
    def _():
        key = lax.broadcasted_iota(jnp.int32, (rows, PAGE), 1)
        qry = lax.broadcasted_iota(jnp.int32, (rows, PAGE), 0) & (QROWS - 1)
        outs, carry = tiles([kn_ref], [vn_ref], jnp.zeros((rows, 1), F32), key < qry)
        carry_ref[...] = carry
        for h in range(N_HEADS):
            acc_ref[h] = outs[h]

    @pl.when(p > 0)
    def _():
        outs, carry = tiles(kc_refs, vc_refs, carry_ref[...], None)
        carry_ref[...] = carry
        for h in range(N_HEADS):
            acc_ref[h] += outs[h]

    @pl.when(p == pl.num_programs(1) - 1)
    def _():
        o_ref[0] = acc_ref[...]


def _sb_sample(q, k_new, v_new, cache_k, cache_v, page_table, lower2):
    bsz = q.shape[0]
    n_pages = page_table.shape[1]
    group = next(g for g in (SB_PAGE_GROUP, 4, 2, 1) if n_pages % g == 0)

    def page(g):
        return lambda b, p, pt: (pt[b, n_pages - 1 - (jnp.maximum(p, 1) - 1) * group - g], 0, 0, 0)

    own = lambda b, p, pt: (b, 0, 0, 0)
    kv_block = (1, N_HEADS, HEAD, PAGE)
    grid_spec = pltpu.PrefetchScalarGridSpec(
        num_scalar_prefetch=1,
        grid=(bsz, n_pages // group + 1),
        in_specs=[pl.BlockSpec((1, N_HEADS, QROWS, HEAD), own), pl.BlockSpec(kv_block, own),
                  pl.BlockSpec(kv_block, own)]
                 + [pl.BlockSpec(kv_block, page(g)) for g in range(group)] * 2
                 + [pl.BlockSpec((2 * PAGE, PAGE), lambda b, p, pt: (0, 0))],
        out_specs=pl.BlockSpec((1, N_HEADS, QROWS, HEAD), own),
        scratch_shapes=[pltpu.VMEM((N_HEADS, QROWS, HEAD), F32), pltpu.VMEM((N_HEADS * QROWS, 1), F32)],
    )
    return pl.pallas_call(
        functools.partial(_sb_sample_kernel, group),
        out_shape=jax.ShapeDtypeStruct((bsz, N_HEADS, QROWS, HEAD), F32),
        grid_spec=grid_spec,
        compiler_params=_cparams(("parallel", "arbitrary"), 48),
        name="sb_sample",
    )(page_table, q, k_new, v_new, *([cache_k] * group), *([cache_v] * group), lower2)


def _rmsnorm_kernel(x_ref, g_ref, o_ref):
    o_ref[...] = _rms_rows(x_ref[...], g_ref[...])


def _rmsnorm(x, g, tm):
    m, d = x.shape
    return pl.pallas_call(
        _rmsnorm_kernel,
        out_shape=jax.ShapeDtypeStruct((m, d), F32),
        grid=(m // tm,),
        in_specs=[pl.BlockSpec((tm, d), lambda i: (i, 0)), pl.BlockSpec((1, d), lambda i: (0, 0))],
        out_specs=pl.BlockSpec((tm, d), lambda i: (i, 0)),
        compiler_params=_cparams(("parallel",), 32),
        name="rmsnorm",
    )(x, g.reshape(1, d))


def _pad_cols(x, width):
    return jnp.pad(x, ((0, 0), (0, width - x.shape[1])))


def _rwkv_cols(x):
    o = 3 * MIX
    parts = [x[:, :o],
             _pad_cols(x[:, o:o + DECAY_LORA], LANES),
             _pad_cols(x[:, o + DECAY_LORA:o + DECAY_LORA + ICLR_LORA], LANES),
             _pad_cols(x[:, o + DECAY_LORA + ICLR_LORA:RWKV_PROJ], 2 * LANES)]
    if x.shape[1] > RWKV_PROJ:
        parts.append(x[:, RWKV_PROJ:])
    return jnp.concatenate(parts, axis=1)


def _rwkv_uncols(x):
    o = 3 * MIX
    return jnp.concatenate([x[:, :o], x[:, o:o + DECAY_LORA], x[:, o + LANES:o + LANES + ICLR_LORA],
                            x[:, o + 2 * LANES:o + 2 * LANES + GATE_LORA]], axis=1)


def _pad_rows(x, rows):
    return jnp.pad(x, ((0, rows - x.shape[0]), (0, 0)))


def _tile(m, pref):
    return pref if m % pref == 0 else m


def kernel(x_prompt, x_sample, mem_prompt, state_rwkv_wkv, state_rwkv_shift, cache_sb_k, cache_sb_v, page_table, cache_mem_k, cache_mem_v, norm_mix, norm_ffn, norm_final, mem_norm, w_mem_kv, w_out, w_ffn_up, w_ffn_down, rwkv_w_in, rwkv_mu, rwkv_w0, rwkv_w2, rwkv_a0, rwkv_a2, rwkv_g2, rwkv_k_k, rwkv_k_a, rwkv_r_k, rwkv_lnx_w, rwkv_lnx_b, sb_w_in, sb_b_qk):
    bp, s, d = x_prompt.shape
    db, t_new, _ = x_sample.shape
    assert bp == 1, "the prompt path handles a single sequence"
    depth = norm_mix.shape[0]
    ms = db * t_new
    n_mem = mem_prompt.shape[1]
    xp = x_prompt.reshape(s, d)
    xs = x_sample.reshape(ms, d)
    tmp = _tile(s, 512)
    tms = ms
    tmp_in = _tile(s, 1024)
    col_tile = lambda n: 1024 if n % 1024 == 0 else 512

    idx = lax.broadcasted_iota(jnp.int32, (LANES, LANES), 0)
    jdx = lax.broadcasted_iota(jnp.int32, (LANES, LANES), 1)
    head_ones = ((idx // HEAD) == (jdx // HEAD)).astype(BF16)
    tq = _tile(s, 256)
    iq = lax.broadcasted_iota(jnp.int32, (tq, tq), 0)
    jq = lax.broadcasted_iota(jnp.int32, (tq, tq), 1)
    lower_q = (iq > jq).astype(BF16)
    lower_p = jnp.tile((idx > jdx).astype(BF16), (2, 1))

    wu_all, wd_all = w_ffn_up.astype(BF16), w_ffn_down.astype(BF16)
    cmk_rows = cache_mem_k.reshape(depth * db, n_mem * MEM_HEADS, MEM_HD)
    cmv_rows = cache_mem_v.reshape(depth * db, n_mem * MEM_HEADS, MEM_HD)
    mem_k, mem_v = [], []
    wkv_p, shift_p, wkv_s, shift_s = [], [], [], []
    sbk_p, sbv_p, sbk_s, sbv_s = [], [], [], []
    for i in range(depth):
        j = i // 2
        kv = _rms_matmul(mem_prompt.reshape(n_mem, d), mem_norm[i], w_mem_kv[i].astype(BF16),
                         jnp.zeros((2 * MEM_W,), F32), n_mem, 2 * MEM_W)
        mk, mv = kv[:, :MEM_W], kv[:, MEM_W:]
        mem_k.append(mk.reshape(1, n_mem, MEM_HEADS, MEM_HD))
        mem_v.append(mv.reshape(1, n_mem, MEM_HEADS, MEM_HD))
        if i % 2 == 0:
            w_in = _rwkv_cols(rwkv_w_in[j]).astype(BF16)
            n_in = w_in.shape[1]
            zero_b = jnp.zeros((n_in,), F32)
            zp = _rms_matmul(xp, norm_mix[i], w_in, zero_b, tmp_in, col_tile(n_in))
            zs = _rms_matmul(xs, norm_mix[i], w_in, zero_b, tms, col_tile(n_in))
            shift_p.append(_rwkv_uncols(zp[s - 1:s]))
            shift_s.append(_rwkv_uncols(zs.reshape(db, t_new, n_in)[:, t_new - 1]))
            wide = 3 * MIX + LORA_PAD
            mu = _rwkv_cols(rwkv_mu[j][None])
            vecs = [x.reshape(1, MIX) for x in (rwkv_w0[j], rwkv_a0[j], rwkv_k_k[j], rwkv_k_a[j], rwkv_r_k[j])]
            lora = (_pad_rows(rwkv_w2[j], LANES).astype(BF16), _pad_rows(rwkv_a2[j], LANES).astype(BF16),
                    _pad_rows(rwkv_g2[j], 2 * LANES).astype(BF16))
            lnx = (rwkv_lnx_w[j].reshape(1, MIX), rwkv_lnx_b[j].reshape(1, MIX))

            def rwkv(z, shift_rows, seq_len, tm, s0, nb, tb, tpost):
                r, w, k, v, a, b, g, bonus = _rwkv_prep(z, shift_rows, seq_len, tm, mu, *vecs, *lora, head_ones)
                if nb == 1 and seq_len % CHUNK == 0:
                    y, s_fin = _rwkv_chunked(r, w, k, v, a, b, s0[0])
                    s_fin = s_fin[None]
                else:
                    sh = lambda x: x.reshape(nb, seq_len, MIX)
                    y, s_fin = _rwkv_scan(sh(r), sh(w), sh(k), sh(v), sh(a), sh(b), s0, tb)
                return _rwkv_post(y.reshape(nb * seq_len, MIX), bonus, g, *lnx, head_ones, tpost), s_fin

            zero_shift = jnp.zeros((8, wide), F32)
            zero_state = jnp.zeros((1, N_HEADS, HEAD, HEAD), F32)
            mix_p, st_p = rwkv(zp, zero_shift, s, tmp_in, zero_state, 1, _tile(s, 256), tmp_in)
            shift_rows = jnp.repeat(_rwkv_cols(state_rwkv_shift[j]), t_new, axis=0)
            mix_s, st_s = rwkv(zs, shift_rows, t_new, tms, state_rwkv_wkv[j], db, t_new, tms)
            wkv_p.append(st_p)
            wkv_s.append(st_s)
            q_off = wide
        else:
            w_in = sb_w_in[j].astype(BF16)
            bias = jnp.concatenate([sb_b_qk[j], jnp.zeros((w_in.shape[1] - 2 * MIX,), F32)])
            zp = _rms_matmul(xp, norm_mix[i], w_in, bias, tmp_in, col_tile(w_in.shape[1]))
            zs = _rms_matmul(xs, norm_mix[i], w_in, bias, tms, col_tile(w_in.shape[1]))
            mix_p, kt_p, vt_p = _sb_prompt(zp, lower_q, tq)
            assert t_new < QROWS
            heads = lambda x: x.reshape(db, t_new, N_HEADS, HEAD)
            q_s, k_s, v_s = heads(zs[:, :MIX]), heads(zs[:, MIX:2 * MIX]), heads(zs[:, 2 * MIX:3 * MIX])
            q_hq = jnp.pad((q_s * (HEAD ** -0.5)).transpose(0, 2, 1, 3),
                           ((0, 0), (0, 0), (0, QROWS - t_new), (0, 0))).astype(BF16)
            keys_last = lambda x: x.transpose(0, 2, 3, 1)
            pad_new = lambda x: keys_last(jnp.pad(x, ((0, 0), (0, PAGE - t_new), (0, 0), (0, 0))))
            o_hq = _sb_sample(q_hq, pad_new(k_s), pad_new(v_s), keys_last(cache_sb_k[j]),
                              keys_last(cache_sb_v[j]), page_table, lower_p)
            mix_s = o_hq[:, :, :t_new].transpose(0, 2, 1, 3).reshape(ms, MIX)
            as_cache = lambda x: x.reshape(1, s // PAGE, N_HEADS, HEAD, PAGE).transpose(0, 1, 4, 2, 3)
            sbk_p.append(as_cache(kt_p))
            sbv_p.append(as_cache(vt_p))
            sbk_s.append(k_s)
            sbv_s.append(v_s)
            q_off = 3 * MIX
        cp = _mem_attend(zp[:, q_off:q_off + MEM_W][None], mk[None], mv[None], tmp)[0]
        cs = _mem_attend(zs[:, q_off:q_off + MEM_W].reshape(db, t_new, MEM_W), cmk_rows, cmv_rows, t_new,
                         first=i * db).reshape(ms, MEM_W)
        wo = w_out[i].astype(BF16)
        xp = _outproj(xp, mix_p, cp, wo[:MIX], wo[MIX:], tmp)
        xs = _outproj(xs, mix_s, cs, wo[:MIX], wo[MIX:], tms)
        xp = _ffn(xp, norm_ffn[i], wu_all, wd_all, i, tmp, col_tile(wu_all.shape[2]))
        xs = _ffn(xs, norm_ffn[i], wu_all, wd_all, i, tms, col_tile(wu_all.shape[2]))
    y_prompt = _rmsnorm(xp, norm_final, tmp).reshape(1, s, d)
    y_sample = _rmsnorm(xs, norm_final, tms).reshape(db, t_new, d)
    return (y_prompt, y_sample, jnp.stack(mem_k), jnp.stack(mem_v),
            jnp.stack(wkv_p), jnp.stack(shift_p), jnp.stack(wkv_s), jnp.stack(shift_s),
            jnp.stack(sbk_p), jnp.stack(sbv_p), jnp.stack(sbk_s), jnp.stack(sbv_s))
```

```python
import functools

import jax
import jax.numpy as jnp
from jax import lax
from jax.experimental import pallas as pl
from jax.experimental.pallas import tpu as pltpu

F32 = jnp.float32
BF16 = jnp.bfloat16

HEAD = 64
N_HEADS = 24
MIX = N_HEADS * HEAD
N_PAIRS = N_HEADS // 2
LANES = 128
MEM_HEADS = 4
MEM_HD = 128
MEM_W = MEM_HEADS * MEM_HD
DECAY_LORA = 64
ICLR_LORA = 64
GATE_LORA = 224
LORA_PAD = 512
RWKV_PROJ = 3 * MIX + DECAY_LORA + ICLR_LORA + GATE_LORA
RMS_EPS = 1e-6
GN_EPS = 64e-5
PAGE = 128

_NT = (((1,), (1,)), ((), ()))


def _cparams(sem, vmem_mb):
    return pltpu.CompilerParams(dimension_semantics=sem, vmem_limit_bytes=vmem_mb << 20)


def _split3(x):
    hi = x.astype(BF16)
    r1 = x - hi.astype(F32)
    mid = r1.astype(BF16)
    lo = (r1 - mid.astype(F32)).astype(BF16)
    return hi, mid, lo


def _dot_exact_rhs(x, ones_bf16):
    hi, mid, lo = _split3(x)
    d = lambda a: jnp.dot(a, ones_bf16, preferred_element_type=F32)
    return d(hi) + d(mid) + d(lo)


def _rms_rows(x, g):
    ms = jnp.mean(x * x, axis=-1, keepdims=True)
    return x * lax.rsqrt(ms + RMS_EPS) * g


def _rms_matmul_kernel(x_ref, g_ref, w_ref, b_ref, o_ref, xn_ref):
    @pl.when(pl.program_id(1) == 0)
    def _():
        xn_ref[...] = _rms_rows(x_ref[...], g_ref[...]).astype(BF16)

    o_ref[...] = jnp.dot(xn_ref[...], w_ref[...], preferred_element_type=F32) + b_ref[...]


def _rms_matmul(x, g, w, b, tm, tn):
    m, d = x.shape
    n = w.shape[1]
    return pl.pallas_call(
        _rms_matmul_kernel,
        out_shape=jax.ShapeDtypeStruct((m, n), F32),
        grid=(m // tm, n // tn),
        in_specs=[
            pl.BlockSpec((tm, d), lambda i, j: (i, 0)),
            pl.BlockSpec((1, d), lambda i, j: (0, 0)),
            pl.BlockSpec((d, tn), lambda i, j: (0, j)),
            pl.BlockSpec((1, tn), lambda i, j: (0, j)),
        ],
        out_specs=pl.BlockSpec((tm, tn), lambda i, j: (i, j)),
        scratch_shapes=[pltpu.VMEM((tm, d), BF16)],
        compiler_params=_cparams(("parallel", "arbitrary"), 48),
        name="rms_matmul",
    )(x, g.reshape(1, d), w, b.reshape(1, n))


def _ffn_kernel(norm_out, x_ref, g_ref, go_ref, wu_ref, wd_ref, o_ref, xn_ref):
    @pl.when(pl.program_id(1) == 0)
    def _():
        x = x_ref[...]
        xn_ref[...] = _rms_rows(x, g_ref[...]).astype(BF16)
        o_ref[...] = x

    h = jnp.maximum(jnp.dot(xn_ref[...], wu_ref[...], preferred_element_type=F32), 0.0)
    o_ref[...] += jnp.dot((h * h).astype(BF16), wd_ref[...], preferred_element_type=F32)

    if norm_out:
        @pl.when(pl.program_id(1) == pl.num_programs(1) - 1)
        def _():
            o_ref[...] = _rms_rows(o_ref[...], go_ref[...])


def _ffn(x, g, wu, wd, layer, tm, tf, out_gain, norm_out):
    m, d = x.shape
    f = wu.shape[2]
    vec = pl.BlockSpec((1, d), lambda i, j: (0, 0))
    return pl.pallas_call(
        functools.partial(_ffn_kernel, norm_out),
        out_shape=jax.ShapeDtypeStruct((m, d), F32),
        grid=(m // tm, f // tf),
        in_specs=[
            pl.BlockSpec((tm, d), lambda i, j: (i, 0)), vec, vec,
            pl.BlockSpec((None, d, tf), lambda i, j: (layer, 0, j)),
            pl.BlockSpec((None, tf, d), lambda i, j: (layer, j, 0)),
        ],
        out_specs=pl.BlockSpec((tm, d), lambda i, j: (i, 0)),
        scratch_shapes=[pltpu.VMEM((tm, d), BF16)],
        compiler_params=_cparams(("parallel", "arbitrary"), 56),
        name="ffn",
    )(x, g.reshape(1, d), out_gain.reshape(1, d), wu, wd)


def _outproj_kernel(x_ref, mix_ref, c_ref, w1_ref, w2_ref, o_ref):
    acc = jnp.dot(mix_ref[...].astype(BF16), w1_ref[...], preferred_element_type=F32)
    acc += jnp.dot(c_ref[...].astype(BF16), w2_ref[...], preferred_element_type=F32)
    o_ref[...] = x_ref[...] + acc


def _outproj(x, mix, c, w1, w2, tm):
    m, d = x.shape
    return pl.pallas_call(
        _outproj_kernel,
        out_shape=jax.ShapeDtypeStruct((m, d), F32),
        grid=(m // tm,),
        in_specs=[
            pl.BlockSpec((tm, d), lambda i: (i, 0)),
            pl.BlockSpec((tm, MIX), lambda i: (i, 0)),
            pl.BlockSpec((tm, MEM_W), lambda i: (i, 0)),
            pl.BlockSpec((MIX, d), lambda i: (0, 0)),
            pl.BlockSpec((MEM_W, d), lambda i: (0, 0)),
        ],
        out_specs=pl.BlockSpec((tm, d), lambda i: (i, 0)),
        compiler_params=_cparams(("parallel",), 48),
        name="outproj",
    )(x, mix, c, w1, w2)


def _mem_attend_kernel(head_rows, q_ref, k_ref, v_ref, o_ref):
    q = q_ref[0]
    heads = [slice(h * MEM_HD, (h + 1) * MEM_HD) for h in range(MEM_HEADS)]

    def head(ref, h):
        if head_rows:
            return ref[0, pl.ds(h, ref.shape[1] // MEM_HEADS, stride=MEM_HEADS), :].astype(BF16)
        return ref[0, :, heads[h]].astype(BF16)

    s = [lax.dot_general(q[:, sl].astype(BF16), head(k_ref, h), _NT, preferred_element_type=F32) * (MEM_HD ** -0.5)
         for h, sl in enumerate(heads)]
    e = [jnp.exp(x - jnp.max(x, axis=-1, keepdims=True)) for x in s]
    p = [x / jnp.sum(x, axis=-1, keepdims=True) for x in e]
    for h, sl in enumerate(heads):
        o_ref[0, :, sl] = jnp.dot(p[h].astype(BF16), head(v_ref, h), preferred_element_type=F32)


def _mem_attend(q, mk, mv, tt, first=0):
    b, t, _ = q.shape
    head_rows = mk.shape[2] == MEM_HD
    kv_spec = pl.BlockSpec((1,) + mk.shape[1:], lambda i, j: (first + i, 0, 0))
    return pl.pallas_call(
        functools.partial(_mem_attend_kernel, head_rows),
        out_shape=jax.ShapeDtypeStruct((b, t, MEM_W), F32),
        grid=(b, t // tt),
        in_specs=[pl.BlockSpec((1, tt, MEM_W), lambda i, j: (i, j, 0)), kv_spec, kv_spec],
        out_specs=pl.BlockSpec((1, tt, MEM_W), lambda i, j: (i, j, 0)),
        compiler_params=_cparams(("parallel", "parallel"), 32),
        name="mem_attend",
    )(q, mk, mv)


def _rwkv_prep_kernel(seq_len, tm,
                      pr_ref, pk_ref, pv_ref, pl_ref,
                      br_ref, bk_ref, bv_ref, bl_ref,
                      sr_ref, sk_ref, sv_ref, sl_ref,
                      mur_ref, muk_ref, muv_ref, mul_ref,
                      w0_ref, a0_ref, kk_ref, ka_ref, rk_ref,
                      w2_ref, a2_ref, g2_ref, ones_ref,
                      r_out, w_out, k_out, v_out, a_out, b_out, g_out, bonus_out):
    i = pl.program_id(0)
    row = lax.broadcasted_iota(jnp.int32, (tm, 1), 0)

    def shifted(cur_ref, before_ref, state_ref, mu_ref):
        cur = cur_ref[...]
        rolled = pltpu.roll(cur, 1, 0)
        if seq_len >= tm:
            first = jnp.where(i == 0, state_ref[0:1, :], before_ref[7:8, :])
            prev = jnp.where(row == 0, first, rolled)
        else:
            prev = jnp.where(row % seq_len == 0, state_ref[...], rolled)
        return cur + (prev - cur) * mu_ref[...]

    r = shifted(pr_ref, br_ref, sr_ref, mur_ref)
    k = shifted(pk_ref, bk_ref, sk_ref, muk_ref)
    v = shifted(pv_ref, bv_ref, sv_ref, muv_ref)
    lo = shifted(pl_ref, bl_ref, sl_ref, mul_ref)

    xw = jnp.tanh(lo[:, 0:128]).astype(BF16)
    xa = lo[:, 128:256].astype(BF16)
    xg = jax.nn.sigmoid(lo[:, 256:512]).astype(BF16)
    wl = w0_ref[...] + jnp.dot(xw, w2_ref[...], preferred_element_type=F32)
    w_log = -jax.nn.softplus(-wl) - 0.5
    log_decay = -jnp.exp(w_log)
    a = jax.nn.sigmoid(a0_ref[...] + jnp.dot(xa, a2_ref[...], preferred_element_type=F32))
    g = jnp.dot(xg, g2_ref[...], preferred_element_type=F32)

    ones = ones_ref[...]
    kk = k * kk_ref[...]
    nrm = jnp.sqrt(_dot_exact_rhs(kk * kk, ones))
    kk = kk / jnp.maximum(nrm, 1e-12)
    k2 = k * (1.0 + (a - 1.0) * ka_ref[...])
    bonus = _dot_exact_rhs(r * k2 * rk_ref[...], ones) * v

    r_out[...] = r
    w_out[...] = log_decay
    k_out[...] = k2
    v_out[...] = v
    a_out[...] = -kk
    b_out[...] = kk * a
    g_out[...] = g
    bonus_out[...] = bonus


def _rwkv_prep(z, shift_rows, seq_len, tm, mu, w0, a0, k_k, k_a, r_k, w2, a2, g2, ones):
    m = z.shape[0]
    nc = MIX // LANES
    lb = 3 * MIX // LORA_PAD
    tb = tm // 8
    srows = shift_rows.shape[0] if seq_len >= tm else tm

    def cur(off):
        return pl.BlockSpec((tm, LANES), lambda i, c: (i, off + c))

    def before(off):
        return pl.BlockSpec((8, LANES), lambda i, c: (jnp.maximum(i * tb - 1, 0), off + c))

    def state(off):
        if seq_len >= tm:
            return pl.BlockSpec((srows, LANES), lambda i, c: (0, off + c))
        return pl.BlockSpec((tm, LANES), lambda i, c: (i, off + c))

    def vec(off):
        return pl.BlockSpec((1, LANES), lambda i, c: (0, off + c))

    cur_l = pl.BlockSpec((tm, LORA_PAD), lambda i, c: (i, lb))
    before_l = pl.BlockSpec((8, LORA_PAD), lambda i, c: (jnp.maximum(i * tb - 1, 0), lb))
    if seq_len >= tm:
        state_l = pl.BlockSpec((srows, LORA_PAD), lambda i, c: (0, lb))
    else:
        state_l = pl.BlockSpec((tm, LORA_PAD), lambda i, c: (i, lb))
    vec_l = pl.BlockSpec((1, LORA_PAD), lambda i, c: (0, lb))

    out_spec = pl.BlockSpec((tm, LANES), lambda i, c: (i, c))
    out_sds = jax.ShapeDtypeStruct((m, MIX), F32)
    return pl.pallas_call(
        functools.partial(_rwkv_prep_kernel, seq_len, tm),
        out_shape=(out_sds,) * 8,
        grid=(m // tm, nc),
        in_specs=[
            cur(0), cur(nc), cur(2 * nc), cur_l,
            before(0), before(nc), before(2 * nc), before_l,
            state(0), state(nc), state(2 * nc), state_l,
            vec(0), vec(nc), vec(2 * nc), vec_l,
            vec(0), vec(0), vec(0), vec(0), vec(0),
            pl.BlockSpec((LANES, LANES), lambda i, c: (0, c)),
            pl.BlockSpec((LANES, LANES), lambda i, c: (0, c)),
            pl.BlockSpec((2 * LANES, LANES), lambda i, c: (0, c)),
            pl.BlockSpec((LANES, LANES), lambda i, c: (0, 0)),
        ],
        out_specs=(out_spec,) * 8,
        compiler_params=_cparams(("parallel", "parallel"), 48),
        name="rwkv_prep",
    )(z, z, z, z, z, z, z, z, shift_rows, shift_rows, shift_rows, shift_rows,
      mu, mu, mu, mu, w0, a0, k_k, k_a, r_k, w2, a2, g2, ones)


def _rwkv_scan_kernel(tb, r_ref, w_ref, k_ref, v_ref, a_ref, b_ref, s0_ref, y_ref, sf_ref, s_ref):
    tblk = pl.program_id(1)

    @pl.when(tblk == 0)
    def _():
        for p in range(N_PAIRS):
            s_ref[p] = jnp.concatenate([s0_ref[0, 2 * p], s0_ref[0, 2 * p + 1]], axis=1)

    lane = lax.broadcasted_iota(jnp.int32, (HEAD, LANES), 1)
    sub = lax.broadcasted_iota(jnp.int32, (HEAD, LANES), 0)
    lo = lane < HEAD
    diag = (lane % HEAD) == sub

    def head_sums(x):
        sa = jnp.sum(jnp.where(lo, x, 0.0), axis=1, keepdims=True)
        sb = jnp.sum(jnp.where(lo, 0.0, x), axis=1, keepdims=True)
        return jnp.where(lo, sa, sb)

    def step(t, carry):
        row = lambda ref: ref[0, pl.ds(t, 1), :]
        r_t, w_t, k_t, v_t, a_t, b_t = (row(x) for x in (r_ref, w_ref, k_ref, v_ref, a_ref, b_ref))
        w_t = jnp.exp(w_t)
        pairs = [slice(p * LANES, (p + 1) * LANES) for p in range(N_PAIRS)]
        s = [s_ref[p] for p in range(N_PAIRS)]
        sa = [head_sums(s[p] * a_t[:, sl]) for p, sl in enumerate(pairs)]
        vc = [head_sums(jnp.where(diag, v_t[:, sl], 0.0)) for sl in pairs]
        s = [s[p] * w_t[:, sl] + sa[p] * b_t[:, sl] + vc[p] * k_t[:, sl] for p, sl in enumerate(pairs)]
        for p in range(N_PAIRS):
            s_ref[p] = s[p]
        yc = [head_sums(s[p] * r_t[:, sl]) for p, sl in enumerate(pairs)]
        ys = [jnp.sum(jnp.where(diag, y, 0.0), axis=0, keepdims=True) for y in yc]
        y_ref[0, pl.ds(t, 1), :] = jnp.concatenate(ys, axis=1)
        return carry

    lax.fori_loop(0, tb, step, 0)

    @pl.when(tblk == pl.num_programs(1) - 1)
    def _():
        for p in range(N_PAIRS):
            sf_ref[0, 2 * p] = s_ref[p, :, :HEAD]
            sf_ref[0, 2 * p + 1] = s_ref[p, :, HEAD:]


def _rwkv_scan(r, w, k, v, a, b, s0, tb):
    bsz, t, _ = r.shape
    seq = pl.BlockSpec((1, tb, MIX), lambda i, j: (i, j, 0))
    st = pl.BlockSpec((1, N_HEADS, HEAD, HEAD), lambda i, j: (i, 0, 0, 0))
    return pl.pallas_call(
        functools.partial(_rwkv_scan_kernel, tb),
        out_shape=(jax.ShapeDtypeStruct((bsz, t, MIX), F32),
                   jax.ShapeDtypeStruct((bsz, N_HEADS, HEAD, HEAD), F32)),
        grid=(bsz, t // tb),
        in_specs=[seq] * 6 + [st],
        out_specs=(seq, st),
        scratch_shapes=[pltpu.VMEM((N_PAIRS, HEAD, LANES), F32)],
        compiler_params=_cparams(("parallel", "arbitrary"), 48),
        name="rwkv_scan",
    )(r, w, k, v, a, b, s0)


CHUNK = 64


def _rwkv_chunk_kernel(r_ref, w_ref, k_ref, v_ref, a_ref, b_ref, s0_ref, y_ref, sf_ref, s_ref):
    L = CHUNK
    assert 2 * L == LANES

    @pl.when(pl.program_id(0) == 0)
    def _():
        s_ref[...] = s0_ref[...]

    ri = lax.broadcasted_iota(jnp.int32, (L, L), 0)
    ci = lax.broadcasted_iota(jnp.int32, (L, L), 1)
    tri = (ci <= ri).astype(BF16)
    i2 = lax.broadcasted_iota(jnp.int32, (LANES, LANES), 0)
    j2 = lax.broadcasted_iota(jnp.int32, (LANES, LANES), 1)
    top, left = i2 < L, j2 < L
    s_idx, t_idx = j2 & (L - 1), i2 & (L - 1)
    is_top = jnp.where(top, 1, 0)
    keep_a = s_idx < t_idx + (1 - is_top)
    keep_b = s_idx < t_idx + is_top
    ul = top & left
    lr = jnp.logical_not(top | left)
    same_head = ul | lr
    eye = (i2 == j2).astype(F32)
    lo = lax.broadcasted_iota(jnp.int32, (1, LANES), 1) < HEAD
    zeros = jnp.zeros((L, LANES), F32)
    cat = lambda x, y: jnp.concatenate([x, y], axis=0)
    dot = lambda x, y: jnp.dot(x.astype(BF16), y.astype(BF16), preferred_element_type=F32)
    dot_nt = lambda x, y: lax.dot_general(x.astype(BF16), y.astype(BF16), _NT, preferred_element_type=F32)

    pairs = range(N_PAIRS)
    each = lambda f, *cols: [f(*xs) for xs in zip(*cols)]
    cols = [[x[:, p * LANES:(p + 1) * LANES] for p in pairs] for x in (r_ref, w_ref, k_ref, v_ref, a_ref, b_ref)]
    r, lw, k, v, a, b = cols

    def log_cumdecay(lw):
        hi, mid, low = _split3(lw)
        return dot(tri, hi) + dot(tri, mid) + dot(tri, low)

    c = each(log_cumdecay, lw)
    c_last = [x[L - 1:L, :] for x in c]
    at = each(lambda a, c, lw: a * jnp.exp(c - lw), a, c, lw)
    rt = each(lambda r, c: r * jnp.exp(c), r, c)
    em = [jnp.exp(-x) for x in c]
    kh, bh = each(jnp.multiply, k, em), each(jnp.multiply, b, em)
    kb = each(lambda k, b, c, cl: cat(k, b) * jnp.exp(cat(cl - c, cl - c)), k, b, c, c_last)
    vm_a, vm_b = [jnp.where(lo, x, 0.0) for x in v], [jnp.where(lo, 0.0, x) for x in v]
    s = [s_ref[p] for p in pairs]

    def state_products(at, rt, s):
        s_hi = s.astype(BF16)
        ar = cat(at, rt)
        return dot_nt(ar, s_hi) + dot_nt(ar, s - s_hi.astype(F32))

    ps = each(state_products, at, rt, s)
    m_a = each(lambda at, rt, kh, bh: jnp.where(keep_a, dot_nt(jnp.where(lo, cat(at, rt), 0.0), cat(bh, kh)), 0.0),
               at, rt, kh, bh)
    m_b = each(lambda at, rt, kh, bh: jnp.where(keep_b, dot_nt(jnp.where(lo, 0.0, cat(rt, at)), cat(kh, bh)), 0.0),
               at, rt, kh, bh)
    rhs = each(lambda ps, m_a, m_b, vm_a, vm_b: ps[:L] + dot(m_a[:L], cat(zeros, vm_a)) + dot(m_b[L:], cat(vm_b, zeros)),
               ps, m_a, m_b, vm_a, vm_b)

    qs = each(lambda m_a, m_b: jnp.where(ul, m_a, 0.0) + jnp.where(lr, m_b, 0.0), m_a, m_b)
    invs = [eye + q for q in qs]
    for _ in range(5):
        qs = [dot(q, q) for q in qs]
        invs = each(lambda inv, q: inv + dot(inv, q), invs, qs)

    u2 = each(lambda inv, rhs: dot(inv, cat(jnp.where(lo, rhs, 0.0), jnp.where(lo, 0.0, rhs))), invs, rhs)
    y = each(lambda ps, m_a, m_b, u2, vm_a, vm_b: ps[L:] + dot(m_a[L:], cat(u2[:L], vm_a)) + dot(m_b[:L], cat(vm_b, u2[L:])),
             ps, m_a, m_b, u2, vm_a, vm_b)
    for p in pairs:
        y_ref[:, p * LANES:(p + 1) * LANES] = y[p]
    upd = each(lambda v, u2, kb: dot(cat(v, u2[:L] + u2[L:]).T, kb), v, u2, kb)
    for p in pairs:
        s_ref[p] = s[p] * jnp.exp(c_last[p]) + jnp.where(same_head, upd[p], 0.0)

    @pl.when(pl.program_id(0) == pl.num_programs(0) - 1)
    def _():
        sf_ref[...] = s_ref[...]


def _rwkv_chunked(r, w, k, v, a, b, s0):
    t = r.shape[0]
    pair_eye = jnp.eye(2, dtype=F32)[None, :, None, :, None]
    s0p = (s0.reshape(N_PAIRS, 2, HEAD, 1, HEAD) * pair_eye).reshape(N_PAIRS, LANES, LANES)
    seq = pl.BlockSpec((CHUNK, MIX), lambda i: (i, 0))
    st = pl.BlockSpec((N_PAIRS, LANES, LANES), lambda i: (0, 0, 0))
    y, sf = pl.pallas_call(
        _rwkv_chunk_kernel,
        out_shape=(jax.ShapeDtypeStruct((t, MIX), F32), jax.ShapeDtypeStruct((N_PAIRS, LANES, LANES), F32)),
        grid=(t // CHUNK,),
        in_specs=[seq] * 6 + [st],
        out_specs=(seq, st),
        scratch_shapes=[pltpu.VMEM((N_PAIRS, LANES, LANES), F32)],
        compiler_params=_cparams(("arbitrary",), 48),
        name="rwkv_chunked",
    )(r, w, k, v, a, b, s0p)
    sf = sf.reshape(N_PAIRS, 2, HEAD, 2, HEAD)
    sf = jnp.stack([sf[:, 0, :, 0], sf[:, 1, :, 1]], axis=1).reshape(N_HEADS, HEAD, HEAD)
    return y, sf


def _rwkv_post_kernel(y_ref, bonus_ref, g_ref, lw_ref, lb_ref, ones_ref, o_ref):
    y = y_ref[...]
    ones = ones_ref[...]
    m = _dot_exact_rhs(y, ones) * (1.0 / HEAD)
    d = y - m
    var = _dot_exact_rhs(d * d, ones) * (1.0 / HEAD)
    yn = d * lax.rsqrt(var + GN_EPS) * lw_ref[...] + lb_ref[...]
    o_ref[...] = (yn + bonus_ref[...]) * g_ref[...]


def _rwkv_post(y, bonus, g, lnx_w, lnx_b, ones, tm):
    m = y.shape[0]
    blk = pl.BlockSpec((tm, LANES), lambda i, c: (i, c))
    vec = pl.BlockSpec((1, LANES), lambda i, c: (0, c))
    return pl.pallas_call(
        _rwkv_post_kernel,
        out_shape=jax.ShapeDtypeStruct((m, MIX), F32),
        grid=(m // tm, MIX // LANES),
        in_specs=[blk, blk, blk, vec, vec, pl.BlockSpec((LANES, LANES), lambda i, c: (0, 0))],
        out_specs=blk,
        compiler_params=_cparams(("parallel", "parallel"), 32),
        name="rwkv_post",
    )(y, bonus, g, lnx_w, lnx_b, ones)


def _sb_inside(z, lower, allowed, two_pass):
    neg_abs = pltpu.bitcast(pltpu.bitcast(z, jnp.uint32) | jnp.uint32(0x80000000), F32)
    sp = jnp.maximum(z, 0.0) + jnp.log(1.0 + jnp.exp(neg_abs))
    ls = z - sp
    if allowed is not None:
        sp = jnp.where(allowed, sp, 0.0)
    lhs = sp.astype(BF16)
    if two_pass:
        lhs = jnp.concatenate([lhs, (sp - lhs.astype(F32)).astype(BF16)], axis=1)
    right = jnp.dot(lhs, lower, preferred_element_type=F32)
    return ls, right, right[:, 0:1] + sp[:, 0:1]


def _sb_weights(scores, carry, allowed):
    ls, right, total = scores
    a = jnp.exp(ls - right - carry)
    if allowed is not None:
        a = jnp.where(allowed, a, 0.0)
    return a, carry + total


def _sb_tile(z, carry, lower, allowed, two_pass):
    return _sb_weights(_sb_inside(z, lower, allowed, two_pass), carry, allowed)


def _sb_prompt_kernel(tq, q_ref, k_ref, v_ref, up_ref, o_ref, kt_ref, vt_ref, kb_ref, va_ref, vb_ref, acc_ref,
                      z_ref, p_ref, c_ref):
    qi = pl.program_id(1)
    lane = lax.broadcasted_iota(jnp.int32, (1, LANES), 1)
    lo = lane < HEAD

    @pl.when(qi == 0)
    def _():
        kb_ref[...] = k_ref[...].astype(BF16)
        v = v_ref[...]
        va_ref[...] = jnp.where(lo, v, 0.0).astype(BF16)
        vb_ref[...] = jnp.where(lo, 0.0, v).astype(BF16)

        def to_cache_layout(g, carry):
            rows = pl.ds(pl.multiple_of(g * PAGE, PAGE), PAGE)
            kt_ref[g] = k_ref[rows, :].T
            vt_ref[g] = v_ref[rows, :].T
            return carry

        lax.fori_loop(0, kt_ref.shape[0], to_cache_layout, 0)

    q = q_ref[...] * (HEAD ** -0.5)
    qa = jnp.where(lo, q, 0.0).astype(BF16)
    qb = jnp.where(lo, 0.0, q).astype(BF16)
    lower = up_ref[...]

    def logits_to(j, slot):
        kblk = kb_ref[pl.ds(pl.multiple_of(jnp.maximum(j, 0) * tq, tq), tq), :]
        z_ref[slot, 0] = lax.dot_general(qa, kblk, _NT, preferred_element_type=F32)
        z_ref[slot, 1] = lax.dot_general(qb, kblk, _NT, preferred_element_type=F32)

    def accumulate(j, slot):
        ks = pl.multiple_of(j * tq, tq)
        acc_ref[...] += (jnp.dot(p_ref[slot, 0], va_ref[pl.ds(ks, tq), :], preferred_element_type=F32)
                         + jnp.dot(p_ref[slot, 1], vb_ref[pl.ds(ks, tq), :], preferred_element_type=F32))

    def weights_to(za, zb, slot, allowed):
        pa, ca = _sb_tile(za, c_ref[0], lower, allowed, False)
        pb, cb = _sb_tile(zb, c_ref[1], lower, allowed, False)
        p_ref[slot, 0] = pa.astype(BF16)
        p_ref[slot, 1] = pb.astype(BF16)
        c_ref[0] = ca
        c_ref[1] = cb

    def step(j, src, dst):
        accumulate(j + 1, src)
        za, zb = z_ref[src, 0], z_ref[src, 1]
        logits_to(j - 1, dst)
        weights_to(za, zb, dst, None)

    acc_ref[...] = jnp.zeros_like(acc_ref)
    c_ref[...] = jnp.zeros_like(c_ref)
    below_diag = (lax.broadcasted_iota(jnp.int32, (tq, tq), 1) < lax.broadcasted_iota(jnp.int32, (tq, tq), 0))
    logits_to(qi, 1)
    za, zb = z_ref[1, 0], z_ref[1, 1]
    logits_to(qi - 1, 0)
    weights_to(za, zb, 0, below_diag)

    odd = qi & 1

    @pl.when(odd == 1)
    def _():
        step(qi - 1, 0, 0)

    def pair(n, carry):
        j = qi - 1 - odd - 2 * n
        step(j, 0, 1)
        step(j - 1, 1, 0)
        return carry

    lax.fori_loop(0, qi >> 1, pair, 0)
    accumulate(0, 0)
    o_ref[...] = acc_ref[...]


def _sb_prompt(z, upper, tq):
    s = z.shape[0]
    nc = MIX // LANES
    return pl.pallas_call(
        functools.partial(_sb_prompt_kernel, tq),
        out_shape=(jax.ShapeDtypeStruct((s, MIX), F32),) + (jax.ShapeDtypeStruct((s // PAGE, MIX, PAGE), F32),) * 2,
        grid=(nc, s // tq),
        in_specs=[
            pl.BlockSpec((tq, LANES), lambda c, i: (i, c)),
            pl.BlockSpec((s, LANES), lambda c, i: (0, nc + c)),
            pl.BlockSpec((s, LANES), lambda c, i: (0, 2 * nc + c)),
            pl.BlockSpec((tq, tq), lambda c, i: (0, 0)),
        ],
        out_specs=(pl.BlockSpec((tq, LANES), lambda c, i: (i, c)),)
                  + (pl.BlockSpec((s // PAGE, LANES, PAGE), lambda c, i: (0, c, 0)),) * 2,
        scratch_shapes=[pltpu.VMEM((s, LANES), BF16)] * 3
                       + [pltpu.VMEM((tq, LANES), F32),
                          pltpu.VMEM((2, 2, tq, tq), F32),
                          pltpu.VMEM((2, 2, tq, tq), BF16),
                          pltpu.VMEM((2, tq, 1), F32)],
        compiler_params=_cparams(("parallel", "arbitrary"), 56),
        name="sb_prompt",
    )(z, z, z, upper)


QROWS = 8
SB_PAGE_GROUP = 8


def _sb_sample_kernel(group, pt_ref, q_ref, kn_ref, vn_ref, *refs):
    kc_refs, vc_refs = refs[:group], refs[group:2 * group]
    up_ref, o_ref, acc_ref, carry_ref = refs[2 * group:]
    p = pl.program_id(1)
    lower2 = up_ref[...]
    rows = N_HEADS * QROWS

    def tiles(k_refs, v_refs, carry, allowed):
        zs = [jnp.concatenate([jnp.dot(q_ref[0, h], k_ref[0, h].astype(BF16), preferred_element_type=F32)
                               for h in range(N_HEADS)], axis=0) for k_ref in k_refs]
        scores = [_sb_inside(z, lower2, allowed, True) for z in zs]
        total = None
        for score, v_ref in zip(scores, v_refs):
            a, carry = _sb_weights(score, carry, allowed)
            outs = [lax.dot_general(a[h * QROWS:(h + 1) * QROWS].astype(BF16), v_ref[0, h].astype(BF16),
                                    _NT, preferred_element_type=F32) for h in range(N_HEADS)]
            total = outs if total is None else [t + o for t, o in zip(total, outs)]
        return total, carry

    @pl.when(p == 0)
    def _():
        key = lax.broadcasted_iota(jnp.int32, (rows, PAGE), 1)
        qry = lax.broadcasted_iota(jnp.int32, (rows, PAGE), 0) & (QROWS - 1)
        outs, carry = tiles([kn_ref], [vn_ref], jnp.zeros((rows, 1), F32), key < qry)
        carry_ref[...] = carry
        for h in range(N_HEADS):
            acc_ref[h] = outs[h]

    @pl.when(p > 0)
    def _():
        outs, carry = tiles(kc_refs, vc_refs, carry_ref[...], None)
        carry_ref[...] = carry
        for h in range(N_HEADS):
            acc_ref[h] += outs[h]

    @pl.when(p == pl.num_programs(1) - 1)
    def _():
        o_ref[0] = acc_ref[...]


def _sb_sample(q, k_new, v_new, cache_k, cache_v, page_table, lower2):
    bsz = q.shape[0]
    n_pages = page_table.shape[1]
    group = next(g for g in (SB_PAGE_GROUP, 4, 2, 1) if n_pages % g == 0)

    def page(g):
        return lambda b, p, pt: (pt[b, n_pages - 1 - (jnp.maximum(p, 1) - 1) * group - g], 0, 0, 0)

    own = lambda b, p, pt: (b, 0, 0, 0)
    kv_block = (1, N_HEADS, HEAD, PAGE)
    grid_spec = pltpu.PrefetchScalarGridSpec(
        num_scalar_prefetch=1,
        grid=(bsz, n_pages // group + 1),
        in_specs=[pl.BlockSpec((1, N_HEADS, QROWS, HEAD), own), pl.BlockSpec(kv_block, own),
                  pl.BlockSpec(kv_block, own)]
                 + [pl.BlockSpec(kv_block, page(g)) for g in range(group)] * 2
                 + [pl.BlockSpec((2 * PAGE, PAGE), lambda b, p, pt: (0, 0))],
        out_specs=pl.BlockSpec((1, N_HEADS, QROWS, HEAD), own),
        scratch_shapes=[pltpu.VMEM((N_HEADS, QROWS, HEAD), F32), pltpu.VMEM((N_HEADS * QROWS, 1), F32)],
    )
    return pl.pallas_call(
        functools.partial(_sb_sample_kernel, group),
        out_shape=jax.ShapeDtypeStruct((bsz, N_HEADS, QROWS, HEAD), F32),
        grid_spec=grid_spec,
        compiler_params=_cparams(("parallel", "arbitrary"), 48),
        name="sb_sample",
    )(page_table, q, k_new, v_new, *([cache_k] * group), *([cache_v] * group), lower2)


def _pad_cols(x, width):
    return jnp.pad(x, ((0, 0), (0, width - x.shape[1])))


def _rwkv_cols(x):
    o = 3 * MIX
    parts = [x[:, :o],
             _pad_cols(x[:, o:o + DECAY_LORA], LANES),
             _pad_cols(x[:, o + DECAY_LORA:o + DECAY_LORA + ICLR_LORA], LANES),
             _pad_cols(x[:, o + DECAY_LORA + ICLR_LORA:RWKV_PROJ], 2 * LANES)]
    if x.shape[1] > RWKV_PROJ:
        parts.append(x[:, RWKV_PROJ:])
    return jnp.concatenate(parts, axis=1)


def _rwkv_uncols(x):
    o = 3 * MIX
    return jnp.concatenate([x[:, :o], x[:, o:o + DECAY_LORA], x[:, o + LANES:o + LANES + ICLR_LORA],
                            x[:, o + 2 * LANES:o + 2 * LANES + GATE_LORA]], axis=1)


def _pad_rows(x, rows):
    return jnp.pad(x, ((0, rows - x.shape[0]), (0, 0)))


def _tile(m, pref):
    return pref if m % pref == 0 else m


def kernel(x_prompt, x_sample, mem_prompt, state_rwkv_wkv, state_rwkv_shift, cache_sb_k, cache_sb_v, page_table, cache_mem_k, cache_mem_v, norm_mix, norm_ffn, norm_final, mem_norm, w_mem_kv, w_out, w_ffn_up, w_ffn_down, rwkv_w_in, rwkv_mu, rwkv_w0, rwkv_w2, rwkv_a0, rwkv_a2, rwkv_g2, rwkv_k_k, rwkv_k_a, rwkv_r_k, rwkv_lnx_w, rwkv_lnx_b, sb_w_in, sb_b_qk):
    bp, s, d = x_prompt.shape
    db, t_new, _ = x_sample.shape
    assert bp == 1, "the prompt path handles a single sequence"
    depth = norm_mix.shape[0]
    assert depth >= 1, "the final norm rides on the last layer's MLP"
    ms = db * t_new
    n_mem = mem_prompt.shape[1]
    xp = x_prompt.reshape(s, d)
    xs = x_sample.reshape(ms, d)
    tmp = _tile(s, 512)
    tms = ms
    tmp_in = _tile(s, 1024)
    col_tile = lambda n: 1024 if n % 1024 == 0 else 512

    idx = lax.broadcasted_iota(jnp.int32, (LANES, LANES), 0)
    jdx = lax.broadcasted_iota(jnp.int32, (LANES, LANES), 1)
    head_ones = ((idx // HEAD) == (jdx // HEAD)).astype(BF16)
    tq = _tile(s, 256)
    iq = lax.broadcasted_iota(jnp.int32, (tq, tq), 0)
    jq = lax.broadcasted_iota(jnp.int32, (tq, tq), 1)
    lower_q = (iq > jq).astype(BF16)
    lower_p = jnp.tile((idx > jdx).astype(BF16), (2, 1))

    wu_all, wd_all = w_ffn_up.astype(BF16), w_ffn_down.astype(BF16)
    cmk_rows = cache_mem_k.reshape(depth * db, n_mem * MEM_HEADS, MEM_HD)
    cmv_rows = cache_mem_v.reshape(depth * db, n_mem * MEM_HEADS, MEM_HD)
    mem_k, mem_v = [], []
    wkv_p, shift_p, wkv_s, shift_s = [], [], [], []
    sbk_p, sbv_p, sbk_s, sbv_s = [], [], [], []
    for i in range(depth):
        j = i // 2
        kv = _rms_matmul(mem_prompt.reshape(n_mem, d), mem_norm[i], w_mem_kv[i].astype(BF16),
                         jnp.zeros((2 * MEM_W,), F32), n_mem, 2 * MEM_W)
        mk, mv = kv[:, :MEM_W], kv[:, MEM_W:]
        mem_k.append(mk.reshape(1, n_mem, MEM_HEADS, MEM_HD))
        mem_v.append(mv.reshape(1, n_mem, MEM_HEADS, MEM_HD))
        if i % 2 == 0:
            w_in = _rwkv_cols(rwkv_w_in[j]).astype(BF16)
            n_in = w_in.shape[1]
            zero_b = jnp.zeros((n_in,), F32)
            zp = _rms_matmul(xp, norm_mix[i], w_in, zero_b, tmp_in, col_tile(n_in))
            zs = _rms_matmul(xs, norm_mix[i], w_in, zero_b, tms, col_tile(n_in))
            shift_p.append(_rwkv_uncols(zp[s - 1:s]))
            shift_s.append(_rwkv_uncols(zs.reshape(db, t_new, n_in)[:, t_new - 1]))
            wide = 3 * MIX + LORA_PAD
            mu = _rwkv_cols(rwkv_mu[j][None])
            vecs = [x.reshape(1, MIX) for x in (rwkv_w0[j], rwkv_a0[j], rwkv_k_k[j], rwkv_k_a[j], rwkv_r_k[j])]
            lora = (_pad_rows(rwkv_w2[j], LANES).astype(BF16), _pad_rows(rwkv_a2[j], LANES).astype(BF16),
                    _pad_rows(rwkv_g2[j], 2 * LANES).astype(BF16))
            lnx = (rwkv_lnx_w[j].reshape(1, MIX), rwkv_lnx_b[j].reshape(1, MIX))

            def rwkv(z, shift_rows, seq_len, tm, s0, nb, tb, tpost):
                r, w, k, v, a, b, g, bonus = _rwkv_prep(z, shift_rows, seq_len, tm, mu, *vecs, *lora, head_ones)
                if nb == 1 and seq_len % CHUNK == 0:
                    y, s_fin = _rwkv_chunked(r, w, k, v, a, b, s0[0])
                    s_fin = s_fin[None]
                else:
                    sh = lambda x: x.reshape(nb, seq_len, MIX)
                    y, s_fin = _rwkv_scan(sh(r), sh(w), sh(k), sh(v), sh(a), sh(b), s0, tb)
                return _rwkv_post(y.reshape(nb * seq_len, MIX), bonus, g, *lnx, head_ones, tpost), s_fin

            zero_shift = jnp.zeros((8, wide), F32)
            zero_state = jnp.zeros((1, N_HEADS, HEAD, HEAD), F32)
            mix_p, st_p = rwkv(zp, zero_shift, s, tmp_in, zero_state, 1, _tile(s, 256), tmp_in)
            shift_rows = jnp.repeat(_rwkv_cols(state_rwkv_shift[j]), t_new, axis=0)
            mix_s, st_s = rwkv(zs, shift_rows, t_new, tms, state_rwkv_wkv[j], db, t_new, tms)
            wkv_p.append(st_p)
            wkv_s.append(st_s)
            q_off = wide
        else:
            w_in = sb_w_in[j].astype(BF16)
            bias = jnp.concatenate([sb_b_qk[j], jnp.zeros((w_in.shape[1] - 2 * MIX,), F32)])
            zp = _rms_matmul(xp, norm_mix[i], w_in, bias, tmp_in, col_tile(w_in.shape[1]))
            zs = _rms_matmul(xs, norm_mix[i], w_in, bias, tms, col_tile(w_in.shape[1]))
            mix_p, kt_p, vt_p = _sb_prompt(zp, lower_q, tq)
            assert t_new < QROWS
            heads = lambda x: x.reshape(db, t_new, N_HEADS, HEAD)
            q_s, k_s, v_s = heads(zs[:, :MIX]), heads(zs[:, MIX:2 * MIX]), heads(zs[:, 2 * MIX:3 * MIX])
            q_hq = jnp.pad((q_s * (HEAD ** -0.5)).transpose(0, 2, 1, 3),
                           ((0, 0), (0, 0), (0, QROWS - t_new), (0, 0))).astype(BF16)
            keys_last = lambda x: x.transpose(0, 2, 3, 1)
            pad_new = lambda x: keys_last(jnp.pad(x, ((0, 0), (0, PAGE - t_new), (0, 0), (0, 0))))
            o_hq = _sb_sample(q_hq, pad_new(k_s), pad_new(v_s), keys_last(cache_sb_k[j]),
                              keys_last(cache_sb_v[j]), page_table, lower_p)
            mix_s = o_hq[:, :, :t_new].transpose(0, 2, 1, 3).reshape(ms, MIX)
            as_cache = lambda x: x.reshape(1, s // PAGE, N_HEADS, HEAD, PAGE).transpose(0, 1, 4, 2, 3)
            sbk_p.append(as_cache(kt_p))
            sbv_p.append(as_cache(vt_p))
            sbk_s.append(k_s)
            sbv_s.append(v_s)
            q_off = 3 * MIX
        cp = _mem_attend(zp[:, q_off:q_off + MEM_W][None], mk[None], mv[None], tmp)[0]
        cs = _mem_attend(zs[:, q_off:q_off + MEM_W].reshape(db, t_new, MEM_W), cmk_rows, cmv_rows, t_new,
                         first=i * db).reshape(ms, MEM_W)
        wo = w_out[i].astype(BF16)
        xp = _outproj(xp, mix_p, cp, wo[:MIX], wo[MIX:], tmp)
        xs = _outproj(xs, mix_s, cs, wo[:MIX], wo[MIX:], tms)
        last = i == depth - 1
        xp = _ffn(xp, norm_ffn[i], wu_all, wd_all, i, tmp_in, 512, norm_final, last)
        xs = _ffn(xs, norm_ffn[i], wu_all, wd_all, i, tms, col_tile(wu_all.shape[2]), norm_final, last)
    y_prompt = xp.reshape(1, s, d)
    y_sample = xs.reshape(db, t_new, d)
    return (y_prompt, y_sample, jnp.stack(mem_k), jnp.stack(mem_v),
            jnp.stack(wkv_p), jnp.stack(shift_p), jnp.stack(wkv_s), jnp.stack(shift_s),
            jnp.stack(sbk_p), jnp.stack(sbv_p), jnp.stack(sbk_s), jnp.stack(sbv_s))
```

```python
import functools

import jax
import jax.numpy as jnp
from jax import lax
from jax.experimental import pallas as pl
from jax.experimental.pallas import tpu as pltpu

F32 = jnp.float32
BF16 = jnp.bfloat16

HEAD = 64
N_HEADS = 24
MIX = N_HEADS * HEAD
N_PAIRS = N_HEADS // 2
LANES = 128
MEM_HEADS = 4
MEM_HD = 128
MEM_W = MEM_HEADS * MEM_HD
DECAY_LORA = 64
ICLR_LORA = 64
GATE_LORA = 224
LORA_PAD = 512
RWKV_PROJ = 3 * MIX + DECAY_LORA + ICLR_LORA + GATE_LORA
RMS_EPS = 1e-6
GN_EPS = 64e-5
PAGE = 128

_NT = (((1,), (1,)), ((), ()))


def _cparams(sem, vmem_mb):
    return pltpu.CompilerParams(dimension_semantics=sem, vmem_limit_bytes=vmem_mb << 20)


def _split3(x):
    hi = x.astype(BF16)
    r1 = x - hi.astype(F32)
    mid = r1.astype(BF16)
    lo = (r1 - mid.astype(F32)).astype(BF16)
    return hi, mid, lo


def _dot_exact_rhs(x, ones_bf16):
    hi, mid, lo = _split3(x)
    d = lambda a: jnp.dot(a, ones_bf16, preferred_element_type=F32)
    return d(hi) + d(mid) + d(lo)


def _rms_rows(x, g):
    ms = jnp.mean(x * x, axis=-1, keepdims=True)
    return x * lax.rsqrt(ms + RMS_EPS) * g


def _rms_matmul_kernel(x_ref, g_ref, w_ref, b_ref, o_ref, xn_ref):
    @pl.when(pl.program_id(1) == 0)
    def _():
        xn_ref[...] = _rms_rows(x_ref[...], g_ref[...]).astype(BF16)

    o_ref[...] = jnp.dot(xn_ref[...], w_ref[...], preferred_element_type=F32) + b_ref[...]


def _rms_matmul(x, g, w, b, tm, tn):
    m, d = x.shape
    n = w.shape[1]
    return pl.pallas_call(
        _rms_matmul_kernel,
        out_shape=jax.ShapeDtypeStruct((m, n), F32),
        grid=(m // tm, n // tn),
        in_specs=[
            pl.BlockSpec((tm, d), lambda i, j: (i, 0)),
            pl.BlockSpec((1, d), lambda i, j: (0, 0)),
            pl.BlockSpec((d, tn), lambda i, j: (0, j)),
            pl.BlockSpec((1, tn), lambda i, j: (0, j)),
        ],
        out_specs=pl.BlockSpec((tm, tn), lambda i, j: (i, j)),
        scratch_shapes=[pltpu.VMEM((tm, d), BF16)],
        compiler_params=_cparams(("parallel", "arbitrary"), 48),
        name="rms_matmul",
    )(x, g.reshape(1, d), w, b.reshape(1, n))


def _ffn_kernel(norm_out, x_ref, g_ref, go_ref, wu_ref, wd_ref, o_ref, xn_ref):
    @pl.when(pl.program_id(1) == 0)
    def _():
        x = x_ref[...]
        xn_ref[...] = _rms_rows(x, g_ref[...]).astype(BF16)
        o_ref[...] = x

    h = jnp.maximum(jnp.dot(xn_ref[...], wu_ref[...], preferred_element_type=F32), 0.0)
    o_ref[...] += jnp.dot((h * h).astype(BF16), wd_ref[...], preferred_element_type=F32)

    if norm_out:
        @pl.when(pl.program_id(1) == pl.num_programs(1) - 1)
        def _():
            o_ref[...] = _rms_rows(o_ref[...], go_ref[...])


def _ffn(x, g, wu, wd, layer, tm, tf, out_gain, norm_out):
    m, d = x.shape
    f = wu.shape[2]
    vec = pl.BlockSpec((1, d), lambda i, j: (0, 0))
    return pl.pallas_call(
        functools.partial(_ffn_kernel, norm_out),
        out_shape=jax.ShapeDtypeStruct((m, d), F32),
        grid=(m // tm, f // tf),
        in_specs=[
            pl.BlockSpec((tm, d), lambda i, j: (i, 0)), vec, vec,
            pl.BlockSpec((None, d, tf), lambda i, j: (layer, 0, j)),
            pl.BlockSpec((None, tf, d), lambda i, j: (layer, j, 0)),
        ],
        out_specs=pl.BlockSpec((tm, d), lambda i, j: (i, 0)),
        scratch_shapes=[pltpu.VMEM((tm, d), BF16)],
        compiler_params=_cparams(("parallel", "arbitrary"), 56),
        name="ffn",
    )(x, g.reshape(1, d), out_gain.reshape(1, d), wu, wd)


def _outproj_kernel(x_ref, mix_ref, c_ref, w1_ref, w2_ref, o_ref):
    acc = jnp.dot(mix_ref[...].astype(BF16), w1_ref[...], preferred_element_type=F32)
    acc += jnp.dot(c_ref[...].astype(BF16), w2_ref[...], preferred_element_type=F32)
    o_ref[...] = x_ref[...] + acc


def _outproj(x, mix, c, w1, w2, tm):
    m, d = x.shape
    return pl.pallas_call(
        _outproj_kernel,
        out_shape=jax.ShapeDtypeStruct((m, d), F32),
        grid=(m // tm,),
        in_specs=[
            pl.BlockSpec((tm, d), lambda i: (i, 0)),
            pl.BlockSpec((tm, MIX), lambda i: (i, 0)),
            pl.BlockSpec((tm, MEM_W), lambda i: (i, 0)),
            pl.BlockSpec((MIX, d), lambda i: (0, 0)),
            pl.BlockSpec((MEM_W, d), lambda i: (0, 0)),
        ],
        out_specs=pl.BlockSpec((tm, d), lambda i: (i, 0)),
        compiler_params=_cparams(("parallel",), 48),
        name="outproj",
    )(x, mix, c, w1, w2)


def _mem_attend_kernel(head_rows, q_ref, k_ref, v_ref, o_ref):
    q = q_ref[0]
    heads = [slice(h * MEM_HD, (h + 1) * MEM_HD) for h in range(MEM_HEADS)]

    def head(ref, h):
        if head_rows:
            return ref[0, pl.ds(h, ref.shape[1] // MEM_HEADS, stride=MEM_HEADS), :].astype(BF16)
        return ref[0, :, heads[h]].astype(BF16)

    s = [lax.dot_general(q[:, sl].astype(BF16), head(k_ref, h), _NT, preferred_element_type=F32) * (MEM_HD ** -0.5)
         for h, sl in enumerate(heads)]
    e = [jnp.exp(x - jnp.max(x, axis=-1, keepdims=True)) for x in s]
    p = [x / jnp.sum(x, axis=-1, keepdims=True) for x in e]
    for h, sl in enumerate(heads):
        o_ref[0, :, sl] = jnp.dot(p[h].astype(BF16), head(v_ref, h), preferred_element_type=F32)


def _mem_attend(q, mk, mv, tt, first=0):
    b, t, _ = q.shape
    head_rows = mk.shape[2] == MEM_HD
    kv_spec = pl.BlockSpec((1,) + mk.shape[1:], lambda i, j: (first + i, 0, 0))
    return pl.pallas_call(
        functools.partial(_mem_attend_kernel, head_rows),
        out_shape=jax.ShapeDtypeStruct((b, t, MEM_W), F32),
        grid=(b, t // tt),
        in_specs=[pl.BlockSpec((1, tt, MEM_W), lambda i, j: (i, j, 0)), kv_spec, kv_spec],
        out_specs=pl.BlockSpec((1, tt, MEM_W), lambda i, j: (i, j, 0)),
        compiler_params=_cparams(("parallel", "parallel"), 32),
        name="mem_attend",
    )(q, mk, mv)


def _rwkv_prep_kernel(seq_len, tm,
                      pr_ref, pk_ref, pv_ref, pl_ref,
                      br_ref, bk_ref, bv_ref, bl_ref,
                      sr_ref, sk_ref, sv_ref, sl_ref,
                      mur_ref, muk_ref, muv_ref, mul_ref,
                      w0_ref, a0_ref, kk_ref, ka_ref, rk_ref,
                      w2_ref, a2_ref, g2_ref, ones_ref,
                      r_out, w_out, k_out, v_out, a_out, b_out, g_out, bonus_out):
    i = pl.program_id(0)
    row = lax.broadcasted_iota(jnp.int32, (tm, 1), 0)

    def shifted(cur_ref, before_ref, state_ref, mu_ref):
        cur = cur_ref[...]
        rolled = pltpu.roll(cur, 1, 0)
        if seq_len >= tm:
            first = jnp.where(i == 0, state_ref[0:1, :], before_ref[7:8, :])
            prev = jnp.where(row == 0, first, rolled)
        else:
            prev = jnp.where(row % seq_len == 0, state_ref[...], rolled)
        return cur + (prev - cur) * mu_ref[...]

    r = shifted(pr_ref, br_ref, sr_ref, mur_ref)
    k = shifted(pk_ref, bk_ref, sk_ref, muk_ref)
    v = shifted(pv_ref, bv_ref, sv_ref, muv_ref)
    lo = shifted(pl_ref, bl_ref, sl_ref, mul_ref)

    xw = jnp.tanh(lo[:, 0:128]).astype(BF16)
    xa = lo[:, 128:256].astype(BF16)
    xg = jax.nn.sigmoid(lo[:, 256:512]).astype(BF16)
    wl = w0_ref[...] + jnp.dot(xw, w2_ref[...], preferred_element_type=F32)
    w_log = -jax.nn.softplus(-wl) - 0.5
    log_decay = -jnp.exp(w_log)
    a = jax.nn.sigmoid(a0_ref[...] + jnp.dot(xa, a2_ref[...], preferred_element_type=F32))
    g = jnp.dot(xg, g2_ref[...], preferred_element_type=F32)

    ones = ones_ref[...]
    kk = k * kk_ref[...]
    nrm = jnp.sqrt(_dot_exact_rhs(kk * kk, ones))
    kk = kk / jnp.maximum(nrm, 1e-12)
    k2 = k * (1.0 + (a - 1.0) * ka_ref[...])
    bonus = _dot_exact_rhs(r * k2 * rk_ref[...], ones) * v

    r_out[...] = r
    w_out[...] = log_decay
    k_out[...] = k2
    v_out[...] = v
    a_out[...] = -kk
    b_out[...] = kk * a
    g_out[...] = g
    bonus_out[...] = bonus


def _rwkv_prep(z, shift_rows, seq_len, tm, mu, w0, a0, k_k, k_a, r_k, w2, a2, g2, ones):
    m = z.shape[0]
    nc = MIX // LANES
    lb = 3 * MIX // LORA_PAD
    tb = tm // 8
    srows = shift_rows.shape[0] if seq_len >= tm else tm

    def cur(off):
        return pl.BlockSpec((tm, LANES), lambda i, c: (i, off + c))

    def before(off):
        return pl.BlockSpec((8, LANES), lambda i, c: (jnp.maximum(i * tb - 1, 0), off + c))

    def state(off):
        if seq_len >= tm:
            return pl.BlockSpec((srows, LANES), lambda i, c: (0, off + c))
        return pl.BlockSpec((tm, LANES), lambda i, c: (i, off + c))

    def vec(off):
        return pl.BlockSpec((1, LANES), lambda i, c: (0, off + c))

    cur_l = pl.BlockSpec((tm, LORA_PAD), lambda i, c: (i, lb))
    before_l = pl.BlockSpec((8, LORA_PAD), lambda i, c: (jnp.maximum(i * tb - 1, 0), lb))
    if seq_len >= tm:
        state_l = pl.BlockSpec((srows, LORA_PAD), lambda i, c: (0, lb))
    else:
        state_l = pl.BlockSpec((tm, LORA_PAD), lambda i, c: (i, lb))
    vec_l = pl.BlockSpec((1, LORA_PAD), lambda i, c: (0, lb))

    out_spec = pl.BlockSpec((tm, LANES), lambda i, c: (i, c))
    out_sds = jax.ShapeDtypeStruct((m, MIX), F32)
    return pl.pallas_call(
        functools.partial(_rwkv_prep_kernel, seq_len, tm),
        out_shape=(out_sds,) * 8,
        grid=(m // tm, nc),
        in_specs=[
            cur(0), cur(nc), cur(2 * nc), cur_l,
            before(0), before(nc), before(2 * nc), before_l,
            state(0), state(nc), state(2 * nc), state_l,
            vec(0), vec(nc), vec(2 * nc), vec_l,
            vec(0), vec(0), vec(0), vec(0), vec(0),
            pl.BlockSpec((LANES, LANES), lambda i, c: (0, c)),
            pl.BlockSpec((LANES, LANES), lambda i, c: (0, c)),
            pl.BlockSpec((2 * LANES, LANES), lambda i, c: (0, c)),
            pl.BlockSpec((LANES, LANES), lambda i, c: (0, 0)),
        ],
        out_specs=(out_spec,) * 8,
        compiler_params=_cparams(("parallel", "parallel"), 48),
        name="rwkv_prep",
    )(z, z, z, z, z, z, z, z, shift_rows, shift_rows, shift_rows, shift_rows,
      mu, mu, mu, mu, w0, a0, k_k, k_a, r_k, w2, a2, g2, ones)


def _rwkv_scan_kernel(tb, r_ref, w_ref, k_ref, v_ref, a_ref, b_ref, s0_ref, y_ref, sf_ref, s_ref):
    tblk = pl.program_id(1)

    @pl.when(tblk == 0)
    def _():
        for p in range(N_PAIRS):
            s_ref[p] = jnp.concatenate([s0_ref[0, 2 * p], s0_ref[0, 2 * p + 1]], axis=1)

    lane = lax.broadcasted_iota(jnp.int32, (HEAD, LANES), 1)
    sub = lax.broadcasted_iota(jnp.int32, (HEAD, LANES), 0)
    lo = lane < HEAD
    diag = (lane % HEAD) == sub

    def head_sums(x):
        sa = jnp.sum(jnp.where(lo, x, 0.0), axis=1, keepdims=True)
        sb = jnp.sum(jnp.where(lo, 0.0, x), axis=1, keepdims=True)
        return jnp.where(lo, sa, sb)

    def step(t, carry):
        row = lambda ref: ref[0, pl.ds(t, 1), :]
        r_t, w_t, k_t, v_t, a_t, b_t = (row(x) for x in (r_ref, w_ref, k_ref, v_ref, a_ref, b_ref))
        w_t = jnp.exp(w_t)
        pairs = [slice(p * LANES, (p + 1) * LANES) for p in range(N_PAIRS)]
        s = [s_ref[p] for p in range(N_PAIRS)]
        sa = [head_sums(s[p] * a_t[:, sl]) for p, sl in enumerate(pairs)]
        vc = [head_sums(jnp.where(diag, v_t[:, sl], 0.0)) for sl in pairs]
        s = [s[p] * w_t[:, sl] + sa[p] * b_t[:, sl] + vc[p] * k_t[:, sl] for p, sl in enumerate(pairs)]
        for p in range(N_PAIRS):
            s_ref[p] = s[p]
        yc = [head_sums(s[p] * r_t[:, sl]) for p, sl in enumerate(pairs)]
        ys = [jnp.sum(jnp.where(diag, y, 0.0), axis=0, keepdims=True) for y in yc]
        y_ref[0, pl.ds(t, 1), :] = jnp.concatenate(ys, axis=1)
        return carry

    lax.fori_loop(0, tb, step, 0)

    @pl.when(tblk == pl.num_programs(1) - 1)
    def _():
        for p in range(N_PAIRS):
            sf_ref[0, 2 * p] = s_ref[p, :, :HEAD]
            sf_ref[0, 2 * p + 1] = s_ref[p, :, HEAD:]


def _rwkv_scan(r, w, k, v, a, b, s0, tb):
    bsz, t, _ = r.shape
    seq = pl.BlockSpec((1, tb, MIX), lambda i, j: (i, j, 0))
    st = pl.BlockSpec((1, N_HEADS, HEAD, HEAD), lambda i, j: (i, 0, 0, 0))
    return pl.pallas_call(
        functools.partial(_rwkv_scan_kernel, tb),
        out_shape=(jax.ShapeDtypeStruct((bsz, t, MIX), F32),
                   jax.ShapeDtypeStruct((bsz, N_HEADS, HEAD, HEAD), F32)),
        grid=(bsz, t // tb),
        in_specs=[seq] * 6 + [st],
        out_specs=(seq, st),
        scratch_shapes=[pltpu.VMEM((N_PAIRS, HEAD, LANES), F32)],
        compiler_params=_cparams(("parallel", "arbitrary"), 48),
        name="rwkv_scan",
    )(r, w, k, v, a, b, s0)


CHUNK = 64


def _rwkv_chunk_kernel(r_ref, w_ref, k_ref, v_ref, a_ref, b_ref, s0_ref, y_ref, sf_ref, s_ref):
    L = CHUNK
    assert 2 * L == LANES

    @pl.when(pl.program_id(0) == 0)
    def _():
        s_ref[...] = s0_ref[...]

    ri = lax.broadcasted_iota(jnp.int32, (L, L), 0)
    ci = lax.broadcasted_iota(jnp.int32, (L, L), 1)
    tri = (ci <= ri).astype(BF16)
    i2 = lax.broadcasted_iota(jnp.int32, (LANES, LANES), 0)
    j2 = lax.broadcasted_iota(jnp.int32, (LANES, LANES), 1)
    top, left = i2 < L, j2 < L
    s_idx, t_idx = j2 & (L - 1), i2 & (L - 1)
    is_top = jnp.where(top, 1, 0)
    keep_a = s_idx < t_idx + (1 - is_top)
    keep_b = s_idx < t_idx + is_top
    ul = top & left
    lr = jnp.logical_not(top | left)
    same_head = ul | lr
    eye = (i2 == j2).astype(F32)
    lo = lax.broadcasted_iota(jnp.int32, (1, LANES), 1) < HEAD
    zeros = jnp.zeros((L, LANES), F32)
    cat = lambda x, y: jnp.concatenate([x, y], axis=0)
    dot = lambda x, y: jnp.dot(x.astype(BF16), y.astype(BF16), preferred_element_type=F32)
    dot_nt = lambda x, y: lax.dot_general(x.astype(BF16), y.astype(BF16), _NT, preferred_element_type=F32)

    pairs = range(N_PAIRS)
    each = lambda f, *cols: [f(*xs) for xs in zip(*cols)]
    cols = [[x[:, p * LANES:(p + 1) * LANES] for p in pairs] for x in (r_ref, w_ref, k_ref, v_ref, a_ref, b_ref)]
    r, lw, k, v, a, b = cols

    def log_cumdecay(lw):
        hi, mid, low = _split3(lw)
        return dot(tri, hi) + dot(tri, mid) + dot(tri, low)

    c = each(log_cumdecay, lw)
    c_last = [x[L - 1:L, :] for x in c]
    at = each(lambda a, c, lw: a * jnp.exp(c - lw), a, c, lw)
    rt = each(lambda r, c: r * jnp.exp(c), r, c)
    em = [jnp.exp(-x) for x in c]
    kh, bh = each(jnp.multiply, k, em), each(jnp.multiply, b, em)
    kb = each(lambda k, b, c, cl: cat(k, b) * jnp.exp(cat(cl - c, cl - c)), k, b, c, c_last)
    vm_a, vm_b = [jnp.where(lo, x, 0.0) for x in v], [jnp.where(lo, 0.0, x) for x in v]
    s = [s_ref[p] for p in pairs]

    def state_products(at, rt, s):
        s_hi = s.astype(BF16)
        ar = cat(at, rt)
        return dot_nt(ar, s_hi) + dot_nt(ar, s - s_hi.astype(F32))

    ps = each(state_products, at, rt, s)
    m_a = each(lambda at, rt, kh, bh: jnp.where(keep_a, dot_nt(jnp.where(lo, cat(at, rt), 0.0), cat(bh, kh)), 0.0),
               at, rt, kh, bh)
    m_b = each(lambda at, rt, kh, bh: jnp.where(keep_b, dot_nt(jnp.where(lo, 0.0, cat(rt, at)), cat(kh, bh)), 0.0),
               at, rt, kh, bh)
    rhs = each(lambda ps, m_a, m_b, vm_a, vm_b: ps[:L] + dot(m_a[:L], cat(zeros, vm_a)) + dot(m_b[L:], cat(vm_b, zeros)),
               ps, m_a, m_b, vm_a, vm_b)

    qs = each(lambda m_a, m_b: jnp.where(ul, m_a, 0.0) + jnp.where(lr, m_b, 0.0), m_a, m_b)
    invs = [eye + q for q in qs]
    for _ in range(5):
        qs = [dot(q, q) for q in qs]
        invs = each(lambda inv, q: inv + dot(inv, q), invs, qs)

    u2 = each(lambda inv, rhs: dot(inv, cat(jnp.where(lo, rhs, 0.0), jnp.where(lo, 0.0, rhs))), invs, rhs)
    y = each(lambda ps, m_a, m_b, u2, vm_a, vm_b: ps[L:] + dot(m_a[L:], cat(u2[:L], vm_a)) + dot(m_b[:L], cat(vm_b, u2[L:])),
             ps, m_a, m_b, u2, vm_a, vm_b)
    for p in pairs:
        y_ref[:, p * LANES:(p + 1) * LANES] = y[p]
    upd = each(lambda v, u2, kb: dot(cat(v, u2[:L] + u2[L:]).T, kb), v, u2, kb)
    for p in pairs:
        s_ref[p] = s[p] * jnp.exp(c_last[p]) + jnp.where(same_head, upd[p], 0.0)

    @pl.when(pl.program_id(0) == pl.num_programs(0) - 1)
    def _():
        sf_ref[...] = s_ref[...]


def _rwkv_chunked(r, w, k, v, a, b, s0):
    t = r.shape[0]
    pair_eye = jnp.eye(2, dtype=F32)[None, :, None, :, None]
    s0p = (s0.reshape(N_PAIRS, 2, HEAD, 1, HEAD) * pair_eye).reshape(N_PAIRS, LANES, LANES)
    seq = pl.BlockSpec((CHUNK, MIX), lambda i: (i, 0))
    st = pl.BlockSpec((N_PAIRS, LANES, LANES), lambda i: (0, 0, 0))
    y, sf = pl.pallas_call(
        _rwkv_chunk_kernel,
        out_shape=(jax.ShapeDtypeStruct((t, MIX), F32), jax.ShapeDtypeStruct((N_PAIRS, LANES, LANES), F32)),
        grid=(t // CHUNK,),
        in_specs=[seq] * 6 + [st],
        out_specs=(seq, st),
        scratch_shapes=[pltpu.VMEM((N_PAIRS, LANES, LANES), F32)],
        compiler_params=_cparams(("arbitrary",), 48),
        name="rwkv_chunked",
    )(r, w, k, v, a, b, s0p)
    sf = sf.reshape(N_PAIRS, 2, HEAD, 2, HEAD)
    sf = jnp.stack([sf[:, 0, :, 0], sf[:, 1, :, 1]], axis=1).reshape(N_HEADS, HEAD, HEAD)
    return y, sf


def _rwkv_post_kernel(y_ref, bonus_ref, g_ref, lw_ref, lb_ref, ones_ref, o_ref):
    y = y_ref[...]
    ones = ones_ref[...]
    m = _dot_exact_rhs(y, ones) * (1.0 / HEAD)
    d = y - m
    var = _dot_exact_rhs(d * d, ones) * (1.0 / HEAD)
    yn = d * lax.rsqrt(var + GN_EPS) * lw_ref[...] + lb_ref[...]
    o_ref[...] = (yn + bonus_ref[...]) * g_ref[...]


def _rwkv_post(y, bonus, g, lnx_w, lnx_b, ones, tm):
    m = y.shape[0]
    blk = pl.BlockSpec((tm, LANES), lambda i, c: (i, c))
    vec = pl.BlockSpec((1, LANES), lambda i, c: (0, c))
    return pl.pallas_call(
        _rwkv_post_kernel,
        out_shape=jax.ShapeDtypeStruct((m, MIX), F32),
        grid=(m // tm, MIX // LANES),
        in_specs=[blk, blk, blk, vec, vec, pl.BlockSpec((LANES, LANES), lambda i, c: (0, 0))],
        out_specs=blk,
        compiler_params=_cparams(("parallel", "parallel"), 32),
        name="rwkv_post",
    )(y, bonus, g, lnx_w, lnx_b, ones)


def _sb_inside(z, lower, allowed, two_pass):
    neg_abs = pltpu.bitcast(pltpu.bitcast(z, jnp.uint32) | jnp.uint32(0x80000000), F32)
    sp = jnp.maximum(z, 0.0) + jnp.log(1.0 + jnp.exp(neg_abs))
    ls = z - sp
    if allowed is not None:
        sp = jnp.where(allowed, sp, 0.0)
    lhs = sp.astype(BF16)
    if two_pass:
        lhs = jnp.concatenate([lhs, (sp - lhs.astype(F32)).astype(BF16)], axis=1)
    right = jnp.dot(lhs, lower, preferred_element_type=F32)
    return ls, right, right[:, 0:1] + sp[:, 0:1]


def _sb_weights(scores, carry, allowed):
    ls, right, total = scores
    a = jnp.exp(ls - right - carry)
    if allowed is not None:
        a = jnp.where(allowed, a, 0.0)
    return a, carry + total


def _sb_tile(z, carry, lower, allowed, two_pass):
    return _sb_weights(_sb_inside(z, lower, allowed, two_pass), carry, allowed)


def _sb_prompt_kernel(tq, q_ref, k_ref, v_ref, up_ref, o_ref, kt_ref, vt_ref, kb_ref, va_ref, vb_ref, acc_ref,
                      z_ref, p_ref, c_ref):
    qi = pl.program_id(1)
    lane = lax.broadcasted_iota(jnp.int32, (1, LANES), 1)
    lo = lane < HEAD

    @pl.when(qi == 0)
    def _():
        kb_ref[...] = k_ref[...].astype(BF16)
        v = v_ref[...]
        va_ref[...] = jnp.where(lo, v, 0.0).astype(BF16)
        vb_ref[...] = jnp.where(lo, 0.0, v).astype(BF16)

        def to_cache_layout(g, carry):
            rows = pl.ds(pl.multiple_of(g * PAGE, PAGE), PAGE)
            kt_ref[g] = k_ref[rows, :].T
            vt_ref[g] = v_ref[rows, :].T
            return carry

        lax.fori_loop(0, kt_ref.shape[0], to_cache_layout, 0)

    q = q_ref[...] * (HEAD ** -0.5)
    qa = jnp.where(lo, q, 0.0).astype(BF16)
    qb = jnp.where(lo, 0.0, q).astype(BF16)
    lower = up_ref[...]

    def logits_to(j, slot):
        kblk = kb_ref[pl.ds(pl.multiple_of(jnp.maximum(j, 0) * tq, tq), tq), :]
        z_ref[slot, 0] = lax.dot_general(qa, kblk, _NT, preferred_element_type=F32)
        z_ref[slot, 1] = lax.dot_general(qb, kblk, _NT, preferred_element_type=F32)

    def accumulate(j, slot):
        ks = pl.multiple_of(j * tq, tq)
        acc_ref[...] += (jnp.dot(p_ref[slot, 0], va_ref[pl.ds(ks, tq), :], preferred_element_type=F32)
                         + jnp.dot(p_ref[slot, 1], vb_ref[pl.ds(ks, tq), :], preferred_element_type=F32))

    def weights_to(za, zb, slot, allowed):
        pa, ca = _sb_tile(za, c_ref[0], lower, allowed, False)
        pb, cb = _sb_tile(zb, c_ref[1], lower, allowed, False)
        p_ref[slot, 0] = pa.astype(BF16)
        p_ref[slot, 1] = pb.astype(BF16)
        c_ref[0] = ca
        c_ref[1] = cb

    def step(j, src, dst):
        accumulate(j + 1, src)
        za, zb = z_ref[src, 0], z_ref[src, 1]
        logits_to(j - 1, dst)
        weights_to(za, zb, dst, None)

    acc_ref[...] = jnp.zeros_like(acc_ref)
    c_ref[...] = jnp.zeros_like(c_ref)
    below_diag = (lax.broadcasted_iota(jnp.int32, (tq, tq), 1) < lax.broadcasted_iota(jnp.int32, (tq, tq), 0))
    logits_to(qi, 1)
    za, zb = z_ref[1, 0], z_ref[1, 1]
    logits_to(qi - 1, 0)
    weights_to(za, zb, 0, below_diag)

    odd = qi & 1

    @pl.when(odd == 1)
    def _():
        step(qi - 1, 0, 0)

    def pair(n, carry):
        j = qi - 1 - odd - 2 * n
        step(j, 0, 1)
        step(j - 1, 1, 0)
        return carry

    lax.fori_loop(0, qi >> 1, pair, 0)
    accumulate(0, 0)
    o_ref[...] = acc_ref[...]


def _sb_prompt(z, upper, tq):
    s = z.shape[0]
    nc = MIX // LANES
    return pl.pallas_call(
        functools.partial(_sb_prompt_kernel, tq),
        out_shape=(jax.ShapeDtypeStruct((s, MIX), F32),) + (jax.ShapeDtypeStruct((s // PAGE, MIX, PAGE), F32),) * 2,
        grid=(nc, s // tq),
        in_specs=[
            pl.BlockSpec((tq, LANES), lambda c, i: (i, c)),
            pl.BlockSpec((s, LANES), lambda c, i: (0, nc + c)),
            pl.BlockSpec((s, LANES), lambda c, i: (0, 2 * nc + c)),
            pl.BlockSpec((tq, tq), lambda c, i: (0, 0)),
        ],
        out_specs=(pl.BlockSpec((tq, LANES), lambda c, i: (i, c)),)
                  + (pl.BlockSpec((s // PAGE, LANES, PAGE), lambda c, i: (0, c, 0)),) * 2,
        scratch_shapes=[pltpu.VMEM((s, LANES), BF16)] * 3
                       + [pltpu.VMEM((tq, LANES), F32),
                          pltpu.VMEM((2, 2, tq, tq), F32),
                          pltpu.VMEM((2, 2, tq, tq), BF16),
                          pltpu.VMEM((2, tq, 1), F32)],
        compiler_params=_cparams(("parallel", "arbitrary"), 56),
        name="sb_prompt",
    )(z, z, z, upper)


QROWS = 8
SB_PAGE_GROUP = 8


def _sb_sample_kernel(group, pt_ref, q_ref, kn_ref, vn_ref, *refs):
    kc_refs, vc_refs = refs[:group], refs[group:2 * group]
    up_ref, o_ref, acc_ref, carry_ref = refs[2 * group:]
    p = pl.program_id(1)
    lower2 = up_ref[...]
    rows = N_HEADS * QROWS

    def tiles(k_refs, v_refs, carry, allowed):
        zs = [jnp.concatenate([jnp.dot(q_ref[0, h], k_ref[0, h].astype(BF16), preferred_element_type=F32)
                               for h in range(N_HEADS)], axis=0) for k_ref in k_refs]
        scores = [_sb_inside(z, lower2, allowed, True) for z in zs]
        total = None
        for score, v_ref in zip(scores, v_refs):
            a, carry = _sb_weights(score, carry, allowed)
            outs = [lax.dot_general(a[h * QROWS:(h + 1) * QROWS].astype(BF16), v_ref[0, h].astype(BF16),
                                    _NT, preferred_element_type=F32) for h in range(N_HEADS)]
            total = outs if total is None else [t + o for t, o in zip(total, outs)]
        return total, carry

    @pl.when(p == 0)
    def _():
        key = lax.broadcasted_iota(jnp.int32, (rows, PAGE), 1)
        qry = lax.broadcasted_iota(jnp.int32, (rows, PAGE), 0) & (QROWS - 1)
        outs, carry = tiles([kn_ref], [vn_ref], jnp.zeros((rows, 1), F32), key < qry)
        carry_ref[...] = carry
        for h in range(N_HEADS):
            acc_ref[h] = outs[h]

    @pl.when(p > 0)
    def _():
        outs, carry = tiles(kc_refs, vc_refs, carry_ref[...], None)
        carry_ref[...] = carry
        for h in range(N_HEADS):
            acc_ref[h] += outs[h]

    @pl.when(p == pl.num_programs(1) - 1)
    def _():
        o_ref[0] = acc_ref[...]


def _sb_sample(q, k_new, v_new, cache_k, cache_v, page_table, lower2):
    bsz = q.shape[0]
    n_pages = page_table.shape[1]
    group = next(g for g in (SB_PAGE_GROUP, 4, 2, 1) if n_pages % g == 0)

    def page(g):
        return lambda b, p, pt: (pt[b, n_pages - 1 - (jnp.maximum(p, 1) - 1) * group - g], 0, 0, 0)

    own = lambda b, p, pt: (b, 0, 0, 0)
    kv_block = (1, N_HEADS, HEAD, PAGE)
    grid_spec = pltpu.PrefetchScalarGridSpec(
        num_scalar_prefetch=1,
        grid=(bsz, n_pages // group + 1),
        in_specs=[pl.BlockSpec((1, N_HEADS, QROWS, HEAD), own), pl.BlockSpec(kv_block, own),
                  pl.BlockSpec(kv_block, own)]
                 + [pl.BlockSpec(kv_block, page(g)) for g in range(group)] * 2
                 + [pl.BlockSpec((2 * PAGE, PAGE), lambda b, p, pt: (0, 0))],
        out_specs=pl.BlockSpec((1, N_HEADS, QROWS, HEAD), own),
        scratch_shapes=[pltpu.VMEM((N_HEADS, QROWS, HEAD), F32), pltpu.VMEM((N_HEADS * QROWS, 1), F32)],
    )
    return pl.pallas_call(
        functools.partial(_sb_sample_kernel, group),
        out_shape=jax.ShapeDtypeStruct((bsz, N_HEADS, QROWS, HEAD), F32),
        grid_spec=grid_spec,
        compiler_params=_cparams(("parallel", "arbitrary"), 48),
        name="sb_sample",
    )(page_table, q, k_new, v_new, *([cache_k] * group), *([cache_v] * group), lower2)


def _pad_cols(x, width):
    return jnp.pad(x, ((0, 0), (0, width - x.shape[1])))


def _rwkv_cols(x):
    o = 3 * MIX
    parts = [x[:, :o],
             _pad_cols(x[:, o:o + DECAY_LORA], LANES),
             _pad_cols(x[:, o + DECAY_LORA:o + DECAY_LORA + ICLR_LORA], LANES),
             _pad_cols(x[:, o + DECAY_LORA + ICLR_LORA:RWKV_PROJ], 2 * LANES)]
    if x.shape[1] > RWKV_PROJ:
        parts.append(x[:, RWKV_PROJ:])
    return jnp.concatenate(parts, axis=1)


def _rwkv_uncols(x):
    o = 3 * MIX
    return jnp.concatenate([x[:, :o], x[:, o:o + DECAY_LORA], x[:, o + LANES:o + LANES + ICLR_LORA],
                            x[:, o + 2 * LANES:o + 2 * LANES + GATE_LORA]], axis=1)


def _pad_rows(x, rows):
    return jnp.pad(x, ((0, rows - x.shape[0]), (0, 0)))


def _tile(m, pref):
    return pref if m % pref == 0 else m


def kernel(x_prompt, x_sample, mem_prompt, state_rwkv_wkv, state_rwkv_shift, cache_sb_k, cache_sb_v, page_table, cache_mem_k, cache_mem_v, norm_mix, norm_ffn, norm_final, mem_norm, w_mem_kv, w_out, w_ffn_up, w_ffn_down, rwkv_w_in, rwkv_mu, rwkv_w0, rwkv_w2, rwkv_a0, rwkv_a2, rwkv_g2, rwkv_k_k, rwkv_k_a, rwkv_r_k, rwkv_lnx_w, rwkv_lnx_b, sb_w_in, sb_b_qk):
    bp, s, d = x_prompt.shape
    db, t_new, _ = x_sample.shape
    assert bp == 1, "the prompt path handles a single sequence"
    depth = norm_mix.shape[0]
    assert depth >= 1, "the final norm rides on the last layer's MLP"
    ms = db * t_new
    n_mem = mem_prompt.shape[1]
    xp = x_prompt.reshape(s, d)
    xs = x_sample.reshape(ms, d)
    tmp = _tile(s, 512)
    tms = ms
    tmp_in = _tile(s, 1024)
    col_tile = lambda n: 1024 if n % 1024 == 0 else 512

    idx = lax.broadcasted_iota(jnp.int32, (LANES, LANES), 0)
    jdx = lax.broadcasted_iota(jnp.int32, (LANES, LANES), 1)
    head_ones = ((idx // HEAD) == (jdx // HEAD)).astype(BF16)
    tq = _tile(s, 256)
    iq = lax.broadcasted_iota(jnp.int32, (tq, tq), 0)
    jq = lax.broadcasted_iota(jnp.int32, (tq, tq), 1)
    lower_q = (iq > jq).astype(BF16)
    lower_p = jnp.tile((idx > jdx).astype(BF16), (2, 1))

    wu_all, wd_all = w_ffn_up.astype(BF16), w_ffn_down.astype(BF16)
    cmk_rows = cache_mem_k.reshape(depth * db, n_mem * MEM_HEADS, MEM_HD)
    cmv_rows = cache_mem_v.reshape(depth * db, n_mem * MEM_HEADS, MEM_HD)
    mem_k, mem_v = [], []
    wkv_p, shift_p, wkv_s, shift_s = [], [], [], []
    sbk_p, sbv_p, sbk_s, sbv_s = [], [], [], []
    for i in range(depth):
        j = i // 2
        kv = _rms_matmul(mem_prompt.reshape(n_mem, d), mem_norm[i], w_mem_kv[i].astype(BF16),
                         jnp.zeros((2 * MEM_W,), F32), n_mem, 2 * MEM_W)
        mk, mv = kv[:, :MEM_W], kv[:, MEM_W:]
        mem_k.append(mk.reshape(1, n_mem, MEM_HEADS, MEM_HD))
        mem_v.append(mv.reshape(1, n_mem, MEM_HEADS, MEM_HD))
        if i % 2 == 0:
            w_in = _rwkv_cols(rwkv_w_in[j]).astype(BF16)
            n_in = w_in.shape[1]
            zero_b = jnp.zeros((n_in,), F32)
            zp = _rms_matmul(xp, norm_mix[i], w_in, zero_b, tmp_in, col_tile(n_in))
            zs = _rms_matmul(xs, norm_mix[i], w_in, zero_b, tms, col_tile(n_in))
            shift_p.append(_rwkv_uncols(zp[s - 1:s]))
            shift_s.append(_rwkv_uncols(zs.reshape(db, t_new, n_in)[:, t_new - 1]))
            wide = 3 * MIX + LORA_PAD
            mu = _rwkv_cols(rwkv_mu[j][None])
            vecs = [x.reshape(1, MIX) for x in (rwkv_w0[j], rwkv_a0[j], rwkv_k_k[j], rwkv_k_a[j], rwkv_r_k[j])]
            lora = (_pad_rows(rwkv_w2[j], LANES).astype(BF16), _pad_rows(rwkv_a2[j], LANES).astype(BF16),
                    _pad_rows(rwkv_g2[j], 2 * LANES).astype(BF16))
            lnx = (rwkv_lnx_w[j].reshape(1, MIX), rwkv_lnx_b[j].reshape(1, MIX))

            def rwkv(z, shift_rows, seq_len, tm, s0, nb, tb, tpost):
                r, w, k, v, a, b, g, bonus = _rwkv_prep(z, shift_rows, seq_len, tm, mu, *vecs, *lora, head_ones)
                if nb == 1 and seq_len % CHUNK == 0:
                    y, s_fin = _rwkv_chunked(r, w, k, v, a, b, s0[0])
                    s_fin = s_fin[None]
                else:
                    sh = lambda x: x.reshape(nb, seq_len, MIX)
                    y, s_fin = _rwkv_scan(sh(r), sh(w), sh(k), sh(v), sh(a), sh(b), s0, tb)
                return _rwkv_post(y.reshape(nb * seq_len, MIX), bonus, g, *lnx, head_ones, tpost), s_fin

            zero_shift = jnp.zeros((8, wide), F32)
            zero_state = jnp.zeros((1, N_HEADS, HEAD, HEAD), F32)
            mix_p, st_p = rwkv(zp, zero_shift, s, tmp_in, zero_state, 1, _tile(s, 256), tmp_in)
            shift_rows = jnp.repeat(_rwkv_cols(state_rwkv_shift[j]), t_new, axis=0)
            mix_s, st_s = rwkv(zs, shift_rows, t_new, tms, state_rwkv_wkv[j], db, t_new, tms)
            wkv_p.append(st_p)
            wkv_s.append(st_s)
            q_off = wide
        else:
            w_in = sb_w_in[j].astype(BF16)
            bias = jnp.concatenate([sb_b_qk[j], jnp.zeros((w_in.shape[1] - 2 * MIX,), F32)])
            zp = _rms_matmul(xp, norm_mix[i], w_in, bias, tmp_in, col_tile(w_in.shape[1]))
            zs = _rms_matmul(xs, norm_mix[i], w_in, bias, tms, col_tile(w_in.shape[1]))
            mix_p, kt_p, vt_p = _sb_prompt(zp, lower_q, tq)
            assert t_new < QROWS
            heads = lambda x: x.reshape(db, t_new, N_HEADS, HEAD)
            q_s, k_s, v_s = heads(zs[:, :MIX]), heads(zs[:, MIX:2 * MIX]), heads(zs[:, 2 * MIX:3 * MIX])
            q_hq = jnp.pad((q_s * (HEAD ** -0.5)).transpose(0, 2, 1, 3),
                           ((0, 0), (0, 0), (0, QROWS - t_new), (0, 0))).astype(BF16)
            keys_last = lambda x: x.transpose(0, 2, 3, 1)
            pad_new = lambda x: keys_last(jnp.pad(x, ((0, 0), (0, PAGE - t_new), (0, 0), (0, 0))))
            o_hq = _sb_sample(q_hq, pad_new(k_s), pad_new(v_s), keys_last(cache_sb_k[j]),
                              keys_last(cache_sb_v[j]), page_table, lower_p)
            mix_s = o_hq[:, :, :t_new].transpose(0, 2, 1, 3).reshape(ms, MIX)
            as_cache = lambda x: x.reshape(1, s // PAGE, N_HEADS, HEAD, PAGE).transpose(0, 1, 4, 2, 3)
            sbk_p.append(as_cache(kt_p))
            sbv_p.append(as_cache(vt_p))
            sbk_s.append(k_s)
            sbv_s.append(v_s)
            q_off = 3 * MIX
        cp = _mem_attend(zp[:, q_off:q_off + MEM_W][None], mk[None], mv[None], tmp)[0]
        cs = _mem_attend(zs[:, q_off:q_off + MEM_W].reshape(db, t_new, MEM_W), cmk_rows, cmv_rows, t_new,
                         first=i * db).reshape(ms, MEM_W)
        wo = w_out[i].astype(BF16)
        xp = _outproj(xp, mix_p, cp, wo[:MIX], wo[MIX:], tmp)
        xs = _outproj(xs, mix_s, cs, wo[:MIX], wo[MIX:], tms)
        last = i == depth - 1
        xp = _ffn(xp, norm_ffn[i], wu_all, wd_all, i, tmp, col_tile(wu_all.shape[2]), norm_final, last)
        xs = _ffn(xs, norm_ffn[i], wu_all, wd_all, i, tms, col_tile(wu_all.shape[2]), norm_final, last)
    y_prompt = xp.reshape(1, s, d)
    y_sample = xs.reshape(db, t_new, d)
    return (y_prompt, y_sample, jnp.stack(mem_k), jnp.stack(mem_v),
            jnp.stack(wkv_p), jnp.stack(shift_p), jnp.stack(wkv_s), jnp.stack(shift_s),
            jnp.stack(sbk_p), jnp.stack(sbv_p), jnp.stack(sbk_s), jnp.stack(sbv_s))
```

```python
import functools

import jax
import jax.numpy as jnp
from jax import lax
from jax.experimental import pallas as pl
from jax.experimental.pallas import tpu as pltpu

F32 = jnp.float32
BF16 = jnp.bfloat16

HEAD = 64
N_HEADS = 24
MIX = N_HEADS * HEAD
N_PAIRS = N_HEADS // 2
LANES = 128
MEM_HEADS = 4
MEM_HD = 128
MEM_W = MEM_HEADS * MEM_HD
DECAY_LORA = 64
ICLR_LORA = 64
GATE_LORA = 224
LORA_PAD = 512
RWKV_PROJ = 3 * MIX + DECAY_LORA + ICLR_LORA + GATE_LORA
RMS_EPS = 1e-6
GN_EPS = 64e-5
PAGE = 128

_NT = (((1,), (1,)), ((), ()))


def _cparams(sem, vmem_mb):
    return pltpu.CompilerParams(dimension_semantics=sem, vmem_limit_bytes=vmem_mb << 20)


def _split3(x):
    hi = x.astype(BF16)
    r1 = x - hi.astype(F32)
    mid = r1.astype(BF16)
    lo = (r1 - mid.astype(F32)).astype(BF16)
    return hi, mid, lo


def _dot_exact_rhs(x, ones_bf16):
    hi, mid, lo = _split3(x)
    d = lambda a: jnp.dot(a, ones_bf16, preferred_element_type=F32)
    return d(hi) + d(mid) + d(lo)


def _rms_rows(x, g):
    ms = jnp.mean(x * x, axis=-1, keepdims=True)
    return x * lax.rsqrt(ms + RMS_EPS) * g


def _rms_matmul_kernel(x_ref, g_ref, w_ref, b_ref, o_ref, xn_ref):
    @pl.when(pl.program_id(1) == 0)
    def _():
        xn_ref[...] = _rms_rows(x_ref[...], g_ref[...]).astype(BF16)

    o_ref[...] = jnp.dot(xn_ref[...], w_ref[...], preferred_element_type=F32) + b_ref[...]


def _rms_matmul(x, g, w, b, tm, tn):
    m, d = x.shape
    n = w.shape[1]
    return pl.pallas_call(
        _rms_matmul_kernel,
        out_shape=jax.ShapeDtypeStruct((m, n), F32),
        grid=(m // tm, n // tn),
        in_specs=[
            pl.BlockSpec((tm, d), lambda i, j: (i, 0)),
            pl.BlockSpec((1, d), lambda i, j: (0, 0)),
            pl.BlockSpec((d, tn), lambda i, j: (0, j)),
            pl.BlockSpec((1, tn), lambda i, j: (0, j)),
        ],
        out_specs=pl.BlockSpec((tm, tn), lambda i, j: (i, j)),
        scratch_shapes=[pltpu.VMEM((tm, d), BF16)],
        compiler_params=_cparams(("parallel", "arbitrary"), 48),
        name="rms_matmul",
    )(x, g.reshape(1, d), w, b.reshape(1, n))


def _ffn_kernel(x_ref, g_ref, wu_ref, wd_ref, o_ref, xn_ref):
    @pl.when(pl.program_id(1) == 0)
    def _():
        x = x_ref[...]
        xn_ref[...] = _rms_rows(x, g_ref[...]).astype(BF16)
        o_ref[...] = x

    h = jnp.maximum(jnp.dot(xn_ref[...], wu_ref[...], preferred_element_type=F32), 0.0)
    o_ref[...] += jnp.dot((h * h).astype(BF16), wd_ref[...], preferred_element_type=F32)


def _ffn(x, g, wu, wd, layer, tm, tf):
    m, d = x.shape
    f = wu.shape[2]
    return pl.pallas_call(
        _ffn_kernel,
        out_shape=jax.ShapeDtypeStruct((m, d), F32),
        grid=(m // tm, f // tf),
        in_specs=[
            pl.BlockSpec((tm, d), lambda i, j: (i, 0)),
            pl.BlockSpec((1, d), lambda i, j: (0, 0)),
            pl.BlockSpec((None, d, tf), lambda i, j: (layer, 0, j)),
            pl.BlockSpec((None, tf, d), lambda i, j: (layer, j, 0)),
        ],
        out_specs=pl.BlockSpec((tm, d), lambda i, j: (i, 0)),
        scratch_shapes=[pltpu.VMEM((tm, d), BF16)],
        compiler_params=_cparams(("parallel", "arbitrary"), 48),
        name="ffn",
    )(x, g.reshape(1, d), wu, wd)


def _outproj_kernel(x_ref, mix_ref, c_ref, w1_ref, w2_ref, o_ref):
    acc = jnp.dot(mix_ref[...].astype(BF16), w1_ref[...], preferred_element_type=F32)
    acc += jnp.dot(c_ref[...].astype(BF16), w2_ref[...], preferred_element_type=F32)
    o_ref[...] = x_ref[...] + acc


def _outproj(x, mix, c, w1, w2, tm):
    m, d = x.shape
    return pl.pallas_call(
        _outproj_kernel,
        out_shape=jax.ShapeDtypeStruct((m, d), F32),
        grid=(m // tm,),
        in_specs=[
            pl.BlockSpec((tm, d), lambda i: (i, 0)),
            pl.BlockSpec((tm, MIX), lambda i: (i, 0)),
            pl.BlockSpec((tm, MEM_W), lambda i: (i, 0)),
            pl.BlockSpec((MIX, d), lambda i: (0, 0)),
            pl.BlockSpec((MEM_W, d), lambda i: (0, 0)),
        ],
        out_specs=pl.BlockSpec((tm, d), lambda i: (i, 0)),
        compiler_params=_cparams(("parallel",), 48),
        name="outproj",
    )(x, mix, c, w1, w2)


def _mem_attend_kernel(head_rows, q_ref, k_ref, v_ref, o_ref):
    q = q_ref[0]
    heads = [slice(h * MEM_HD, (h + 1) * MEM_HD) for h in range(MEM_HEADS)]

    def head(ref, h):
        if head_rows:
            return ref[0, pl.ds(h, ref.shape[1] // MEM_HEADS, stride=MEM_HEADS), :].astype(BF16)
        return ref[0, :, heads[h]].astype(BF16)

    s = [lax.dot_general(q[:, sl].astype(BF16), head(k_ref, h), _NT, preferred_element_type=F32) * (MEM_HD ** -0.5)
         for h, sl in enumerate(heads)]
    e = [jnp.exp(x - jnp.max(x, axis=-1, keepdims=True)) for x in s]
    p = [x / jnp.sum(x, axis=-1, keepdims=True) for x in e]
    for h, sl in enumerate(heads):
        o_ref[0, :, sl] = jnp.dot(p[h].astype(BF16), head(v_ref, h), preferred_element_type=F32)


def _mem_attend(q, mk, mv, tt, first=0):
    b, t, _ = q.shape
    head_rows = mk.shape[2] == MEM_HD
    kv_spec = pl.BlockSpec((1,) + mk.shape[1:], lambda i, j: (first + i, 0, 0))
    return pl.pallas_call(
        functools.partial(_mem_attend_kernel, head_rows),
        out_shape=jax.ShapeDtypeStruct((b, t, MEM_W), F32),
        grid=(b, t // tt),
        in_specs=[pl.BlockSpec((1, tt, MEM_W), lambda i, j: (i, j, 0)), kv_spec, kv_spec],
        out_specs=pl.BlockSpec((1, tt, MEM_W), lambda i, j: (i, j, 0)),
        compiler_params=_cparams(("parallel", "parallel"), 32),
        name="mem_attend",
    )(q, mk, mv)


def _rwkv_prep_kernel(seq_len, tm,
                      pr_ref, pk_ref, pv_ref, pl_ref,
                      br_ref, bk_ref, bv_ref, bl_ref,
                      sr_ref, sk_ref, sv_ref, sl_ref,
                      mur_ref, muk_ref, muv_ref, mul_ref,
                      w0_ref, a0_ref, kk_ref, ka_ref, rk_ref,
                      w2_ref, a2_ref, g2_ref, ones_ref,
                      r_out, w_out, k_out, v_out, a_out, b_out, g_out, bonus_out):
    i = pl.program_id(0)
    row = lax.broadcasted_iota(jnp.int32, (tm, 1), 0)

    def shifted(cur_ref, before_ref, state_ref, mu_ref):
        cur = cur_ref[...]
        rolled = pltpu.roll(cur, 1, 0)
        if seq_len >= tm:
            first = jnp.where(i == 0, state_ref[0:1, :], before_ref[7:8, :])
            prev = jnp.where(row == 0, first, rolled)
        else:
            prev = jnp.where(row % seq_len == 0, state_ref[...], rolled)
        return cur + (prev - cur) * mu_ref[...]

    r = shifted(pr_ref, br_ref, sr_ref, mur_ref)
    k = shifted(pk_ref, bk_ref, sk_ref, muk_ref)
    v = shifted(pv_ref, bv_ref, sv_ref, muv_ref)
    lo = shifted(pl_ref, bl_ref, sl_ref, mul_ref)

    xw = jnp.tanh(lo[:, 0:128]).astype(BF16)
    xa = lo[:, 128:256].astype(BF16)
    xg = jax.nn.sigmoid(lo[:, 256:512]).astype(BF16)
    wl = w0_ref[...] + jnp.dot(xw, w2_ref[...], preferred_element_type=F32)
    w_log = -jax.nn.softplus(-wl) - 0.5
    log_decay = -jnp.exp(w_log)
    a = jax.nn.sigmoid(a0_ref[...] + jnp.dot(xa, a2_ref[...], preferred_element_type=F32))
    g = jnp.dot(xg, g2_ref[...], preferred_element_type=F32)

    ones = ones_ref[...]
    kk = k * kk_ref[...]
    nrm = jnp.sqrt(_dot_exact_rhs(kk * kk, ones))
    kk = kk / jnp.maximum(nrm, 1e-12)
    k2 = k * (1.0 + (a - 1.0) * ka_ref[...])
    bonus = _dot_exact_rhs(r * k2 * rk_ref[...], ones) * v

    r_out[...] = r
    w_out[...] = log_decay
    k_out[...] = k2
    v_out[...] = v
    a_out[...] = -kk
    b_out[...] = kk * a
    g_out[...] = g
    bonus_out[...] = bonus


def _rwkv_prep(z, shift_rows, seq_len, tm, mu, w0, a0, k_k, k_a, r_k, w2, a2, g2, ones):
    m = z.shape[0]
    nc = MIX // LANES
    lb = 3 * MIX // LORA_PAD
    tb = tm // 8
    srows = shift_rows.shape[0] if seq_len >= tm else tm

    def cur(off):
        return pl.BlockSpec((tm, LANES), lambda i, c: (i, off + c))

    def before(off):
        return pl.BlockSpec((8, LANES), lambda i, c: (jnp.maximum(i * tb - 1, 0), off + c))

    def state(off):
        if seq_len >= tm:
            return pl.BlockSpec((srows, LANES), lambda i, c: (0, off + c))
        return pl.BlockSpec((tm, LANES), lambda i, c: (i, off + c))

    def vec(off):
        return pl.BlockSpec((1, LANES), lambda i, c: (0, off + c))

    cur_l = pl.BlockSpec((tm, LORA_PAD), lambda i, c: (i, lb))
    before_l = pl.BlockSpec((8, LORA_PAD), lambda i, c: (jnp.maximum(i * tb - 1, 0), lb))
    if seq_len >= tm:
        state_l = pl.BlockSpec((srows, LORA_PAD), lambda i, c: (0, lb))
    else:
        state_l = pl.BlockSpec((tm, LORA_PAD), lambda i, c: (i, lb))
    vec_l = pl.BlockSpec((1, LORA_PAD), lambda i, c: (0, lb))

    out_spec = pl.BlockSpec((tm, LANES), lambda i, c: (i, c))
    out_sds = jax.ShapeDtypeStruct((m, MIX), F32)
    return pl.pallas_call(
        functools.partial(_rwkv_prep_kernel, seq_len, tm),
        out_shape=(out_sds,) * 8,
        grid=(m // tm, nc),
        in_specs=[
            cur(0), cur(nc), cur(2 * nc), cur_l,
            before(0), before(nc), before(2 * nc), before_l,
            state(0), state(nc), state(2 * nc), state_l,
            vec(0), vec(nc), vec(2 * nc), vec_l,
            vec(0), vec(0), vec(0), vec(0), vec(0),
            pl.BlockSpec((LANES, LANES), lambda i, c: (0, c)),
            pl.BlockSpec((LANES, LANES), lambda i, c: (0, c)),
            pl.BlockSpec((2 * LANES, LANES), lambda i, c: (0, c)),
            pl.BlockSpec((LANES, LANES), lambda i, c: (0, 0)),
        ],
        out_specs=(out_spec,) * 8,
        compiler_params=_cparams(("parallel", "parallel"), 48),
        name="rwkv_prep",
    )(z, z, z, z, z, z, z, z, shift_rows, shift_rows, shift_rows, shift_rows,
      mu, mu, mu, mu, w0, a0, k_k, k_a, r_k, w2, a2, g2, ones)


def _rwkv_scan_kernel(tb, r_ref, w_ref, k_ref, v_ref, a_ref, b_ref, s0_ref, y_ref, sf_ref, s_ref):
    tblk = pl.program_id(1)

    @pl.when(tblk == 0)
    def _():
        for p in range(N_PAIRS):
            s_ref[p] = jnp.concatenate([s0_ref[0, 2 * p], s0_ref[0, 2 * p + 1]], axis=1)

    lane = lax.broadcasted_iota(jnp.int32, (HEAD, LANES), 1)
    sub = lax.broadcasted_iota(jnp.int32, (HEAD, LANES), 0)
    lo = lane < HEAD
    diag = (lane % HEAD) == sub

    def head_sums(x):
        sa = jnp.sum(jnp.where(lo, x, 0.0), axis=1, keepdims=True)
        sb = jnp.sum(jnp.where(lo, 0.0, x), axis=1, keepdims=True)
        return jnp.where(lo, sa, sb)

    def step(t, carry):
        row = lambda ref: ref[0, pl.ds(t, 1), :]
        r_t, w_t, k_t, v_t, a_t, b_t = (row(x) for x in (r_ref, w_ref, k_ref, v_ref, a_ref, b_ref))
        w_t = jnp.exp(w_t)
        pairs = [slice(p * LANES, (p + 1) * LANES) for p in range(N_PAIRS)]
        s = [s_ref[p] for p in range(N_PAIRS)]
        sa = [head_sums(s[p] * a_t[:, sl]) for p, sl in enumerate(pairs)]
        vc = [head_sums(jnp.where(diag, v_t[:, sl], 0.0)) for sl in pairs]
        s = [s[p] * w_t[:, sl] + sa[p] * b_t[:, sl] + vc[p] * k_t[:, sl] for p, sl in enumerate(pairs)]
        for p in range(N_PAIRS):
            s_ref[p] = s[p]
        yc = [head_sums(s[p] * r_t[:, sl]) for p, sl in enumerate(pairs)]
        ys = [jnp.sum(jnp.where(diag, y, 0.0), axis=0, keepdims=True) for y in yc]
        y_ref[0, pl.ds(t, 1), :] = jnp.concatenate(ys, axis=1)
        return carry

    lax.fori_loop(0, tb, step, 0)

    @pl.when(tblk == pl.num_programs(1) - 1)
    def _():
        for p in range(N_PAIRS):
            sf_ref[0, 2 * p] = s_ref[p, :, :HEAD]
            sf_ref[0, 2 * p + 1] = s_ref[p, :, HEAD:]


def _rwkv_scan(r, w, k, v, a, b, s0, tb):
    bsz, t, _ = r.shape
    seq = pl.BlockSpec((1, tb, MIX), lambda i, j: (i, j, 0))
    st = pl.BlockSpec((1, N_HEADS, HEAD, HEAD), lambda i, j: (i, 0, 0, 0))
    return pl.pallas_call(
        functools.partial(_rwkv_scan_kernel, tb),
        out_shape=(jax.ShapeDtypeStruct((bsz, t, MIX), F32),
                   jax.ShapeDtypeStruct((bsz, N_HEADS, HEAD, HEAD), F32)),
        grid=(bsz, t // tb),
        in_specs=[seq] * 6 + [st],
        out_specs=(seq, st),
        scratch_shapes=[pltpu.VMEM((N_PAIRS, HEAD, LANES), F32)],
        compiler_params=_cparams(("parallel", "arbitrary"), 48),
        name="rwkv_scan",
    )(r, w, k, v, a, b, s0)


CHUNK = 64
PAIR_GROUP = 12


def _rwkv_chunk_kernel(r_ref, w_ref, k_ref, v_ref, a_ref, b_ref, s0_ref, y_ref, sf_ref, s_ref):
    L = CHUNK
    assert 2 * L == LANES

    @pl.when(pl.program_id(0) == 0)
    def _():
        s_ref[...] = s0_ref[...]

    ri = lax.broadcasted_iota(jnp.int32, (L, L), 0)
    ci = lax.broadcasted_iota(jnp.int32, (L, L), 1)
    tri = (ci <= ri).astype(BF16)
    i2 = lax.broadcasted_iota(jnp.int32, (LANES, LANES), 0)
    j2 = lax.broadcasted_iota(jnp.int32, (LANES, LANES), 1)
    top, left = i2 < L, j2 < L
    s_idx, t_idx = j2 & (L - 1), i2 & (L - 1)
    is_top = jnp.where(top, 1, 0)
    keep_a = s_idx < t_idx + (1 - is_top)
    keep_b = s_idx < t_idx + is_top
    ul = top & left
    lr = jnp.logical_not(top | left)
    same_head = ul | lr
    eye = (i2 == j2).astype(F32)
    lo = lax.broadcasted_iota(jnp.int32, (1, LANES), 1) < HEAD
    zeros = jnp.zeros((L, LANES), F32)
    cat = lambda x, y: jnp.concatenate([x, y], axis=0)
    dot = lambda x, y: jnp.dot(x.astype(BF16), y.astype(BF16), preferred_element_type=F32)
    dot_nt = lambda x, y: lax.dot_general(x.astype(BF16), y.astype(BF16), _NT, preferred_element_type=F32)

    pairs = range(N_PAIRS)
    each = lambda f, *cols: [f(*xs) for xs in zip(*cols)]
    cols = [[x[:, p * LANES:(p + 1) * LANES] for p in pairs] for x in (r_ref, w_ref, k_ref, v_ref, a_ref, b_ref)]
    r, lw, k, v, a, b = cols

    def log_cumdecay(lw):
        hi, mid, low = _split3(lw)
        return dot(tri, hi) + dot(tri, mid) + dot(tri, low)

    c = each(log_cumdecay, lw)
    c_last = [x[L - 1:L, :] for x in c]
    at = each(lambda a, c, lw: a * jnp.exp(c - lw), a, c, lw)
    rt = each(lambda r, c: r * jnp.exp(c), r, c)
    em = [jnp.exp(-x) for x in c]
    kh, bh = each(jnp.multiply, k, em), each(jnp.multiply, b, em)
    kb = each(lambda k, b, c, cl: cat(k, b) * jnp.exp(cat(cl - c, cl - c)), k, b, c, c_last)
    vm_a, vm_b = [jnp.where(lo, x, 0.0) for x in v], [jnp.where(lo, 0.0, x) for x in v]
    s = [s_ref[p] for p in pairs]

    def state_products(at, rt, s):
        s_hi = s.astype(BF16)
        ar = cat(at, rt)
        return dot_nt(ar, s_hi) + dot_nt(ar, s - s_hi.astype(F32))

    ps = each(state_products, at, rt, s)
    m_a = each(lambda at, rt, kh, bh: jnp.where(keep_a, dot_nt(jnp.where(lo, cat(at, rt), 0.0), cat(bh, kh)), 0.0),
               at, rt, kh, bh)
    m_b = each(lambda at, rt, kh, bh: jnp.where(keep_b, dot_nt(jnp.where(lo, 0.0, cat(rt, at)), cat(kh, bh)), 0.0),
               at, rt, kh, bh)
    rhs = each(lambda ps, m_a, m_b, vm_a, vm_b: ps[:L] + dot(m_a[:L], cat(zeros, vm_a)) + dot(m_b[L:], cat(vm_b, zeros)),
               ps, m_a, m_b, vm_a, vm_b)

    qs = each(lambda m_a, m_b: jnp.where(ul, m_a, 0.0) + jnp.where(lr, m_b, 0.0), m_a, m_b)
    invs = [eye + q for q in qs]
    for _ in range(5):
        qs = [dot(q, q) for q in qs]
        invs = each(lambda inv, q: inv + dot(inv, q), invs, qs)

    u2 = each(lambda inv, rhs: dot(inv, cat(jnp.where(lo, rhs, 0.0), jnp.where(lo, 0.0, rhs))), invs, rhs)
    y = each(lambda ps, m_a, m_b, u2, vm_a, vm_b: ps[L:] + dot(m_a[L:], cat(u2[:L], vm_a)) + dot(m_b[:L], cat(vm_b, u2[L:])),
             ps, m_a, m_b, u2, vm_a, vm_b)
    for p in pairs:
        y_ref[:, p * LANES:(p + 1) * LANES] = y[p]
    upd = each(lambda v, u2, kb: dot(cat(v, u2[:L] + u2[L:]).T, kb), v, u2, kb)
    for p in pairs:
        s_ref[p] = s[p] * jnp.exp(c_last[p]) + jnp.where(same_head, upd[p], 0.0)

    @pl.when(pl.program_id(0) == pl.num_programs(0) - 1)
    def _():
        sf_ref[...] = s_ref[...]


def _rwkv_chunked(r, w, k, v, a, b, s0):
    t = r.shape[0]
    pair_eye = jnp.eye(2, dtype=F32)[None, :, None, :, None]
    s0p = (s0.reshape(N_PAIRS, 2, HEAD, 1, HEAD) * pair_eye).reshape(N_PAIRS, LANES, LANES)
    seq = pl.BlockSpec((CHUNK, MIX), lambda i: (i, 0))
    st = pl.BlockSpec((N_PAIRS, LANES, LANES), lambda i: (0, 0, 0))
    y, sf = pl.pallas_call(
        _rwkv_chunk_kernel,
        out_shape=(jax.ShapeDtypeStruct((t, MIX), F32), jax.ShapeDtypeStruct((N_PAIRS, LANES, LANES), F32)),
        grid=(t // CHUNK,),
        in_specs=[seq] * 6 + [st],
        out_specs=(seq, st),
        scratch_shapes=[pltpu.VMEM((N_PAIRS, LANES, LANES), F32)],
        compiler_params=_cparams(("arbitrary",), 48),
        name="rwkv_chunked",
    )(r, w, k, v, a, b, s0p)
    sf = sf.reshape(N_PAIRS, 2, HEAD, 2, HEAD)
    sf = jnp.stack([sf[:, 0, :, 0], sf[:, 1, :, 1]], axis=1).reshape(N_HEADS, HEAD, HEAD)
    return y, sf


def _rwkv_post_kernel(y_ref, bonus_ref, g_ref, lw_ref, lb_ref, ones_ref, o_ref):
    y = y_ref[...]
    ones = ones_ref[...]
    m = _dot_exact_rhs(y, ones) * (1.0 / HEAD)
    d = y - m
    var = _dot_exact_rhs(d * d, ones) * (1.0 / HEAD)
    yn = d * lax.rsqrt(var + GN_EPS) * lw_ref[...] + lb_ref[...]
    o_ref[...] = (yn + bonus_ref[...]) * g_ref[...]


def _rwkv_post(y, bonus, g, lnx_w, lnx_b, ones, tm):
    m = y.shape[0]
    blk = pl.BlockSpec((tm, LANES), lambda i, c: (i, c))
    vec = pl.BlockSpec((1, LANES), lambda i, c: (0, c))
    return pl.pallas_call(
        _rwkv_post_kernel,
        out_shape=jax.ShapeDtypeStruct((m, MIX), F32),
        grid=(m // tm, MIX // LANES),
        in_specs=[blk, blk, blk, vec, vec, pl.BlockSpec((LANES, LANES), lambda i, c: (0, 0))],
        out_specs=blk,
        compiler_params=_cparams(("parallel", "parallel"), 32),
        name="rwkv_post",
    )(y, bonus, g, lnx_w, lnx_b, ones)


def _sb_inside(z, lower, allowed, two_pass):
    sp = jnp.maximum(z, 0.0) + jnp.log(1.0 + jnp.exp(-jnp.abs(z)))
    ls = z - sp
    if allowed is not None:
        sp = jnp.where(allowed, sp, 0.0)
    lhs = sp.astype(BF16)
    if two_pass:
        lhs = jnp.concatenate([lhs, (sp - lhs.astype(F32)).astype(BF16)], axis=1)
    right = jnp.dot(lhs, lower, preferred_element_type=F32)
    return ls, right, right[:, 0:1] + sp[:, 0:1]


def _sb_weights(scores, carry, allowed):
    ls, right, total = scores
    a = jnp.exp(ls - right - carry)
    if allowed is not None:
        a = jnp.where(allowed, a, 0.0)
    return a, carry + total


def _sb_tile(z, carry, lower, allowed, two_pass):
    return _sb_weights(_sb_inside(z, lower, allowed, two_pass), carry, allowed)


def _sb_prompt_kernel(tq, q_ref, k_ref, v_ref, up_ref, o_ref, kt_ref, vt_ref, kb_ref, va_ref, vb_ref, acc_ref,
                      z_ref, p_ref, c_ref):
    qi = pl.program_id(1)
    lane = lax.broadcasted_iota(jnp.int32, (1, LANES), 1)
    lo = lane < HEAD

    @pl.when(qi == 0)
    def _():
        kb_ref[...] = k_ref[...].astype(BF16)
        v = v_ref[...]
        va_ref[...] = jnp.where(lo, v, 0.0).astype(BF16)
        vb_ref[...] = jnp.where(lo, 0.0, v).astype(BF16)

        def to_cache_layout(g, carry):
            rows = pl.ds(pl.multiple_of(g * PAGE, PAGE), PAGE)
            kt_ref[g] = k_ref[rows, :].T
            vt_ref[g] = v_ref[rows, :].T
            return carry

        lax.fori_loop(0, kt_ref.shape[0], to_cache_layout, 0)

    q = q_ref[...] * (HEAD ** -0.5)
    qa = jnp.where(lo, q, 0.0).astype(BF16)
    qb = jnp.where(lo, 0.0, q).astype(BF16)
    lower = up_ref[...]

    def logits_to(j, slot):
        kblk = kb_ref[pl.ds(pl.multiple_of(jnp.maximum(j, 0) * tq, tq), tq), :]
        z_ref[slot, 0] = lax.dot_general(qa, kblk, _NT, preferred_element_type=F32)
        z_ref[slot, 1] = lax.dot_general(qb, kblk, _NT, preferred_element_type=F32)

    def accumulate(j, slot):
        ks = pl.multiple_of(j * tq, tq)
        acc_ref[...] += (jnp.dot(p_ref[slot, 0], va_ref[pl.ds(ks, tq), :], preferred_element_type=F32)
                         + jnp.dot(p_ref[slot, 1], vb_ref[pl.ds(ks, tq), :], preferred_element_type=F32))

    def weights_to(za, zb, slot, allowed):
        pa, ca = _sb_tile(za, c_ref[0], lower, allowed, False)
        pb, cb = _sb_tile(zb, c_ref[1], lower, allowed, False)
        p_ref[slot, 0] = pa.astype(BF16)
        p_ref[slot, 1] = pb.astype(BF16)
        c_ref[0] = ca
        c_ref[1] = cb

    def step(j, src, dst):
        accumulate(j + 1, src)
        za, zb = z_ref[src, 0], z_ref[src, 1]
        logits_to(j - 1, dst)
        weights_to(za, zb, dst, None)

    acc_ref[...] = jnp.zeros_like(acc_ref)
    c_ref[...] = jnp.zeros_like(c_ref)
    below_diag = (lax.broadcasted_iota(jnp.int32, (tq, tq), 1) < lax.broadcasted_iota(jnp.int32, (tq, tq), 0))
    logits_to(qi, 1)
    za, zb = z_ref[1, 0], z_ref[1, 1]
    logits_to(qi - 1, 0)
    weights_to(za, zb, 0, below_diag)

    odd = qi & 1

    @pl.when(odd == 1)
    def _():
        step(qi - 1, 0, 0)

    def pair(n, carry):
        j = qi - 1 - odd - 2 * n
        step(j, 0, 1)
        step(j - 1, 1, 0)
        return carry

    lax.fori_loop(0, qi >> 1, pair, 0)
    accumulate(0, 0)
    o_ref[...] = acc_ref[...]


def _sb_prompt(z, upper, tq):
    s = z.shape[0]
    nc = MIX // LANES
    return pl.pallas_call(
        functools.partial(_sb_prompt_kernel, tq),
        out_shape=(jax.ShapeDtypeStruct((s, MIX), F32),) + (jax.ShapeDtypeStruct((s // PAGE, MIX, PAGE), F32),) * 2,
        grid=(nc, s // tq),
        in_specs=[
            pl.BlockSpec((tq, LANES), lambda c, i: (i, c)),
            pl.BlockSpec((s, LANES), lambda c, i: (0, nc + c)),
            pl.BlockSpec((s, LANES), lambda c, i: (0, 2 * nc + c)),
            pl.BlockSpec((tq, tq), lambda c, i: (0, 0)),
        ],
        out_specs=(pl.BlockSpec((tq, LANES), lambda c, i: (i, c)),)
                  + (pl.BlockSpec((s // PAGE, LANES, PAGE), lambda c, i: (0, c, 0)),) * 2,
        scratch_shapes=[pltpu.VMEM((s, LANES), BF16)] * 3
                       + [pltpu.VMEM((tq, LANES), F32),
                          pltpu.VMEM((2, 2, tq, tq), F32),
                          pltpu.VMEM((2, 2, tq, tq), BF16),
                          pltpu.VMEM((2, tq, 1), F32)],
        compiler_params=_cparams(("parallel", "arbitrary"), 56),
        name="sb_prompt",
    )(z, z, z, upper)


QROWS = 8
SB_PAGE_GROUP = 8


def _sb_sample_kernel(group, pt_ref, q_ref, kn_ref, vn_ref, *refs):
    kc_refs, vc_refs = refs[:group], refs[group:2 * group]
    up_ref, o_ref, acc_ref, carry_ref = refs[2 * group:]
    p = pl.program_id(1)
    lower2 = up_ref[...]
    rows = N_HEADS * QROWS

    def tiles(k_refs, v_refs, carry, allowed):
        zs = [jnp.concatenate([jnp.dot(q_ref[0, h], k_ref[0, h].astype(BF16), preferred_element_type=F32)
                               for h in range(N_HEADS)], axis=0) for k_ref in k_refs]
        scores = [_sb_inside(z, lower2, allowed, True) for z in zs]
        total = None
        for score, v_ref in zip(scores, v_refs):
            a, carry = _sb_weights(score, carry, allowed)
            outs = [lax.dot_general(a[h * QROWS:(h + 1) * QROWS].astype(BF16), v_ref[0, h].astype(BF16),
                                    _NT, preferred_element_type=F32) for h in range(N_HEADS)]
            total = outs if total is None else [t + o for t, o in zip(total, outs)]
        return total, carry

    @pl.when(p == 0)
    def _():
        key = lax.broadcasted_iota(jnp.int32, (rows, PAGE), 1)
        qry = lax.broadcasted_iota(jnp.int32, (rows, PAGE), 0) & (QROWS - 1)
        outs, carry = tiles([kn_ref], [vn_ref], jnp.zeros((rows, 1), F32), key < qry)
        carry_ref[...] = carry
        for h in range(N_HEADS):
            acc_ref[h] = outs[h]

    @pl.when(p > 0)
    def _():
        outs, carry = tiles(kc_refs, vc_refs, carry_ref[...], None)
        carry_ref[...] = carry
        for h in range(N_HEADS):
            acc_ref[h] += outs[h]

    @pl.when(p == pl.num_programs(1) - 1)
    def _():
        o_ref[0] = acc_ref[...]


def _sb_sample(q, k_new, v_new, cache_k, cache_v, page_table, lower2):
    bsz = q.shape[0]
    n_pages = page_table.shape[1]
    group = next(g for g in (SB_PAGE_GROUP, 4, 2, 1) if n_pages % g == 0)

    def page(g):
        return lambda b, p, pt: (pt[b, n_pages - 1 - (jnp.maximum(p, 1) - 1) * group - g], 0, 0, 0)

    own = lambda b, p, pt: (b, 0, 0, 0)
    kv_block = (1, N_HEADS, HEAD, PAGE)
    grid_spec = pltpu.PrefetchScalarGridSpec(
        num_scalar_prefetch=1,
        grid=(bsz, n_pages // group + 1),
        in_specs=[pl.BlockSpec((1, N_HEADS, QROWS, HEAD), own), pl.BlockSpec(kv_block, own),
                  pl.BlockSpec(kv_block, own)]
                 + [pl.BlockSpec(kv_block, page(g)) for g in range(group)] * 2
                 + [pl.BlockSpec((2 * PAGE, PAGE), lambda b, p, pt: (0, 0))],
        out_specs=pl.BlockSpec((1, N_HEADS, QROWS, HEAD), own),
        scratch_shapes=[pltpu.VMEM((N_HEADS, QROWS, HEAD), F32), pltpu.VMEM((N_HEADS * QROWS, 1), F32)],
    )
    return pl.pallas_call(
        functools.partial(_sb_sample_kernel, group),
        out_shape=jax.ShapeDtypeStruct((bsz, N_HEADS, QROWS, HEAD), F32),
        grid_spec=grid_spec,
        compiler_params=_cparams(("parallel", "arbitrary"), 48),
        name="sb_sample",
    )(page_table, q, k_new, v_new, *([cache_k] * group), *([cache_v] * group), lower2)


def _rmsnorm_kernel(x_ref, g_ref, o_ref):
    o_ref[...] = _rms_rows(x_ref[...], g_ref[...])


def _rmsnorm(x, g, tm):
    m, d = x.shape
    return pl.pallas_call(
        _rmsnorm_kernel,
        out_shape=jax.ShapeDtypeStruct((m, d), F32),
        grid=(m // tm,),
        in_specs=[pl.BlockSpec((tm, d), lambda i: (i, 0)), pl.BlockSpec((1, d), lambda i: (0, 0))],
        out_specs=pl.BlockSpec((tm, d), lambda i: (i, 0)),
        compiler_params=_cparams(("parallel",), 32),
        name="rmsnorm",
    )(x, g.reshape(1, d))


def _pad_cols(x, width):
    return jnp.pad(x, ((0, 0), (0, width - x.shape[1])))


def _rwkv_cols(x):
    o = 3 * MIX
    parts = [x[:, :o],
             _pad_cols(x[:, o:o + DECAY_LORA], LANES),
             _pad_cols(x[:, o + DECAY_LORA:o + DECAY_LORA + ICLR_LORA], LANES),
             _pad_cols(x[:, o + DECAY_LORA + ICLR_LORA:RWKV_PROJ], 2 * LANES)]
    if x.shape[1] > RWKV_PROJ:
        parts.append(x[:, RWKV_PROJ:])
    return jnp.concatenate(parts, axis=1)


def _rwkv_uncols(x):
    o = 3 * MIX
    return jnp.concatenate([x[:, :o], x[:, o:o + DECAY_LORA], x[:, o + LANES:o + LANES + ICLR_LORA],
                            x[:, o + 2 * LANES:o + 2 * LANES + GATE_LORA]], axis=1)


def _pad_rows(x, rows):
    return jnp.pad(x, ((0, rows - x.shape[0]), (0, 0)))


def _tile(m, pref):
    return pref if m % pref == 0 else m


def kernel(x_prompt, x_sample, mem_prompt, state_rwkv_wkv, state_rwkv_shift, cache_sb_k, cache_sb_v, page_table, cache_mem_k, cache_mem_v, norm_mix, norm_ffn, norm_final, mem_norm, w_mem_kv, w_out, w_ffn_up, w_ffn_down, rwkv_w_in, rwkv_mu, rwkv_w0, rwkv_w2, rwkv_a0, rwkv_a2, rwkv_g2, rwkv_k_k, rwkv_k_a, rwkv_r_k, rwkv_lnx_w, rwkv_lnx_b, sb_w_in, sb_b_qk):
    bp, s, d = x_prompt.shape
    db, t_new, _ = x_sample.shape
    assert bp == 1, "the prompt path handles a single sequence"
    depth = norm_mix.shape[0]
    ms = db * t_new
    n_mem = mem_prompt.shape[1]
    xp = x_prompt.reshape(s, d)
    xs = x_sample.reshape(ms, d)
    tmp = _tile(s, 512)
    tms = ms
    tmp_in = _tile(s, 1024)
    col_tile = lambda n: 1024 if n % 1024 == 0 else 512

    idx = lax.broadcasted_iota(jnp.int32, (LANES, LANES), 0)
    jdx = lax.broadcasted_iota(jnp.int32, (LANES, LANES), 1)
    head_ones = ((idx // HEAD) == (jdx // HEAD)).astype(BF16)
    tq = _tile(s, 256)
    iq = lax.broadcasted_iota(jnp.int32, (tq, tq), 0)
    jq = lax.broadcasted_iota(jnp.int32, (tq, tq), 1)
    lower_q = (iq > jq).astype(BF16)
    lower_p = jnp.tile((idx > jdx).astype(BF16), (2, 1))

    wu_all, wd_all = w_ffn_up.astype(BF16), w_ffn_down.astype(BF16)
    cmk_rows = cache_mem_k.reshape(depth * db, n_mem * MEM_HEADS, MEM_HD)
    cmv_rows = cache_mem_v.reshape(depth * db, n_mem * MEM_HEADS, MEM_HD)
    mem_k, mem_v = [], []
    wkv_p, shift_p, wkv_s, shift_s = [], [], [], []
    sbk_p, sbv_p, sbk_s, sbv_s = [], [], [], []
    for i in range(depth):
        j = i // 2
        kv = _rms_matmul(mem_prompt.reshape(n_mem, d), mem_norm[i], w_mem_kv[i].astype(BF16),
                         jnp.zeros((2 * MEM_W,), F32), n_mem, 2 * MEM_W)
        mk, mv = kv[:, :MEM_W], kv[:, MEM_W:]
        mem_k.append(mk.reshape(1, n_mem, MEM_HEADS, MEM_HD))
        mem_v.append(mv.reshape(1, n_mem, MEM_HEADS, MEM_HD))
        if i % 2 == 0:
            w_in = _rwkv_cols(rwkv_w_in[j]).astype(BF16)
            n_in = w_in.shape[1]
            zero_b = jnp.zeros((n_in,), F32)
            zp = _rms_matmul(xp, norm_mix[i], w_in, zero_b, tmp_in, col_tile(n_in))
            zs = _rms_matmul(xs, norm_mix[i], w_in, zero_b, tms, col_tile(n_in))
            shift_p.append(_rwkv_uncols(zp[s - 1:s]))
            shift_s.append(_rwkv_uncols(zs.reshape(db, t_new, n_in)[:, t_new - 1]))
            wide = 3 * MIX + LORA_PAD
            mu = _rwkv_cols(rwkv_mu[j][None])
            vecs = [x.reshape(1, MIX) for x in (rwkv_w0[j], rwkv_a0[j], rwkv_k_k[j], rwkv_k_a[j], rwkv_r_k[j])]
            lora = (_pad_rows(rwkv_w2[j], LANES).astype(BF16), _pad_rows(rwkv_a2[j], LANES).astype(BF16),
                    _pad_rows(rwkv_g2[j], 2 * LANES).astype(BF16))
            lnx = (rwkv_lnx_w[j].reshape(1, MIX), rwkv_lnx_b[j].reshape(1, MIX))

            def rwkv(z, shift_rows, seq_len, tm, s0, nb, tb, tpost):
                r, w, k, v, a, b, g, bonus = _rwkv_prep(z, shift_rows, seq_len, tm, mu, *vecs, *lora, head_ones)
                if nb == 1 and seq_len % CHUNK == 0:
                    y, s_fin = _rwkv_chunked(r, w, k, v, a, b, s0[0])
                    s_fin = s_fin[None]
                else:
                    sh = lambda x: x.reshape(nb, seq_len, MIX)
                    y, s_fin = _rwkv_scan(sh(r), sh(w), sh(k), sh(v), sh(a), sh(b), s0, tb)
                return _rwkv_post(y.reshape(nb * seq_len, MIX), bonus, g, *lnx, head_ones, tpost), s_fin

            zero_shift = jnp.zeros((8, wide), F32)
            zero_state = jnp.zeros((1, N_HEADS, HEAD, HEAD), F32)
            mix_p, st_p = rwkv(zp, zero_shift, s, tmp_in, zero_state, 1, _tile(s, 256), tmp_in)
            shift_rows = jnp.repeat(_rwkv_cols(state_rwkv_shift[j]), t_new, axis=0)
            mix_s, st_s = rwkv(zs, shift_rows, t_new, tms, state_rwkv_wkv[j], db, t_new, tms)
            wkv_p.append(st_p)
            wkv_s.append(st_s)
            q_off = wide
        else:
            w_in = sb_w_in[j].astype(BF16)
            bias = jnp.concatenate([sb_b_qk[j], jnp.zeros((w_in.shape[1] - 2 * MIX,), F32)])
            zp = _rms_matmul(xp, norm_mix[i], w_in, bias, tmp_in, col_tile(w_in.shape[1]))
            zs = _rms_matmul(xs, norm_mix[i], w_in, bias, tms, col_tile(w_in.shape[1]))
            mix_p, kt_p, vt_p = _sb_prompt(zp, lower_q, tq)
            assert t_new < QROWS
            heads = lambda x: x.reshape(db, t_new, N_HEADS, HEAD)
            q_s, k_s, v_s = heads(zs[:, :MIX]), heads(zs[:, MIX:2 * MIX]), heads(zs[:, 2 * MIX:3 * MIX])
            q_hq = jnp.pad((q_s * (HEAD ** -0.5)).transpose(0, 2, 1, 3),
                           ((0, 0), (0, 0), (0, QROWS - t_new), (0, 0))).astype(BF16)
            keys_last = lambda x: x.transpose(0, 2, 3, 1)
            pad_new = lambda x: keys_last(jnp.pad(x, ((0, 0), (0, PAGE - t_new), (0, 0), (0, 0))))
            o_hq = _sb_sample(q_hq, pad_new(k_s), pad_new(v_s), keys_last(cache_sb_k[j]),
                              keys_last(cache_sb_v[j]), page_table, lower_p)
            mix_s = o_hq[:, :, :t_new].transpose(0, 2, 1, 3).reshape(ms, MIX)
            as_cache = lambda x: x.reshape(1, s // PAGE, N_HEADS, HEAD, PAGE).transpose(0, 1, 4, 2, 3)
            sbk_p.append(as_cache(kt_p))
            sbv_p.append(as_cache(vt_p))
            sbk_s.append(k_s)
            sbv_s.append(v_s)
            q_off = 3 * MIX
        cp = _mem_attend(zp[:, q_off:q_off + MEM_W][None], mk[None], mv[None], tmp)[0]
        cs = _mem_attend(zs[:, q_off:q_off + MEM_W].reshape(db, t_new, MEM_W), cmk_rows, cmv_rows, t_new,
                         first=i * db).reshape(ms, MEM_W)
        wo = w_out[i].astype(BF16)
        xp = _outproj(xp, mix_p, cp, wo[:MIX], wo[MIX:], tmp)
        xs = _outproj(xs, mix_s, cs, wo[:MIX], wo[MIX:], tms)
        xp = _ffn(xp, norm_ffn[i], wu_all, wd_all, i, tmp, col_tile(wu_all.shape[2]))
        xs = _ffn(xs, norm_ffn[i], wu_all, wd_all, i, tms, col_tile(wu_all.shape[2]))
    y_prompt = _rmsnorm(xp, norm_final, tmp).reshape(1, s, d)
    y_sample = _rmsnorm(xs, norm_final, tms).reshape(db, t_new, d)
    return (y_prompt, y_sample, jnp.stack(mem_k), jnp.stack(mem_v),
            jnp.stack(wkv_p), jnp.stack(shift_p), jnp.stack(wkv_s), jnp.stack(shift_s),
            jnp.stack(sbk_p), jnp.stack(sbv_p), jnp.stack(sbk_s), jnp.stack(sbv_s))
```

```python
import functools

import jax
import jax.numpy as jnp
from jax import lax
from jax.experimental import pallas as pl
from jax.experimental.pallas import tpu as pltpu

F32 = jnp.float32
BF16 = jnp.bfloat16

HEAD = 64
N_HEADS = 24
MIX = N_HEADS * HEAD
N_PAIRS = N_HEADS // 2
LANES = 128
MEM_HEADS = 4
MEM_HD = 128
MEM_W = MEM_HEADS * MEM_HD
DECAY_LORA = 64
ICLR_LORA = 64
GATE_LORA = 224
LORA_PAD = 512
RWKV_PROJ = 3 * MIX + DECAY_LORA + ICLR_LORA + GATE_LORA
RMS_EPS = 1e-6
GN_EPS = 64e-5
PAGE = 128

_NT = (((1,), (1,)), ((), ()))


def _cparams(sem, vmem_mb):
    return pltpu.CompilerParams(dimension_semantics=sem, vmem_limit_bytes=vmem_mb << 20)


def _split3(x):
    hi = x.astype(BF16)
    r1 = x - hi.astype(F32)
    mid = r1.astype(BF16)
    lo = (r1 - mid.astype(F32)).astype(BF16)
    return hi, mid, lo


def _dot_exact_rhs(x, ones_bf16):
    hi, mid, lo = _split3(x)
    d = lambda a: jnp.dot(a, ones_bf16, preferred_element_type=F32)
    return d(hi) + d(mid) + d(lo)


def _rms_rows(x, g):
    ms = jnp.mean(x * x, axis=-1, keepdims=True)
    return x * lax.rsqrt(ms + RMS_EPS) * g


def _rms_matmul_kernel(x_ref, g_ref, w_ref, b_ref, o_ref, xn_ref):
    @pl.when(pl.program_id(1) == 0)
    def _():
        xn_ref[...] = _rms_rows(x_ref[...], g_ref[...]).astype(BF16)

    o_ref[...] = jnp.dot(xn_ref[...], w_ref[...], preferred_element_type=F32) + b_ref[...]


def _rms_matmul(x, g, w, b, tm, tn):
    m, d = x.shape
    n = w.shape[1]
    return pl.pallas_call(
        _rms_matmul_kernel,
        out_shape=jax.ShapeDtypeStruct((m, n), F32),
        grid=(m // tm, n // tn),
        in_specs=[
            pl.BlockSpec((tm, d), lambda i, j: (i, 0)),
            pl.BlockSpec((1, d), lambda i, j: (0, 0)),
            pl.BlockSpec((d, tn), lambda i, j: (0, j)),
            pl.BlockSpec((1, tn), lambda i, j: (0, j)),
        ],
        out_specs=pl.BlockSpec((tm, tn), lambda i, j: (i, j)),
        scratch_shapes=[pltpu.VMEM((tm, d), BF16)],
        compiler_params=_cparams(("parallel", "arbitrary"), 48),
        name="rms_matmul",
    )(x, g.reshape(1, d), w, b.reshape(1, n))


def _ffn_kernel(x_ref, g_ref, wu_ref, wd_ref, o_ref, xn_ref):
    @pl.when(pl.program_id(1) == 0)
    def _():
        x = x_ref[...]
        xn_ref[...] = _rms_rows(x, g_ref[...]).astype(BF16)
        o_ref[...] = x

    h = jnp.maximum(jnp.dot(xn_ref[...], wu_ref[...], preferred_element_type=F32), 0.0)
    o_ref[...] += jnp.dot((h * h).astype(BF16), wd_ref[...], preferred_element_type=F32)


def _ffn(x, g, wu, wd, layer, tm, tf):
    m, d = x.shape
    f = wu.shape[2]
    return pl.pallas_call(
        _ffn_kernel,
        out_shape=jax.ShapeDtypeStruct((m, d), F32),
        grid=(m // tm, f // tf),
        in_specs=[
            pl.BlockSpec((tm, d), lambda i, j: (i, 0)),
            pl.BlockSpec((1, d), lambda i, j: (0, 0)),
            pl.BlockSpec((None, d, tf), lambda i, j: (layer, 0, j)),
            pl.BlockSpec((None, tf, d), lambda i, j: (layer, j, 0)),
        ],
        out_specs=pl.BlockSpec((tm, d), lambda i, j: (i, 0)),
        scratch_shapes=[pltpu.VMEM((tm, d), BF16)],
        compiler_params=_cparams(("parallel", "arbitrary"), 48),
        name="ffn",
    )(x, g.reshape(1, d), wu, wd)


def _outproj_kernel(x_ref, mix_ref, c_ref, w1_ref, w2_ref, o_ref):
    acc = jnp.dot(mix_ref[...].astype(BF16), w1_ref[...], preferred_element_type=F32)
    acc += jnp.dot(c_ref[...].astype(BF16), w2_ref[...], preferred_element_type=F32)
    o_ref[...] = x_ref[...] + acc


def _outproj(x, mix, c, w1, w2, tm):
    m, d = x.shape
    return pl.pallas_call(
        _outproj_kernel,
        out_shape=jax.ShapeDtypeStruct((m, d), F32),
        grid=(m // tm,),
        in_specs=[
            pl.BlockSpec((tm, d), lambda i: (i, 0)),
            pl.BlockSpec((tm, MIX), lambda i: (i, 0)),
            pl.BlockSpec((tm, MEM_W), lambda i: (i, 0)),
            pl.BlockSpec((MIX, d), lambda i: (0, 0)),
            pl.BlockSpec((MEM_W, d), lambda i: (0, 0)),
        ],
        out_specs=pl.BlockSpec((tm, d), lambda i: (i, 0)),
        compiler_params=_cparams(("parallel",), 48),
        name="outproj",
    )(x, mix, c, w1, w2)


def _mem_attend_kernel(head_rows, q_ref, k_ref, v_ref, o_ref):
    q = q_ref[0]
    heads = [slice(h * MEM_HD, (h + 1) * MEM_HD) for h in range(MEM_HEADS)]

    def head(ref, h):
        if head_rows:
            return ref[0, pl.ds(h, ref.shape[1] // MEM_HEADS, stride=MEM_HEADS), :].astype(BF16)
        return ref[0, :, heads[h]].astype(BF16)

    s = [lax.dot_general(q[:, sl].astype(BF16), head(k_ref, h), _NT, preferred_element_type=F32) * (MEM_HD ** -0.5)
         for h, sl in enumerate(heads)]
    e = [jnp.exp(x - jnp.max(x, axis=-1, keepdims=True)) for x in s]
    p = [x / jnp.sum(x, axis=-1, keepdims=True) for x in e]
    for h, sl in enumerate(heads):
        o_ref[0, :, sl] = jnp.dot(p[h].astype(BF16), head(v_ref, h), preferred_element_type=F32)


def _mem_attend(q, mk, mv, tt, first=0):
    b, t, _ = q.shape
    head_rows = mk.shape[2] == MEM_HD
    kv_spec = pl.BlockSpec((1,) + mk.shape[1:], lambda i, j: (first + i, 0, 0))
    return pl.pallas_call(
        functools.partial(_mem_attend_kernel, head_rows),
        out_shape=jax.ShapeDtypeStruct((b, t, MEM_W), F32),
        grid=(b, t // tt),
        in_specs=[pl.BlockSpec((1, tt, MEM_W), lambda i, j: (i, j, 0)), kv_spec, kv_spec],
        out_specs=pl.BlockSpec((1, tt, MEM_W), lambda i, j: (i, j, 0)),
        compiler_params=_cparams(("parallel", "parallel"), 32),
        name="mem_attend",
    )(q, mk, mv)


def _rwkv_prep_kernel(seq_len, tm,
                      pr_ref, pk_ref, pv_ref, pl_ref,
                      br_ref, bk_ref, bv_ref, bl_ref,
                      sr_ref, sk_ref, sv_ref, sl_ref,
                      mur_ref, muk_ref, muv_ref, mul_ref,
                      w0_ref, a0_ref, kk_ref, ka_ref, rk_ref,
                      w2_ref, a2_ref, g2_ref, ones_ref,
                      r_out, w_out, k_out, v_out, a_out, b_out, g_out, bonus_out):
    i = pl.program_id(0)
    row = lax.broadcasted_iota(jnp.int32, (tm, 1), 0)

    def shifted(cur_ref, before_ref, state_ref, mu_ref):
        cur = cur_ref[...]
        rolled = pltpu.roll(cur, 1, 0)
        if seq_len >= tm:
            first = jnp.where(i == 0, state_ref[0:1, :], before_ref[7:8, :])
            prev = jnp.where(row == 0, first, rolled)
        else:
            prev = jnp.where(row % seq_len == 0, state_ref[...], rolled)
        return cur + (prev - cur) * mu_ref[...]

    r = shifted(pr_ref, br_ref, sr_ref, mur_ref)
    k = shifted(pk_ref, bk_ref, sk_ref, muk_ref)
    v = shifted(pv_ref, bv_ref, sv_ref, muv_ref)
    lo = shifted(pl_ref, bl_ref, sl_ref, mul_ref)

    xw = jnp.tanh(lo[:, 0:128]).astype(BF16)
    xa = lo[:, 128:256].astype(BF16)
    xg = jax.nn.sigmoid(lo[:, 256:512]).astype(BF16)
    wl = w0_ref[...] + jnp.dot(xw, w2_ref[...], preferred_element_type=F32)
    w_log = -jax.nn.softplus(-wl) - 0.5
    log_decay = -jnp.exp(w_log)
    a = jax.nn.sigmoid(a0_ref[...] + jnp.dot(xa, a2_ref[...], preferred_element_type=F32))
    g = jnp.dot(xg, g2_ref[...], preferred_element_type=F32)

    ones = ones_ref[...]
    kk = k * kk_ref[...]
    nrm = jnp.sqrt(_dot_exact_rhs(kk * kk, ones))
    kk = kk / jnp.maximum(nrm, 1e-12)
    k2 = k * (1.0 + (a - 1.0) * ka_ref[...])
    bonus = _dot_exact_rhs(r * k2 * rk_ref[...], ones) * v

    r_out[...] = r
    w_out[...] = log_decay
    k_out[...] = k2
    v_out[...] = v
    a_out[...] = -kk
    b_out[...] = kk * a
    g_out[...] = g
    bonus_out[...] = bonus


def _rwkv_prep(z, shift_rows, seq_len, tm, mu, w0, a0, k_k, k_a, r_k, w2, a2, g2, ones):
    m = z.shape[0]
    nc = MIX // LANES
    lb = 3 * MIX // LORA_PAD
    tb = tm // 8
    srows = shift_rows.shape[0] if seq_len >= tm else tm

    def cur(off):
        return pl.BlockSpec((tm, LANES), lambda i, c: (i, off + c))

    def before(off):
        return pl.BlockSpec((8, LANES), lambda i, c: (jnp.maximum(i * tb - 1, 0), off + c))

    def state(off):
        if seq_len >= tm:
            return pl.BlockSpec((srows, LANES), lambda i, c: (0, off + c))
        return pl.BlockSpec((tm, LANES), lambda i, c: (i, off + c))

    def vec(off):
        return pl.BlockSpec((1, LANES), lambda i, c: (0, off + c))

    cur_l = pl.BlockSpec((tm, LORA_PAD), lambda i, c: (i, lb))
    before_l = pl.BlockSpec((8, LORA_PAD), lambda i, c: (jnp.maximum(i * tb - 1, 0), lb))
    if seq_len >= tm:
        state_l = pl.BlockSpec((srows, LORA_PAD), lambda i, c: (0, lb))
    else:
        state_l = pl.BlockSpec((tm, LORA_PAD), lambda i, c: (i, lb))
    vec_l = pl.BlockSpec((1, LORA_PAD), lambda i, c: (0, lb))

    out_spec = pl.BlockSpec((tm, LANES), lambda i, c: (i, c))
    out_sds = jax.ShapeDtypeStruct((m, MIX), F32)
    return pl.pallas_call(
        functools.partial(_rwkv_prep_kernel, seq_len, tm),
        out_shape=(out_sds,) * 8,
        grid=(m // tm, nc),
        in_specs=[
            cur(0), cur(nc), cur(2 * nc), cur_l,
            before(0), before(nc), before(2 * nc), before_l,
            state(0), state(nc), state(2 * nc), state_l,
            vec(0), vec(nc), vec(2 * nc), vec_l,
            vec(0), vec(0), vec(0), vec(0), vec(0),
            pl.BlockSpec((LANES, LANES), lambda i, c: (0, c)),
            pl.BlockSpec((LANES, LANES), lambda i, c: (0, c)),
            pl.BlockSpec((2 * LANES, LANES), lambda i, c: (0, c)),
            pl.BlockSpec((LANES, LANES), lambda i, c: (0, 0)),
        ],
        out_specs=(out_spec,) * 8,
        compiler_params=_cparams(("parallel", "parallel"), 48),
        name="rwkv_prep",
    )(z, z, z, z, z, z, z, z, shift_rows, shift_rows, shift_rows, shift_rows,
      mu, mu, mu, mu, w0, a0, k_k, k_a, r_k, w2, a2, g2, ones)


def _rwkv_scan_kernel(tb, r_ref, w_ref, k_ref, v_ref, a_ref, b_ref, s0_ref, y_ref, sf_ref, s_ref):
    tblk = pl.program_id(1)

    @pl.when(tblk == 0)
    def _():
        for p in range(N_PAIRS):
            s_ref[p] = jnp.concatenate([s0_ref[0, 2 * p], s0_ref[0, 2 * p + 1]], axis=1)

    lane = lax.broadcasted_iota(jnp.int32, (HEAD, LANES), 1)
    sub = lax.broadcasted_iota(jnp.int32, (HEAD, LANES), 0)
    lo = lane < HEAD
    diag = (lane % HEAD) == sub

    def head_sums(x):
        sa = jnp.sum(jnp.where(lo, x, 0.0), axis=1, keepdims=True)
        sb = jnp.sum(jnp.where(lo, 0.0, x), axis=1, keepdims=True)
        return jnp.where(lo, sa, sb)

    def step(t, carry):
        row = lambda ref: ref[0, pl.ds(t, 1), :]
        r_t, w_t, k_t, v_t, a_t, b_t = (row(x) for x in (r_ref, w_ref, k_ref, v_ref, a_ref, b_ref))
        w_t = jnp.exp(w_t)
        pairs = [slice(p * LANES, (p + 1) * LANES) for p in range(N_PAIRS)]
        s = [s_ref[p] for p in range(N_PAIRS)]
        sa = [head_sums(s[p] * a_t[:, sl]) for p, sl in enumerate(pairs)]
        vc = [head_sums(jnp.where(diag, v_t[:, sl], 0.0)) for sl in pairs]
        s = [s[p] * w_t[:, sl] + sa[p] * b_t[:, sl] + vc[p] * k_t[:, sl] for p, sl in enumerate(pairs)]
        for p in range(N_PAIRS):
            s_ref[p] = s[p]
        yc = [head_sums(s[p] * r_t[:, sl]) for p, sl in enumerate(pairs)]
        ys = [jnp.sum(jnp.where(diag, y, 0.0), axis=0, keepdims=True) for y in yc]
        y_ref[0, pl.ds(t, 1), :] = jnp.concatenate(ys, axis=1)
        return carry

    lax.fori_loop(0, tb, step, 0)

    @pl.when(tblk == pl.num_programs(1) - 1)
    def _():
        for p in range(N_PAIRS):
            sf_ref[0, 2 * p] = s_ref[p, :, :HEAD]
            sf_ref[0, 2 * p + 1] = s_ref[p, :, HEAD:]


def _rwkv_scan(r, w, k, v, a, b, s0, tb):
    bsz, t, _ = r.shape
    seq = pl.BlockSpec((1, tb, MIX), lambda i, j: (i, j, 0))
    st = pl.BlockSpec((1, N_HEADS, HEAD, HEAD), lambda i, j: (i, 0, 0, 0))
    return pl.pallas_call(
        functools.partial(_rwkv_scan_kernel, tb),
        out_shape=(jax.ShapeDtypeStruct((bsz, t, MIX), F32),
                   jax.ShapeDtypeStruct((bsz, N_HEADS, HEAD, HEAD), F32)),
        grid=(bsz, t // tb),
        in_specs=[seq] * 6 + [st],
        out_specs=(seq, st),
        scratch_shapes=[pltpu.VMEM((N_PAIRS, HEAD, LANES), F32)],
        compiler_params=_cparams(("parallel", "arbitrary"), 48),
        name="rwkv_scan",
    )(r, w, k, v, a, b, s0)


CHUNK = 64
PAIR_GROUP = 12


def _rwkv_chunk_kernel(r_ref, w_ref, k_ref, v_ref, a_ref, b_ref, s0_ref, y_ref, sf_ref, s_ref):
    L = CHUNK
    assert 2 * L == LANES

    @pl.when(pl.program_id(0) == 0)
    def _():
        s_ref[...] = s0_ref[...]

    ri = lax.broadcasted_iota(jnp.int32, (L, L), 0)
    ci = lax.broadcasted_iota(jnp.int32, (L, L), 1)
    tri = (ci <= ri).astype(BF16)
    i2 = lax.broadcasted_iota(jnp.int32, (LANES, LANES), 0)
    j2 = lax.broadcasted_iota(jnp.int32, (LANES, LANES), 1)
    top, left = i2 < L, j2 < L
    s_idx, t_idx = j2 & (L - 1), i2 & (L - 1)
    is_top = jnp.where(top, 1, 0)
    keep_a = s_idx < t_idx + (1 - is_top)
    keep_b = s_idx < t_idx + is_top
    ul = top & left
    lr = jnp.logical_not(top | left)
    same_head = ul | lr
    eye = (i2 == j2).astype(F32)
    lo = lax.broadcasted_iota(jnp.int32, (1, LANES), 1) < HEAD
    zeros = jnp.zeros((L, LANES), F32)
    cat = lambda x, y: jnp.concatenate([x, y], axis=0)
    dot = lambda x, y: jnp.dot(x.astype(BF16), y.astype(BF16), preferred_element_type=F32)
    dot_nt = lambda x, y: lax.dot_general(x.astype(BF16), y.astype(BF16), _NT, preferred_element_type=F32)

    pairs = range(N_PAIRS)
    each = lambda f, *cols: [f(*xs) for xs in zip(*cols)]
    cols = [[x[:, p * LANES:(p + 1) * LANES] for p in pairs] for x in (r_ref, w_ref, k_ref, v_ref, a_ref, b_ref)]
    r, lw, k, v, a, b = cols

    def log_cumdecay(lw):
        hi, mid, low = _split3(lw)
        return dot(tri, hi) + dot(tri, mid) + dot(tri, low)

    c = each(log_cumdecay, lw)
    c_last = [x[L - 1:L, :] for x in c]
    at = each(lambda a, c, lw: a * jnp.exp(c - lw), a, c, lw)
    rt = each(lambda r, c: r * jnp.exp(c), r, c)
    em = [jnp.exp(-x) for x in c]
    kh, bh = each(jnp.multiply, k, em), each(jnp.multiply, b, em)
    kb = each(lambda k, b, c, cl: cat(k, b) * jnp.exp(cat(cl - c, cl - c)), k, b, c, c_last)
    vm_a, vm_b = [jnp.where(lo, x, 0.0) for x in v], [jnp.where(lo, 0.0, x) for x in v]
    s = [s_ref[p] for p in pairs]

    def state_products(at, rt, s):
        s_hi = s.astype(BF16)
        ar = cat(at, rt)
        return dot_nt(ar, s_hi) + dot_nt(ar, s - s_hi.astype(F32))

    ps = each(state_products, at, rt, s)
    m_a = each(lambda at, rt, kh, bh: jnp.where(keep_a, dot_nt(jnp.where(lo, cat(at, rt), 0.0), cat(bh, kh)), 0.0),
               at, rt, kh, bh)
    m_b = each(lambda at, rt, kh, bh: jnp.where(keep_b, dot_nt(jnp.where(lo, 0.0, cat(rt, at)), cat(kh, bh)), 0.0),
               at, rt, kh, bh)
    rhs = each(lambda ps, m_a, m_b, vm_a, vm_b: ps[:L] + dot(m_a[:L], cat(zeros, vm_a)) + dot(m_b[L:], cat(vm_b, zeros)),
               ps, m_a, m_b, vm_a, vm_b)

    qs = each(lambda m_a, m_b: jnp.where(ul, m_a, 0.0) + jnp.where(lr, m_b, 0.0), m_a, m_b)
    invs = [eye + q for q in qs]
    for _ in range(5):
        qs = [dot(q, q) for q in qs]
        invs = each(lambda inv, q: inv + dot(inv, q), invs, qs)

    u2 = each(lambda inv, rhs: dot(inv, cat(jnp.where(lo, rhs, 0.0), jnp.where(lo, 0.0, rhs))), invs, rhs)
    y = each(lambda ps, m_a, m_b, u2, vm_a, vm_b: ps[L:] + dot(m_a[L:], cat(u2[:L], vm_a)) + dot(m_b[:L], cat(vm_b, u2[L:])),
             ps, m_a, m_b, u2, vm_a, vm_b)
    for p in pairs:
        y_ref[:, p * LANES:(p + 1) * LANES] = y[p]
    upd = each(lambda v, u2, kb: dot(cat(v, u2[:L] + u2[L:]).T, kb), v, u2, kb)
    for p in pairs:
        s_ref[p] = s[p] * jnp.exp(c_last[p]) + jnp.where(same_head, upd[p], 0.0)

    @pl.when(pl.program_id(0) == pl.num_programs(0) - 1)
    def _():
        sf_ref[...] = s_ref[...]


def _rwkv_chunked(r, w, k, v, a, b, s0):
    t = r.shape[0]
    pair_eye = jnp.eye(2, dtype=F32)[None, :, None, :, None]
    s0p = (s0.reshape(N_PAIRS, 2, HEAD, 1, HEAD) * pair_eye).reshape(N_PAIRS, LANES, LANES)
    seq = pl.BlockSpec((CHUNK, MIX), lambda i: (i, 0))
    st = pl.BlockSpec((N_PAIRS, LANES, LANES), lambda i: (0, 0, 0))
    y, sf = pl.pallas_call(
        _rwkv_chunk_kernel,
        out_shape=(jax.ShapeDtypeStruct((t, MIX), F32), jax.ShapeDtypeStruct((N_PAIRS, LANES, LANES), F32)),
        grid=(t // CHUNK,),
        in_specs=[seq] * 6 + [st],
        out_specs=(seq, st),
        scratch_shapes=[pltpu.VMEM((N_PAIRS, LANES, LANES), F32)],
        compiler_params=_cparams(("arbitrary",), 48),
        name="rwkv_chunked",
    )(r, w, k, v, a, b, s0p)
    sf = sf.reshape(N_PAIRS, 2, HEAD, 2, HEAD)
    sf = jnp.stack([sf[:, 0, :, 0], sf[:, 1, :, 1]], axis=1).reshape(N_HEADS, HEAD, HEAD)
    return y, sf


def _rwkv_post_kernel(y_ref, bonus_ref, g_ref, lw_ref, lb_ref, ones_ref, o_ref):
    y = y_ref[...]
    ones = ones_ref[...]
    m = _dot_exact_rhs(y, ones) * (1.0 / HEAD)
    d = y - m
    var = _dot_exact_rhs(d * d, ones) * (1.0 / HEAD)
    yn = d * lax.rsqrt(var + GN_EPS) * lw_ref[...] + lb_ref[...]
    o_ref[...] = (yn + bonus_ref[...]) * g_ref[...]


def _rwkv_post(y, bonus, g, lnx_w, lnx_b, ones, tm):
    m = y.shape[0]
    blk = pl.BlockSpec((tm, LANES), lambda i, c: (i, c))
    vec = pl.BlockSpec((1, LANES), lambda i, c: (0, c))
    return pl.pallas_call(
        _rwkv_post_kernel,
        out_shape=jax.ShapeDtypeStruct((m, MIX), F32),
        grid=(m // tm, MIX // LANES),
        in_specs=[blk, blk, blk, vec, vec, pl.BlockSpec((LANES, LANES), lambda i, c: (0, 0))],
        out_specs=blk,
        compiler_params=_cparams(("parallel", "parallel"), 32),
        name="rwkv_post",
    )(y, bonus, g, lnx_w, lnx_b, ones)


def _sb_inside(z, lower, allowed, two_pass):
    sp = jnp.maximum(z, 0.0) + jnp.log(1.0 + jnp.exp(-jnp.abs(z)))
    ls = z - sp
    if allowed is not None:
        sp = jnp.where(allowed, sp, 0.0)
    lhs = sp.astype(BF16)
    if two_pass:
        lhs = jnp.concatenate([lhs, (sp - lhs.astype(F32)).astype(BF16)], axis=1)
    right = jnp.dot(lhs, lower, preferred_element_type=F32)
    return ls, right, right[:, 0:1] + sp[:, 0:1]


def _sb_weights(scores, carry, allowed):
    ls, right, total = scores
    a = jnp.exp(ls - right - carry)
    if allowed is not None:
        a = jnp.where(allowed, a, 0.0)
    return a, carry + total


def _sb_tile(z, carry, lower, allowed, two_pass):
    return _sb_weights(_sb_inside(z, lower, allowed, two_pass), carry, allowed)


def _sb_prompt_kernel(tq, q_ref, k_ref, v_ref, up_ref, o_ref, kt_ref, vt_ref, kb_ref, va_ref, vb_ref, acc_ref,
                      z_ref, p_ref, c_ref):
    qi = pl.program_id(1)
    lane = lax.broadcasted_iota(jnp.int32, (1, LANES), 1)
    lo = lane < HEAD

    @pl.when(qi == 0)
    def _():
        kb_ref[...] = k_ref[...].astype(BF16)
        v = v_ref[...]
        va_ref[...] = jnp.where(lo, v, 0.0).astype(BF16)
        vb_ref[...] = jnp.where(lo, 0.0, v).astype(BF16)

        def to_cache_layout(g, carry):
            rows = pl.ds(pl.multiple_of(g * PAGE, PAGE), PAGE)
            kt_ref[g] = k_ref[rows, :].T
            vt_ref[g] = v_ref[rows, :].T
            return carry

        lax.fori_loop(0, kt_ref.shape[0], to_cache_layout, 0)

    q = q_ref[...] * (HEAD ** -0.5)
    qa = jnp.where(lo, q, 0.0).astype(BF16)
    qb = jnp.where(lo, 0.0, q).astype(BF16)
    lower = up_ref[...]

    tk = tq // 2

    def logits_to(j, slot):
        kblk = kb_ref[pl.ds(pl.multiple_of(jnp.maximum(j, 0) * tk, tk), tk), :]
        z_ref[slot, 0] = lax.dot_general(qa, kblk, _NT, preferred_element_type=F32)
        z_ref[slot, 1] = lax.dot_general(qb, kblk, _NT, preferred_element_type=F32)

    def accumulate(j, slot):
        ks = pl.multiple_of(j * tk, tk)
        acc_ref[...] += (jnp.dot(p_ref[slot, 0], va_ref[pl.ds(ks, tk), :], preferred_element_type=F32)
                         + jnp.dot(p_ref[slot, 1], vb_ref[pl.ds(ks, tk), :], preferred_element_type=F32))

    def weights_to(za, zb, slot, allowed):
        pa, ca = _sb_tile(za, c_ref[0], lower, allowed, False)
        pb, cb = _sb_tile(zb, c_ref[1], lower, allowed, False)
        p_ref[slot, 0] = pa.astype(BF16)
        p_ref[slot, 1] = pb.astype(BF16)
        c_ref[0] = ca
        c_ref[1] = cb

    def step(j, src, dst, allowed=None, first=False):
        if not first:
            accumulate(j + 1, src)
        za, zb = z_ref[src, 0], z_ref[src, 1]
        logits_to(j - 1, dst)
        weights_to(za, zb, dst, allowed)

    acc_ref[...] = jnp.zeros_like(acc_ref)
    c_ref[...] = jnp.zeros_like(c_ref)
    col = lax.broadcasted_iota(jnp.int32, (tq, tk), 1)
    row = lax.broadcasted_iota(jnp.int32, (tq, tk), 0)
    jd = 2 * qi
    logits_to(jd + 1, 0)
    step(jd + 1, 0, 1, col + tk < row, first=True)
    step(jd, 1, 0, col < row)

    def pair(n, carry):
        j = jd - 1 - 2 * n
        step(j, 0, 1)
        step(j - 1, 1, 0)
        return carry

    lax.fori_loop(0, qi, pair, 0)
    accumulate(0, 0)
    o_ref[...] = acc_ref[...]


def _sb_prompt(z, upper, tq):
    s = z.shape[0]
    nc = MIX // LANES
    return pl.pallas_call(
        functools.partial(_sb_prompt_kernel, tq),
        out_shape=(jax.ShapeDtypeStruct((s, MIX), F32),) + (jax.ShapeDtypeStruct((s // PAGE, MIX, PAGE), F32),) * 2,
        grid=(nc, s // tq),
        in_specs=[
            pl.BlockSpec((tq, LANES), lambda c, i: (i, c)),
            pl.BlockSpec((s, LANES), lambda c, i: (0, nc + c)),
            pl.BlockSpec((s, LANES), lambda c, i: (0, 2 * nc + c)),
            pl.BlockSpec((tq // 2, tq // 2), lambda c, i: (0, 0)),
        ],
        out_specs=(pl.BlockSpec((tq, LANES), lambda c, i: (i, c)),)
                  + (pl.BlockSpec((s // PAGE, LANES, PAGE), lambda c, i: (0, c, 0)),) * 2,
        scratch_shapes=[pltpu.VMEM((s, LANES), BF16)] * 3
                       + [pltpu.VMEM((tq, LANES), F32),
                          pltpu.VMEM((2, 2, tq, tq // 2), F32),
                          pltpu.VMEM((2, 2, tq, tq // 2), BF16),
                          pltpu.VMEM((2, tq, 1), F32)],
        compiler_params=_cparams(("parallel", "arbitrary"), 56),
        name="sb_prompt",
    )(z, z, z, upper)


QROWS = 8
SB_PAGE_GROUP = 8


def _sb_sample_kernel(group, pt_ref, q_ref, kn_ref, vn_ref, *refs):
    kc_refs, vc_refs = refs[:group], refs[group:2 * group]
    up_ref, o_ref, acc_ref, carry_ref = refs[2 * group:]
    p = pl.program_id(1)
    lower2 = up_ref[...]
    rows = N_HEADS * QROWS

    def tiles(k_refs, v_refs, carry, allowed):
        zs = [jnp.concatenate([jnp.dot(q_ref[0, h], k_ref[0, h].astype(BF16), preferred_element_type=F32)
                               for h in range(N_HEADS)], axis=0) for k_ref in k_refs]
        scores = [_sb_inside(z, lower2, allowed, True) for z in zs]
        total = None
        for score, v_ref in zip(scores, v_refs):
            a, carry = _sb_weights(score, carry, allowed)
            outs = [lax.dot_general(a[h * QROWS:(h + 1) * QROWS].astype(BF16), v_ref[0, h].astype(BF16),
                                    _NT, preferred_element_type=F32) for h in range(N_HEADS)]
            total = outs if total is None else [t + o for t, o in zip(total, outs)]
        return total, carry

    @pl.when(p == 0)
    def _():
        key = lax.broadcasted_iota(jnp.int32, (rows, PAGE), 1)
        qry = lax.broadcasted_iota(jnp.int32, (rows, PAGE), 0) & (QROWS - 1)
        outs, carry = tiles([kn_ref], [vn_ref], jnp.zeros((rows, 1), F32), key < qry)
        carry_ref[...] = carry
        for h in range(N_HEADS):
            acc_ref[h] = outs[h]

    @pl.when(p > 0)
    def _():
        outs, carry = tiles(kc_refs, vc_refs, carry_ref[...], None)
        carry_ref[...] = carry
        for h in range(N_HEADS):
            acc_ref[h] += outs[h]

    @pl.when(p == pl.num_programs(1) - 1)
    def _():
        o_ref[0] = acc_ref[...]


def _sb_sample(q, k_new, v_new, cache_k, cache_v, page_table, lower2):
    bsz = q.shape[0]
    n_pages = page_table.shape[1]
    group = next(g for g in (SB_PAGE_GROUP, 4, 2, 1) if n_pages % g == 0)

    def page(g):
        return lambda b, p, pt: (pt[b, n_pages - 1 - (jnp.maximum(p, 1) - 1) * group - g], 0, 0, 0)

    own = lambda b, p, pt: (b, 0, 0, 0)
    kv_block = (1, N_HEADS, HEAD, PAGE)
    grid_spec = pltpu.PrefetchScalarGridSpec(
        num_scalar_prefetch=1,
        grid=(bsz, n_pages // group + 1),
        in_specs=[pl.BlockSpec((1, N_HEADS, QROWS, HEAD), own), pl.BlockSpec(kv_block, own),
                  pl.BlockSpec(kv_block, own)]
                 + [pl.BlockSpec(kv_block, page(g)) for g in range(group)] * 2
                 + [pl.BlockSpec((2 * PAGE, PAGE), lambda b, p, pt: (0, 0))],
        out_specs=pl.BlockSpec((1, N_HEADS, QROWS, HEAD), own),
        scratch_shapes=[pltpu.VMEM((N_HEADS, QROWS, HEAD), F32), pltpu.VMEM((N_HEADS * QROWS, 1), F32)],
    )
    return pl.pallas_call(
        functools.partial(_sb_sample_kernel, group),
        out_shape=jax.ShapeDtypeStruct((bsz, N_HEADS, QROWS, HEAD), F32),
        grid_spec=grid_spec,
        compiler_params=_cparams(("parallel", "arbitrary"), 48),
        name="sb_sample",
    )(page_table, q, k_new, v_new, *([cache_k] * group), *([cache_v] * group), lower2)


def _rmsnorm_kernel(x_ref, g_ref, o_ref):
    o_ref[...] = _rms_rows(x_ref[...], g_ref[...])


def _rmsnorm(x, g, tm):
    m, d = x.shape
    return pl.pallas_call(
        _rmsnorm_kernel,
        out_shape=jax.ShapeDtypeStruct((m, d), F32),
        grid=(m // tm,),
        in_specs=[pl.BlockSpec((tm, d), lambda i: (i, 0)), pl.BlockSpec((1, d), lambda i: (0, 0))],
        out_specs=pl.BlockSpec((tm, d), lambda i: (i, 0)),
        compiler_params=_cparams(("parallel",), 32),
        name="rmsnorm",
    )(x, g.reshape(1, d))


def _pad_cols(x, width):
    return jnp.pad(x, ((0, 0), (0, width - x.shape[1])))


def _rwkv_cols(x):
    o = 3 * MIX
    parts = [x[:, :o],
             _pad_cols(x[:, o:o + DECAY_LORA], LANES),
             _pad_cols(x[:, o + DECAY_LORA:o + DECAY_LORA + ICLR_LORA], LANES),
             _pad_cols(x[:, o + DECAY_LORA + ICLR_LORA:RWKV_PROJ], 2 * LANES)]
    if x.shape[1] > RWKV_PROJ:
        parts.append(x[:, RWKV_PROJ:])
    return jnp.concatenate(parts, axis=1)


def _rwkv_uncols(x):
    o = 3 * MIX
    return jnp.concatenate([x[:, :o], x[:, o:o + DECAY_LORA], x[:, o + LANES:o + LANES + ICLR_LORA],
                            x[:, o + 2 * LANES:o + 2 * LANES + GATE_LORA]], axis=1)


def _pad_rows(x, rows):
    return jnp.pad(x, ((0, rows - x.shape[0]), (0, 0)))


def _tile(m, pref):
    return pref if m % pref == 0 else m


def kernel(x_prompt, x_sample, mem_prompt, state_rwkv_wkv, state_rwkv_shift, cache_sb_k, cache_sb_v, page_table, cache_mem_k, cache_mem_v, norm_mix, norm_ffn, norm_final, mem_norm, w_mem_kv, w_out, w_ffn_up, w_ffn_down, rwkv_w_in, rwkv_mu, rwkv_w0, rwkv_w2, rwkv_a0, rwkv_a2, rwkv_g2, rwkv_k_k, rwkv_k_a, rwkv_r_k, rwkv_lnx_w, rwkv_lnx_b, sb_w_in, sb_b_qk):
    bp, s, d = x_prompt.shape
    db, t_new, _ = x_sample.shape
    assert bp == 1, "the prompt path handles a single sequence"
    depth = norm_mix.shape[0]
    ms = db * t_new
    n_mem = mem_prompt.shape[1]
    xp = x_prompt.reshape(s, d)
    xs = x_sample.reshape(ms, d)
    tmp = _tile(s, 512)
    tms = ms
    tmp_in = _tile(s, 1024)
    col_tile = lambda n: 1024 if n % 1024 == 0 else 512

    idx = lax.broadcasted_iota(jnp.int32, (LANES, LANES), 0)
    jdx = lax.broadcasted_iota(jnp.int32, (LANES, LANES), 1)
    head_ones = ((idx // HEAD) == (jdx // HEAD)).astype(BF16)
    tq = _tile(s, 512)
    iq = lax.broadcasted_iota(jnp.int32, (tq // 2, tq // 2), 0)
    jq = lax.broadcasted_iota(jnp.int32, (tq // 2, tq // 2), 1)
    lower_q = (iq > jq).astype(BF16)
    lower_p = jnp.tile((idx > jdx).astype(BF16), (2, 1))

    wu_all, wd_all = w_ffn_up.astype(BF16), w_ffn_down.astype(BF16)
    cmk_rows = cache_mem_k.reshape(depth * db, n_mem * MEM_HEADS, MEM_HD)
    cmv_rows = cache_mem_v.reshape(depth * db, n_mem * MEM_HEADS, MEM_HD)
    mem_k, mem_v = [], []
    wkv_p, shift_p, wkv_s, shift_s = [], [], [], []
    sbk_p, sbv_p, sbk_s, sbv_s = [], [], [], []
    for i in range(depth):
        j = i // 2
        kv = _rms_matmul(mem_prompt.reshape(n_mem, d), mem_norm[i], w_mem_kv[i].astype(BF16),
                         jnp.zeros((2 * MEM_W,), F32), n_mem, 2 * MEM_W)
        mk, mv = kv[:, :MEM_W], kv[:, MEM_W:]
        mem_k.append(mk.reshape(1, n_mem, MEM_HEADS, MEM_HD))
        mem_v.append(mv.reshape(1, n_mem, MEM_HEADS, MEM_HD))
        if i % 2 == 0:
            w_in = _rwkv_cols(rwkv_w_in[j]).astype(BF16)
            n_in = w_in.shape[1]
            zero_b = jnp.zeros((n_in,), F32)
            zp = _rms_matmul(xp, norm_mix[i], w_in, zero_b, tmp_in, col_tile(n_in))
            zs = _rms_matmul(xs, norm_mix[i], w_in, zero_b, tms, col_tile(n_in))
            shift_p.append(_rwkv_uncols(zp[s - 1:s]))
            shift_s.append(_rwkv_uncols(zs.reshape(db, t_new, n_in)[:, t_new - 1]))
            wide = 3 * MIX + LORA_PAD
            mu = _rwkv_cols(rwkv_mu[j][None])
            vecs = [x.reshape(1, MIX) for x in (rwkv_w0[j], rwkv_a0[j], rwkv_k_k[j], rwkv_k_a[j], rwkv_r_k[j])]
            lora = (_pad_rows(rwkv_w2[j], LANES).astype(BF16), _pad_rows(rwkv_a2[j], LANES).astype(BF16),
                    _pad_rows(rwkv_g2[j], 2 * LANES).astype(BF16))
            lnx = (rwkv_lnx_w[j].reshape(1, MIX), rwkv_lnx_b[j].reshape(1, MIX))

            def rwkv(z, shift_rows, seq_len, tm, s0, nb, tb, tpost):
                r, w, k, v, a, b, g, bonus = _rwkv_prep(z, shift_rows, seq_len, tm, mu, *vecs, *lora, head_ones)
                if nb == 1 and seq_len % CHUNK == 0:
                    y, s_fin = _rwkv_chunked(r, w, k, v, a, b, s0[0])
                    s_fin = s_fin[None]
                else:
                    sh = lambda x: x.reshape(nb, seq_len, MIX)
                    y, s_fin = _rwkv_scan(sh(r), sh(w), sh(k), sh(v), sh(a), sh(b), s0, tb)
                return _rwkv_post(y.reshape(nb * seq_len, MIX), bonus, g, *lnx, head_ones, tpost), s_fin

            zero_shift = jnp.zeros((8, wide), F32)
            zero_state = jnp.zeros((1, N_HEADS, HEAD, HEAD), F32)
            mix_p, st_p = rwkv(zp, zero_shift, s, tmp_in, zero_state, 1, _tile(s, 256), tmp_in)
            shift_rows = jnp.repeat(_rwkv_cols(state_rwkv_shift[j]), t_new, axis=0)
            mix_s, st_s = rwkv(zs, shift_rows, t_new, tms, state_rwkv_wkv[j], db, t_new, tms)
            wkv_p.append(st_p)
            wkv_s.append(st_s)
            q_off = wide
        else:
            w_in = sb_w_in[j].astype(BF16)
            bias = jnp.concatenate([sb_b_qk[j], jnp.zeros((w_in.shape[1] - 2 * MIX,), F32)])
            zp = _rms_matmul(xp, norm_mix[i], w_in, bias, tmp_in, col_tile(w_in.shape[1]))
            zs = _rms_matmul(xs, norm_mix[i], w_in, bias, tms, col_tile(w_in.shape[1]))
            mix_p, kt_p, vt_p = _sb_prompt(zp, lower_q, tq)
            assert t_new < QROWS
            heads = lambda x: x.reshape(db, t_new, N_HEADS, HEAD)
            q_s, k_s, v_s = heads(zs[:, :MIX]), heads(zs[:, MIX:2 * MIX]), heads(zs[:, 2 * MIX:3 * MIX])
            q_hq = jnp.pad((q_s * (HEAD ** -0.5)).transpose(0, 2, 1, 3),
                           ((0, 0), (0, 0), (0, QROWS - t_new), (0, 0))).astype(BF16)
            keys_last = lambda x: x.transpose(0, 2, 3, 1)
            pad_new = lambda x: keys_last(jnp.pad(x, ((0, 0), (0, PAGE - t_new), (0, 0), (0, 0))))
            o_hq = _sb_sample(q_hq, pad_new(k_s), pad_new(v_s), keys_last(cache_sb_k[j]),
                              keys_last(cache_sb_v[j]), page_table, lower_p)
            mix_s = o_hq[:, :, :t_new].transpose(0, 2, 1, 3).reshape(ms, MIX)
            as_cache = lambda x: x.reshape(1, s // PAGE, N_HEADS, HEAD, PAGE).transpose(0, 1, 4, 2, 3)
            sbk_p.append(as_cache(kt_p))
            sbv_p.append(as_cache(vt_p))
            sbk_s.append(k_s)
            sbv_s.append(v_s)
            q_off = 3 * MIX
        cp = _mem_attend(zp[:, q_off:q_off + MEM_W][None], mk[None], mv[None], tmp)[0]
        cs = _mem_attend(zs[:, q_off:q_off + MEM_W].reshape(db, t_new, MEM_W), cmk_rows, cmv_rows, t_new,
                         first=i * db).reshape(ms, MEM_W)
        wo = w_out[i].astype(BF16)
        xp = _outproj(xp, mix_p, cp, wo[:MIX], wo[MIX:], tmp)
        xs = _outproj(xs, mix_s, cs, wo[:MIX], wo[MIX:], tms)
        xp = _ffn(xp, norm_ffn[i], wu_all, wd_all, i, tmp, col_tile(wu_all.shape[2]))
        xs = _ffn(xs, norm_ffn[i], wu_all, wd_all, i, tms, col_tile(wu_all.shape[2]))
    y_prompt = _rmsnorm(xp, norm_final, tmp).reshape(1, s, d)
    y_sample = _rmsnorm(xs, norm_final, tms).reshape(db, t_new, d)
    return (y_prompt, y_sample, jnp.stack(mem_k), jnp.stack(mem_v),
            jnp.stack(wkv_p), jnp.stack(shift_p), jnp.stack(wkv_s), jnp.stack(shift_s),
            jnp.stack(sbk_p), jnp.stack(sbv_p), jnp.stack(sbk_s), jnp.stack(sbv_s))
```
